```python
import math
import jax, jax.numpy as jnp
from jax import lax
import numpy as np

D_MODEL = 4096
BATCH = 4
SEQ = 4096
DEPTH = 2

CHUNK = 64
Q_BLOCK = 128
PLE_DIM = 256

SSM_WIDTH = D_MODEL // 2
SSM_HEAD_DIM = 64
SSM_HEADS = SSM_WIDTH // SSM_HEAD_DIM
SSM_GROUPS = 8
SSM_STATE = 128
SSM_CONV = 4
SSM_XBC = SSM_WIDTH + 2 * SSM_GROUPS * SSM_STATE

DIFF_WIDTH = D_MODEL // 4
DIFF_HEAD_DIM = 64
DIFF_HEADS = DIFF_WIDTH // (2 * DIFF_HEAD_DIM)

MLSTM_WIDTH = D_MODEL // 4
MLSTM_HEAD_DIM = 128
MLSTM_HEADS = MLSTM_WIDTH // MLSTM_HEAD_DIM

MIX_WIDTH = SSM_WIDTH + DIFF_WIDTH + MLSTM_WIDTH
N_BRANCH = 3

SPLIT_SIZES = (
    SSM_XBC, SSM_WIDTH, SSM_HEADS,
    DIFF_WIDTH, DIFF_WIDTH, DIFF_WIDTH, DIFF_WIDTH,
    MLSTM_WIDTH, MLSTM_WIDTH, MLSTM_WIDTH, MLSTM_WIDTH, MLSTM_WIDTH,
    MLSTM_HEADS, MLSTM_HEADS,
    N_BRANCH * D_MODEL,
)
IN_WIDTH = sum(SPLIT_SIZES)

DEEPNORM_ALPHA = (2.0 * DEPTH) ** 0.25
DEEPNORM_BETA = (8.0 * DEPTH) ** -0.25

kernel_name = "hybrid_ssd_diffattn_mlstm_deepnorm"


def _split_cols(t, sizes):
    idx, acc = [], 0
    for s in sizes[:-1]:
        acc += s
        idx.append(acc)
    return jnp.split(t, idx, axis=-1)


def _rmsnorm(x, g, eps=1e-5):
    xf = x.astype(jnp.float32)
    y = xf * lax.rsqrt(jnp.mean(xf * xf, axis=-1, keepdims=True) + eps)
    return y * g.astype(jnp.float32)


def _layernorm(x, g, b, eps=1e-5):
    xf = x.astype(jnp.float32)
    mu = jnp.mean(xf, axis=-1, keepdims=True)
    var = jnp.mean(jnp.square(xf - mu), axis=-1, keepdims=True)
    return (xf - mu) * lax.rsqrt(var + eps) * g.astype(jnp.float32) + b.astype(jnp.float32)


def _causal_dwconv(u, w, b):
    k = w.shape[0]
    out = lax.conv_general_dilated(u, w[:, None, :], window_strides=(1,), padding=[(k - 1, 0)],
                                   dimension_numbers=('NWC', 'WIO', 'NWC'),
                                   feature_group_count=u.shape[-1])
    return out + b


def _ssd(xh, dt, a, bmat, cmat):
    bsz, s, h, pdim = xh.shape
    g, n = bmat.shape[2], bmat.shape[3]
    r = h // g
    nc = s // CHUNK
    f32 = jnp.float32
    xdt = (xh.astype(f32) * dt[..., None]).reshape(bsz, nc, CHUNK, g, r, pdim)
    la = (dt * a).reshape(bsz, nc, CHUNK, g, r)
    bc = bmat.astype(f32).reshape(bsz, nc, CHUNK, g, n)
    cc = cmat.astype(f32).reshape(bsz, nc, CHUNK, g, n)
    la_cum = jnp.cumsum(la, axis=2)
    causal = jnp.tril(jnp.ones((CHUNK, CHUNK), bool))[:, :, None, None]
    seg = la_cum[:, :, :, None] - la_cum[:, :, None, :]
    decay = jnp.where(causal, jnp.exp(jnp.where(causal, seg, 0.0)), 0.0)
    cb = jnp.einsum('bclgn,bcsgn->bclsg', cc, bc)
    y_diag = jnp.einsum('bclsgr,bcsgrp->bclgrp', cb[..., None] * decay, xdt)
    decay_to_end = jnp.exp(la_cum[:, :, -1:] - la_cum)
    states = jnp.einsum('bclgn,bclgrp->bcgrpn', bc, xdt * decay_to_end[..., None])
    chunk_decay = jnp.exp(la_cum[:, :, -1])

    def step(carry, inp):
        st, dec = inp
        return carry * dec[..., None, None] + st, carry

    init = jnp.zeros((bsz, g, r, pdim, n), f32)
    _, prev = lax.scan(step, init, (jnp.moveaxis(states, 1, 0), jnp.moveaxis(chunk_decay, 1, 0)))
    prev = jnp.moveaxis(prev, 0, 1)
    y_off = jnp.einsum('bclgn,bcgrpn->bclgrp', cc, prev) * jnp.exp(la_cum)[..., None]
    return (y_diag + y_off).reshape(bsz, s, h, pdim)


def _diff_attention(q, k, v, lam, slopes):
    bsz, s, h = q.shape[:3]
    nb = s // Q_BLOCK
    scale = DIFF_HEAD_DIM ** -0.5
    kpos = jnp.arange(s)
    qb = jnp.moveaxis(q.reshape(bsz, nb, Q_BLOCK, h, 2, DIFF_HEAD_DIM), 1, 0)

    def block(args):
        qblk, bi = args
        qpos = bi * Q_BLOCK + jnp.arange(Q_BLOCK)
        visible = (kpos[None, :] // CHUNK) <= (qpos[:, None] // CHUNK)
        dist = jnp.abs(qpos[:, None] - kpos[None, :]).astype(jnp.float32)
        bias = -slopes[:, None, None] * dist
        sc = jnp.einsum('bqhcd,bkhcd->bhcqk', qblk, k,
                        preferred_element_type=jnp.float32) * scale + bias[None, :, None]
        sc = jnp.where(visible, sc, -jnp.inf)
        pr = jax.nn.softmax(sc, axis=-1)
        attn = pr[:, :, 0] - lam * pr[:, :, 1]
        return jnp.einsum('bhqk,bkhe->bqhe', attn.astype(v.dtype), v)

    out = lax.map(block, (qb, jnp.arange(nb)))
    return jnp.moveaxis(out, 0, 1).reshape(bsz, s, h, 2 * DIFF_HEAD_DIM)


def _mlstm(q, k, v, i_raw, f_raw):
    f32 = jnp.float32
    bsz, s, h, d = q.shape
    nc = s // CHUNK
    qc = q.astype(f32).reshape(bsz, nc, CHUNK, h, d)
    kc = (k.astype(f32) * d ** -0.5).reshape(bsz, nc, CHUNK, h, d)
    vc = v.astype(f32).reshape(bsz, nc, CHUNK, h, d)
    logf = jax.nn.log_sigmoid(f_raw.astype(f32)).reshape(bsz, nc, CHUNK, h)
    ig = i_raw.astype(f32).reshape(bsz, nc, CHUNK, h)
    bcum = jnp.cumsum(logf, axis=2)
    btot = bcum[:, :, -1]
    w_end = btot[:, :, None] - bcum + ig
    m_loc = jnp.max(w_end, axis=2)
    e_end = jnp.exp(w_end - m_loc[:, :, None])
    c_loc = jnp.einsum('bcjhv,bcjhk->bchvk', vc * e_end[..., None], kc)
    n_loc = jnp.einsum('bcjh,bcjhk->bchk', e_end, kc)

    def step(carry, inp):
        cm, nm, mm = carry
        cl, nl, ml, bt = inp
        m_new = jnp.maximum(bt + mm, ml)
        a_old = jnp.exp(bt + mm - m_new)
        a_loc = jnp.exp(ml - m_new)
        c_new = a_old[..., None, None] * cm + a_loc[..., None, None] * cl
        n_new = a_old[..., None] * nm + a_loc[..., None] * nl
        return (c_new, n_new, m_new), (cm, nm, mm)

    init = (jnp.zeros((bsz, h, d, d), f32), jnp.zeros((bsz, h, d), f32), jnp.zeros((bsz, h), f32))
    xs = tuple(jnp.moveaxis(t, 1, 0) for t in (c_loc, n_loc, m_loc, btot))
    _, (c_prev, n_prev, m_prev) = lax.scan(step, init, xs)
    c_prev = jnp.moveaxis(c_prev, 0, 1)
    n_prev = jnp.moveaxis(n_prev, 0, 1)
    m_prev = jnp.moveaxis(m_prev, 0, 1)
    causal = jnp.tril(jnp.ones((CHUNK, CHUNK), bool))[:, :, None]
    dmat = bcum[:, :, :, None] - bcum[:, :, None, :] + ig[:, :, None, :]
    dmat = jnp.where(causal, dmat, -jnp.inf)
    m_inter = bcum + m_prev[:, :, None]
    m_row = jnp.maximum(m_inter, jnp.max(dmat, axis=3))
    wts = jnp.exp(dmat - m_row[:, :, :, None])
    sw = wts * jnp.einsum('bcihd,bcjhd->bcijh', qc, kc)
    num = jnp.einsum('bcijh,bcjhd->bcihd', sw, vc)
    den = jnp.sum(sw, axis=3)
    a_inter = jnp.exp(m_inter - m_row)
    num = num + jnp.einsum('bcihk,bchvk->bcihv', qc, c_prev) * a_inter[..., None]
    den = den + jnp.einsum('bcihk,bchk->bcih', qc, n_prev) * a_inter
    hout = num / jnp.maximum(jnp.abs(den), jnp.exp(-m_row))[..., None]
    return hout.reshape(bsz, s, h, d)


def _layer(x, p_i, li, w_in, conv_w, conv_b, dt_bias, a_log, d_skip, ssm_norm_g,
           diff_lambda, diff_norm_g, mlstm_gate_b, w_branch, w_out, ln_g, ln_b,
           w_ple, ple_norm_g, w_ple_gate):
    bsz, s, _ = x.shape
    f32 = jnp.float32
    proj = x @ w_in
    (xbc, z_a, dt_raw, q_b, k_b, v_b, z_b, q_c, k_c, v_c, o_c, z_c, i_c, f_c,
     gates) = _split_cols(proj, SPLIT_SIZES)

    xbc = jax.nn.silu(_causal_dwconv(xbc, conv_w, conv_b))
    x_a, b_a, c_a = _split_cols(xbc, (SSM_WIDTH, SSM_GROUPS * SSM_STATE, SSM_GROUPS * SSM_STATE))
    dt = jax.nn.softplus(dt_raw.astype(f32) + dt_bias.astype(f32))
    a = -jnp.exp(a_log.astype(f32))
    xh = x_a.reshape(bsz, s, SSM_HEADS, SSM_HEAD_DIM)
    y = _ssd(xh, dt, a, b_a.reshape(bsz, s, SSM_GROUPS, SSM_STATE),
             c_a.reshape(bsz, s, SSM_GROUPS, SSM_STATE))
    y = (y + d_skip.astype(f32)[:, None] * xh.astype(f32)).reshape(bsz, s, SSM_WIDTH)
    out_a = _rmsnorm(y * jax.nn.silu(z_a.astype(f32)), ssm_norm_g).astype(x.dtype)

    lam_init = 0.8 - 0.6 * math.exp(-0.3 * li)
    dl = diff_lambda.astype(f32)
    lam = jnp.exp(jnp.sum(dl[0] * dl[1])) - jnp.exp(jnp.sum(dl[2] * dl[3])) + lam_init
    slopes = 2.0 ** (-8.0 * jnp.arange(1, DIFF_HEADS + 1, dtype=f32) / DIFF_HEADS)
    qd = q_b.reshape(bsz, s, DIFF_HEADS, 2, DIFF_HEAD_DIM)
    kd = k_b.reshape(bsz, s, DIFF_HEADS, 2, DIFF_HEAD_DIM)
    vd = v_b.reshape(bsz, s, DIFF_HEADS, 2 * DIFF_HEAD_DIM)
    od = _rmsnorm(_diff_attention(qd, kd, vd, lam, slopes), diff_norm_g) * (1.0 - lam_init)
    out_b = (od.reshape(bsz, s, DIFF_WIDTH) * jax.nn.silu(z_b.astype(f32))).astype(x.dtype)

    i_raw = i_c.astype(f32) + mlstm_gate_b[0].astype(f32)
    f_raw = f_c.astype(f32) + mlstm_gate_b[1].astype(f32)
    hm = _mlstm(q_c.reshape(bsz, s, MLSTM_HEADS, MLSTM_HEAD_DIM),
                k_c.reshape(bsz, s, MLSTM_HEADS, MLSTM_HEAD_DIM),
                v_c.reshape(bsz, s, MLSTM_HEADS, MLSTM_HEAD_DIM), i_raw, f_raw)
    out_c = (jax.nn.sigmoid(o_c.astype(f32)) * hm.reshape(bsz, s, MLSTM_WIDTH)
             * jax.nn.silu(z_c.astype(f32))).astype(x.dtype)

    y_a = out_a @ w_branch[:SSM_WIDTH]
    y_b = out_b @ w_branch[SSM_WIDTH:SSM_WIDTH + DIFF_WIDTH]
    y_c = out_c @ w_branch[SSM_WIDTH + DIFF_WIDTH:]
    g = jax.nn.sigmoid(gates.astype(f32)).reshape(bsz, s, N_BRANCH, D_MODEL)
    merged = (g[:, :, 0] * y_a + g[:, :, 1] * y_b + g[:, :, 2] * y_c).astype(x.dtype)
    sub = merged @ w_out

    h = _layernorm(DEEPNORM_ALPHA * x + sub, ln_g, ln_b).astype(x.dtype)

    e = _rmsnorm(p_i @ w_ple, ple_norm_g)
    gate = jax.nn.sigmoid((h @ w_ple_gate).astype(f32))
    return (h.astype(f32) + gate * e).astype(x.dtype)


def setup_inputs(seed: int = 0) -> dict:
    key = jax.random.key(seed)
    ks = jax.random.split(key, 24)
    f32 = jnp.float32
    nrm = lambda k, shape: jax.random.normal(k, shape, f32)
    x = nrm(ks[0], (BATCH, SEQ, D_MODEL))
    p = nrm(ks[1], (DEPTH, BATCH, SEQ, PLE_DIM))
    w_in = nrm(ks[2], (DEPTH, D_MODEL, IN_WIDTH)) * D_MODEL ** -0.5
    conv_w = nrm(ks[3], (DEPTH, SSM_CONV, SSM_XBC)) * SSM_CONV ** -0.5
    conv_b = 0.01 * nrm(ks[4], (DEPTH, SSM_XBC))
    log_dt = jax.random.uniform(ks[5], (DEPTH, SSM_HEADS), f32, math.log(1e-3), math.log(1e-1))
    dt0 = jnp.exp(log_dt)
    dt_bias = dt0 + jnp.log(-jnp.expm1(-dt0))
    a_log = jnp.log(jax.random.uniform(ks[6], (DEPTH, SSM_HEADS), f32, 1.0, 16.0))
    d_skip = 1.0 + 0.1 * nrm(ks[7], (DEPTH, SSM_HEADS))
    ssm_norm_g = 1.0 + 0.02 * nrm(ks[8], (DEPTH, SSM_WIDTH))
    diff_lambda = 0.1 * nrm(ks[9], (DEPTH, 4, DIFF_HEAD_DIM))
    diff_norm_g = 1.0 + 0.02 * nrm(ks[10], (DEPTH, 2 * DIFF_HEAD_DIM))
    i_bias = -3.0 + 0.1 * nrm(ks[11], (DEPTH, MLSTM_HEADS))
    f_bias = jnp.linspace(3.0, 6.0, MLSTM_HEADS, dtype=f32)[None] + 0.1 * nrm(ks[12], (DEPTH, MLSTM_HEADS))
    mlstm_gate_b = jnp.stack([i_bias, f_bias], axis=1)
    row_scale = jnp.concatenate([jnp.full((SSM_WIDTH,), SSM_WIDTH ** -0.5, f32),
                                 jnp.full((DIFF_WIDTH,), DIFF_WIDTH ** -0.5, f32),
                                 jnp.full((MLSTM_WIDTH,), MLSTM_WIDTH ** -0.5, f32)])
    w_branch = nrm(ks[13], (DEPTH, MIX_WIDTH, D_MODEL)) * row_scale[None, :, None] * DEEPNORM_BETA
    w_out = nrm(ks[14], (DEPTH, D_MODEL, D_MODEL)) * (D_MODEL ** -0.5 * DEEPNORM_BETA)
    ln_g = 1.0 + 0.02 * nrm(ks[15], (DEPTH, D_MODEL))
    ln_b = 0.02 * nrm(ks[16], (DEPTH, D_MODEL))
    w_ple = nrm(ks[17], (DEPTH, PLE_DIM, D_MODEL)) * PLE_DIM ** -0.5
    ple_norm_g = 1.0 + 0.02 * nrm(ks[18], (DEPTH, D_MODEL))
    w_ple_gate = nrm(ks[19], (DEPTH, D_MODEL, D_MODEL)) * D_MODEL ** -0.5
    return {"x": x, "p": p, "w_in": w_in, "conv_w": conv_w, "conv_b": conv_b,
            "dt_bias": dt_bias, "a_log": a_log, "d_skip": d_skip, "ssm_norm_g": ssm_norm_g,
            "diff_lambda": diff_lambda, "diff_norm_g": diff_norm_g, "mlstm_gate_b": mlstm_gate_b,
            "w_branch": w_branch, "w_out": w_out, "ln_g": ln_g, "ln_b": ln_b,
            "w_ple": w_ple, "ple_norm_g": ple_norm_g, "w_ple_gate": w_ple_gate}


def reference(x, p, w_in, conv_w, conv_b, dt_bias, a_log, d_skip, ssm_norm_g,
              diff_lambda, diff_norm_g, mlstm_gate_b, w_branch, w_out, ln_g, ln_b,
              w_ple, ple_norm_g, w_ple_gate):
    h = x
    for li in range(DEPTH):
        h = _layer(h, p[li], li, w_in[li], conv_w[li], conv_b[li], dt_bias[li], a_log[li],
                   d_skip[li], ssm_norm_g[li], diff_lambda[li], diff_norm_g[li],
                   mlstm_gate_b[li], w_branch[li], w_out[li], ln_g[li], ln_b[li],
                   w_ple[li], ple_norm_g[li], w_ple_gate[li])
    return h
```

```python
import functools
import math

import jax
import jax.numpy as jnp
from jax import lax
from jax.experimental import pallas as pl
from jax.experimental.pallas import tpu as pltpu

F32 = jnp.float32
BF16 = jnp.bfloat16

V7X_VMEM_LIMIT_BYTES = 56 * 1024 * 1024
LANES = 128

D_MODEL = 4096
CHUNK = 64
SSM_WIDTH = 2048
SSM_HEAD_DIM = 64
SSM_HEADS = 32
SSM_GROUPS = 8
SSM_STATE = 128
SSM_XBC = SSM_WIDTH + 2 * SSM_GROUPS * SSM_STATE
GROUP_WIDTH = SSM_WIDTH // SSM_GROUPS
HEADS_PER_GROUP = SSM_HEADS // SSM_GROUPS
DIFF_WIDTH = 1024
DIFF_HEAD_DIM = 64
DIFF_HEADS = 8
MLSTM_WIDTH = 1024
MLSTM_HEAD_DIM = 128
MLSTM_HEADS = 8
N_BRANCH = 3
PLE_DIM = 256
DEPTH = 2
DEEPNORM_ALPHA = (2.0 * DEPTH) ** 0.25

OFF_XBC = 0
OFF_ZA = OFF_XBC + SSM_XBC
OFF_DT = OFF_ZA + SSM_WIDTH
OFF_QB = OFF_DT + SSM_HEADS
OFF_ZB = OFF_QB + 3 * DIFF_WIDTH
OFF_QC = OFF_ZB + DIFF_WIDTH
OFF_IC = OFF_QC + 5 * MLSTM_WIDTH
OFF_FC = OFF_IC + MLSTM_HEADS
OFF_GATES = OFF_FC + MLSTM_HEADS

LANE_DT = 0
LANE_I = SSM_HEADS
LANE_F = SSM_HEADS + MLSTM_HEADS

SCAN_CHUNK = 256
ATT_BLOCK = 256
NORM_ROWS = 64
NEG_BIG = -1e30


def _cparams(sem):
    return pltpu.CompilerParams(dimension_semantics=sem, vmem_limit_bytes=V7X_VMEM_LIMIT_BYTES)


def _silu(x):
    return x * jax.nn.sigmoid(x)


def _softplus(x):
    return jnp.maximum(x, 0.0) + jnp.log1p(jnp.exp(-jnp.abs(x)))


def _dot(a, b):
    return jnp.dot(a, b, preferred_element_type=F32)


def _dot_nt(a, b):
    return lax.dot_general(a, b, (((1,), (1,)), ((), ())), preferred_element_type=F32)


def _dot_tn(a, b):
    return lax.dot_general(a, b, (((0,), (0,)), ((), ())), preferred_element_type=F32)


def _cumsum_rows(x, tril_bf16):
    hi = x.astype(BF16)
    r1 = x - hi.astype(F32)
    mid = r1.astype(BF16)
    lo = (r1 - mid.astype(F32)).astype(BF16)
    return _dot(tril_bf16, hi) + _dot(tril_bf16, mid) + _dot(tril_bf16, lo)


def _mm_kernel(x_ref, w_ref, o_ref):
    o_ref[...] = _dot(x_ref[...], w_ref[...]).astype(o_ref.dtype)


def _matmul(x, w, out_dtype, tm, tn):
    m, k = x.shape
    n = w.shape[1]
    return pl.pallas_call(
        _mm_kernel,
        grid=(m // tm, n // tn),
        in_specs=[pl.BlockSpec((tm, k), lambda i, j: (i, 0)),
                  pl.BlockSpec((k, tn), lambda i, j: (0, j))],
        out_specs=pl.BlockSpec((tm, tn), lambda i, j: (i, j)),
        out_shape=jax.ShapeDtypeStruct((m, n), out_dtype),
        compiler_params=_cparams(("parallel", "arbitrary")),
        name="proj_matmul",
    )(x, w)


def _ssd_kernel(xbc_ref, z_ref, ps_ref, cw_ref, cb_ref, dtb_ref, alog_ref, dskip_ref, ng_ref,
                o_ref, xpad, state, ybuf):
    L = SCAN_CHUNK
    c = pl.program_id(1)

    @pl.when(c == 0)
    def _():
        xpad[0:8, :] = jnp.zeros((8, SSM_XBC), F32)
        state[...] = jnp.zeros(state.shape, F32)

    @pl.when(c > 0)
    def _():
        xpad[0:8, :] = xpad[L:L + 8, :]

    xpad[8:8 + L, :] = xbc_ref[0]

    def conv_silu(c0, width):
        acc = cb_ref[:, c0:c0 + width] + cw_ref[3:4, c0:c0 + width] * xpad[8:8 + L, c0:c0 + width]
        acc = acc + cw_ref[2:3, c0:c0 + width] * xpad[7:7 + L, c0:c0 + width]
        acc = acc + cw_ref[1:2, c0:c0 + width] * xpad[6:6 + L, c0:c0 + width]
        acc = acc + cw_ref[0:1, c0:c0 + width] * xpad[5:5 + L, c0:c0 + width]
        return _silu(acc)

    rows = lax.broadcasted_iota(jnp.int32, (L, L), 0)
    cols = lax.broadcasted_iota(jnp.int32, (L, L), 1)
    causal = rows >= cols
    tril = jnp.where(causal, 1.0, 0.0).astype(BF16)

    dt = _softplus(ps_ref[0] + dtb_ref[...])
    la = dt * (-jnp.exp(alog_ref[...]))
    lc = _cumsum_rows(la, tril)
    lc_t = lc.T

    head_of_lane = lax.broadcasted_iota(jnp.int32, (L, GROUP_WIDTH), 1) // SSM_HEAD_DIM

    def expand(vals, hol):
        out = vals[HEADS_PER_GROUP - 1]
        for r in range(HEADS_PER_GROUP - 2, -1, -1):
            out = jnp.where(hol == r, vals[r], out)
        return out

    ssq = jnp.zeros((L, 1), F32)
    for g in range(SSM_GROUPS):
        h0 = g * HEADS_PER_GROUP
        xg = conv_silu(g * GROUP_WIDTH, GROUP_WIDTH)
        bg = conv_silu(SSM_WIDTH + g * SSM_STATE, SSM_STATE).astype(BF16)
        cg = conv_silu(SSM_WIDTH + SSM_GROUPS * SSM_STATE + g * SSM_STATE, SSM_STATE).astype(BF16)

        dt4 = expand([dt[:, h0 + r:h0 + r + 1] for r in range(HEADS_PER_GROUP)], head_of_lane)
        lc4 = expand([lc[:, h0 + r:h0 + r + 1] for r in range(HEADS_PER_GROUP)], head_of_lane)
        lc_end4 = expand([lc[L - 1:L, h0 + r:h0 + r + 1] for r in range(HEADS_PER_GROUP)],
                         head_of_lane[0:1, :])
        xdt = xg * dt4

        cb = _dot_nt(cg, bg)
        y = jnp.zeros((L, GROUP_WIDTH), F32)
        for r in range(HEADS_PER_GROUP):
            h = h0 + r
            seg = lc[:, h:h + 1] - lc_t[h:h + 1, :]
            decay = jnp.where(causal, jnp.exp(seg), 0.0)
            m_r = (cb * decay).astype(BF16)
            x_r = jnp.where(head_of_lane == r, xdt, 0.0).astype(BF16)
            y = y + _dot(m_r, x_r)

        prev = state[g]
        y = y + _dot(cg, prev.astype(BF16)) * jnp.exp(lc4)
        xs = (xdt * jnp.exp(lc_end4 - lc4)).astype(BF16)
        state[g] = prev * jnp.exp(lc_end4) + _dot_tn(bg, xs)

        y = y + dskip_ref[:, g * GROUP_WIDTH:(g + 1) * GROUP_WIDTH] * xg
        yz = y * _silu(z_ref[0, :, g * GROUP_WIDTH:(g + 1) * GROUP_WIDTH])
        ybuf[:, g * GROUP_WIDTH:(g + 1) * GROUP_WIDTH] = yz
        ssq = ssq + jnp.sum(yz * yz, axis=-1, keepdims=True)

    rs = lax.rsqrt(ssq * (1.0 / SSM_WIDTH) + 1e-5)
    o_ref[0] = (ybuf[...] * rs * ng_ref[...]).astype(o_ref.dtype)


def _ssd_call(pa, ps, conv_w, conv_b, dtb, alog, dskip, norm_g):
    bsz, s, _ = pa.shape
    L = SCAN_CHUNK
    full = lambda b, c: (0, 0)
    return pl.pallas_call(
        _ssd_kernel,
        grid=(bsz, s // L),
        in_specs=[pl.BlockSpec((1, L, SSM_XBC), lambda b, c: (b, c, 0)),
                  pl.BlockSpec((1, L, SSM_WIDTH), lambda b, c: (b, c, SSM_XBC // SSM_WIDTH)),
                  pl.BlockSpec((1, L, LANES), lambda b, c: (b, c, 0)),
                  pl.BlockSpec(conv_w.shape, full),
                  pl.BlockSpec(conv_b.shape, full),
                  pl.BlockSpec(dtb.shape, full),
                  pl.BlockSpec(alog.shape, full),
                  pl.BlockSpec(dskip.shape, full),
                  pl.BlockSpec(norm_g.shape, full)],
        out_specs=pl.BlockSpec((1, L, SSM_WIDTH), lambda b, c: (b, c, 0)),
        out_shape=jax.ShapeDtypeStruct((bsz, s, SSM_WIDTH), BF16),
        scratch_shapes=[pltpu.VMEM((L + 8, SSM_XBC), F32),
                        pltpu.VMEM((SSM_GROUPS, SSM_STATE, GROUP_WIDTH), F32),
                        pltpu.VMEM((L, SSM_WIDTH), F32)],
        compiler_params=_cparams(("parallel", "arbitrary")),
        name="ssd_scan",
    )(pa, pa, ps, conv_w, conv_b, dtb, alog, dskip, norm_g)


def _attn_kernel(lam_init, q_ref, k_ref, v_ref, z_ref, slope_ref, dl_ref, ng_ref, o_ref):
    T = ATT_BLOCK
    qi = pl.program_id(2)
    slope = slope_ref[0][:, 0:1]

    lane = lax.broadcasted_iota(jnp.int32, (T, 2 * DIFF_HEAD_DIM), 1)
    qs = q_ref[0].astype(F32) * (DIFF_HEAD_DIM ** -0.5)
    q_lo = jnp.where(lane < DIFF_HEAD_DIM, qs, 0.0).astype(BF16)
    q_hi = jnp.where(lane >= DIFF_HEAD_DIM, qs, 0.0).astype(BF16)

    ri = lax.broadcasted_iota(jnp.int32, (T, T), 0)
    ci = lax.broadcasted_iota(jnp.int32, (T, T), 1)
    dij = (ri - ci).astype(F32)
    bias_off = -slope * dij
    visible = (ci // CHUNK) <= (ri // CHUNK)
    bias_diag = jnp.where(visible, -slope * jnp.abs(dij), NEG_BIG)

    def update(carry, s, vb):
        m, l, acc = carry
        m_new = jnp.maximum(m, jnp.max(s, axis=-1, keepdims=True))
        alpha = jnp.exp(m - m_new)
        p = jnp.exp(s - m_new)
        l = alpha * l + jnp.sum(p, axis=-1, keepdims=True)
        acc = alpha * acc + _dot(p.astype(BF16), vb)
        return m_new, l, acc

    def block(kb, carry, bias):
        c1, c2 = carry
        start = pl.multiple_of(kb * T, T)
        kblk = k_ref[0, pl.ds(start, T), :]
        vblk = v_ref[0, pl.ds(start, T), :]
        c1 = update(c1, _dot_nt(q_lo, kblk) + bias, vblk)
        c2 = update(c2, _dot_nt(q_hi, kblk) + bias, vblk)
        return c1, c2

    def off_diag(kb, carry):
        delta = ((qi - kb) * T).astype(F32)
        return block(kb, carry, bias_off - slope * delta)

    init1 = (jnp.full((T, 1), NEG_BIG, F32), jnp.zeros((T, 1), F32), jnp.zeros((T, 2 * DIFF_HEAD_DIM), F32))
    carry = lax.fori_loop(0, qi, off_diag, (init1, init1))
    (m1, l1, a1), (m2, l2, a2) = block(qi, carry, bias_diag)

    dl = dl_ref[...]
    lam = (jnp.exp(jnp.sum(dl[0:1] * dl[1:2], axis=-1, keepdims=True))
           - jnp.exp(jnp.sum(dl[2:3] * dl[3:4], axis=-1, keepdims=True)) + lam_init)
    o = a1 / l1 - lam * (a2 / l2)
    y = o * lax.rsqrt(jnp.mean(o * o, axis=-1, keepdims=True) + 1e-5) * ng_ref[...]
    y = y * (1.0 - lam_init)
    o_ref[0] = (y * _silu(z_ref[0])).astype(o_ref.dtype)


def _attn_call(qkv, pz, slopes, dl, norm_g, lam_init):
    bsz, s, _ = qkv.shape
    T = ATT_BLOCK
    hw = 2 * DIFF_HEAD_DIM
    return pl.pallas_call(
        functools.partial(_attn_kernel, lam_init),
        grid=(bsz, DIFF_HEADS, s // T),
        in_specs=[pl.BlockSpec((1, T, hw), lambda b, h, i: (b, i, h)),
                  pl.BlockSpec((1, s, hw), lambda b, h, i: (b, 0, DIFF_HEADS + h)),
                  pl.BlockSpec((1, s, hw), lambda b, h, i: (b, 0, 2 * DIFF_HEADS + h)),
                  pl.BlockSpec((1, T, hw), lambda b, h, i: (b, i, h)),
                  pl.BlockSpec((1, 1, LANES), lambda b, h, i: (h, 0, 0)),
                  pl.BlockSpec(dl.shape, lambda b, h, i: (0, 0)),
                  pl.BlockSpec(norm_g.shape, lambda b, h, i: (0, 0))],
        out_specs=pl.BlockSpec((1, T, hw), lambda b, h, i: (b, i, h)),
        out_shape=jax.ShapeDtypeStruct((bsz, s, DIFF_WIDTH), BF16),
        compiler_params=_cparams(("parallel", "parallel", "arbitrary")),
        name="diff_attention",
    )(qkv, qkv, qkv, pz, slopes, dl, norm_g)


def _mlstm_kernel(q_ref, k_ref, v_ref, o_ref_in, z_ref, ps_ref, gb_ref, out_ref, c_state, n_state, m_state):
    L = SCAN_CHUNK
    D = MLSTM_HEAD_DIM
    c = pl.program_id(1)

    @pl.when(c == 0)
    def _():
        c_state[...] = jnp.zeros(c_state.shape, F32)
        n_state[...] = jnp.zeros(n_state.shape, F32)
        m_state[...] = jnp.zeros(m_state.shape, F32)

    rows = lax.broadcasted_iota(jnp.int32, (L, L), 0)
    cols = lax.broadcasted_iota(jnp.int32, (L, L), 1)
    causal = rows >= cols
    tril = jnp.where(causal, 1.0, 0.0).astype(BF16)

    raw = ps_ref[0] + gb_ref[...]
    logf = -_softplus(-raw)
    bcum = _cumsum_rows(logf, tril)
    bcum_t = bcum.T
    raw_t = raw.T

    for h in range(MLSTM_HEADS):
        li, lf = LANE_I + h, LANE_F + h
        sl = slice(h * D, (h + 1) * D)
        q = q_ref[0, :, sl]
        ks = k_ref[0, :, sl] * (D ** -0.5)
        v = v_ref[0, :, sl]
        qb, kb = q.astype(BF16), ks.astype(BF16)

        b_col = bcum[:, lf:lf + 1]
        b_row = bcum_t[lf:lf + 1, :]
        ig_col = raw[:, li:li + 1]
        ig_row = raw_t[li:li + 1, :]
        btot = bcum[L - 1:L, lf:lf + 1]
        m_prev = m_state[h:h + 1, 0:1]
        n_prev = n_state[h:h + 1, :]
        c_prev = c_state[h]

        dmat = jnp.where(causal, b_col - b_row + ig_row, NEG_BIG)
        m_inter = b_col + m_prev
        m_row = jnp.maximum(m_inter, jnp.max(dmat, axis=-1, keepdims=True))
        wts = jnp.exp(dmat - m_row)
        sw = wts * _dot_nt(qb, kb)
        num = _dot(sw.astype(BF16), v.astype(BF16))
        den = jnp.sum(sw, axis=-1, keepdims=True)
        a_inter = jnp.exp(m_inter - m_row)
        num = num + _dot(qb, c_prev.astype(BF16)) * a_inter
        den = den + jnp.sum(q * n_prev, axis=-1, keepdims=True) * a_inter
        hout = num / jnp.maximum(jnp.abs(den), jnp.exp(-m_row))
        gate = jax.nn.sigmoid(o_ref_in[0, :, sl]) * _silu(z_ref[0, :, sl])
        out_ref[0, :, sl] = (gate * hout).astype(out_ref.dtype)

        w_end = btot - b_col + ig_col
        m_loc = jnp.max(w_end, axis=0, keepdims=True)
        m_new = jnp.maximum(btot + m_prev, m_loc)
        a_old = jnp.exp(btot + m_prev - m_new)
        e_end = jnp.exp(w_end - m_new)
        c_state[h] = a_old * c_prev + _dot_tn(kb, (v * e_end).astype(BF16))
        n_state[h:h + 1, :] = a_old * n_prev + jnp.sum(ks * e_end, axis=0, keepdims=True)
        m_state[h:h + 1, :] = jnp.broadcast_to(m_new, (1, LANES))


def _mlstm_call(pc, ps, gate_bias):
    bsz, s, _ = pc.shape
    L = SCAN_CHUNK
    W = MLSTM_WIDTH
    return pl.pallas_call(
        _mlstm_kernel,
        grid=(bsz, s // L),
        in_specs=[pl.BlockSpec((1, L, W), lambda b, c: (b, c, 0)),
                  pl.BlockSpec((1, L, W), lambda b, c: (b, c, 1)),
                  pl.BlockSpec((1, L, W), lambda b, c: (b, c, 2)),
                  pl.BlockSpec((1, L, W), lambda b, c: (b, c, 3)),
                  pl.BlockSpec((1, L, W), lambda b, c: (b, c, 4)),
                  pl.BlockSpec((1, L, LANES), lambda b, c: (b, c, 0)),
                  pl.BlockSpec(gate_bias.shape, lambda b, c: (0, 0))],
        out_specs=pl.BlockSpec((1, L, W), lambda b, c: (b, c, 0)),
        out_shape=jax.ShapeDtypeStruct((bsz, s, W), BF16),
        scratch_shapes=[pltpu.VMEM((MLSTM_HEADS, MLSTM_HEAD_DIM, MLSTM_HEAD_DIM), F32),
                        pltpu.VMEM((MLSTM_HEADS, MLSTM_HEAD_DIM), F32),
                        pltpu.VMEM((MLSTM_HEADS, LANES), F32)],
        compiler_params=_cparams(("parallel", "arbitrary")),
        name="mlstm_scan",
    )(pc, pc, pc, pc, pc, ps, gate_bias)


def _merge_kernel(x_ref, oa_ref, ob_ref, oc_ref, wg_ref, wa_ref, wb_ref, wc_ref, o_ref):
    x = x_ref[...]
    acc = jax.nn.sigmoid(_dot(x, wg_ref[0])) * _dot(oa_ref[...], wa_ref[...])
    acc = acc + jax.nn.sigmoid(_dot(x, wg_ref[1])) * _dot(ob_ref[...], wb_ref[...])
    acc = acc + jax.nn.sigmoid(_dot(x, wg_ref[2])) * _dot(oc_ref[...], wc_ref[...])
    o_ref[...] = acc.astype(o_ref.dtype)


def _merge_call(xb, oa, ob, oc, wg, wa, wb, wc, tm, tn):
    m = xb.shape[0]
    row = lambda i, j: (i, 0)
    col = lambda i, j: (0, j)
    return pl.pallas_call(
        _merge_kernel,
        grid=(m // tm, D_MODEL // tn),
        in_specs=[pl.BlockSpec((tm, D_MODEL), row),
                  pl.BlockSpec((tm, SSM_WIDTH), row),
                  pl.BlockSpec((tm, DIFF_WIDTH), row),
                  pl.BlockSpec((tm, MLSTM_WIDTH), row),
                  pl.BlockSpec((N_BRANCH, D_MODEL, tn), lambda i, j: (0, 0, j)),
                  pl.BlockSpec((SSM_WIDTH, tn), col),
                  pl.BlockSpec((DIFF_WIDTH, tn), col),
                  pl.BlockSpec((MLSTM_WIDTH, tn), col)],
        out_specs=pl.BlockSpec((tm, tn), lambda i, j: (i, j)),
        out_shape=jax.ShapeDtypeStruct((m, D_MODEL), BF16),
        compiler_params=_cparams(("parallel", "arbitrary")),
        name="gated_merge",
    )(xb, oa, ob, oc, wg, wa, wb, wc)


def _final_kernel(nj, tn, mg_ref, w_ref, x_ref, p_ref, wple_ref, lng_ref, lnb_ref, pg_ref,
                  o_ref, ob_ref, h_scr, hb_scr, rs_scr):
    j = pl.program_id(1)
    tm = h_scr.shape[0]

    @pl.when(j < nj)
    def _():
        off = pl.multiple_of(j * tn, tn)
        h_scr[:, pl.ds(off, tn)] = DEEPNORM_ALPHA * x_ref[...] + _dot(mg_ref[...], w_ref[0])

    @pl.when(j == nj)
    def _():
        def norm_rows(r, carry):
            rows = pl.ds(pl.multiple_of(r * NORM_ROWS, NORM_ROWS), NORM_ROWS)
            pre = h_scr[rows, :]
            mu = jnp.mean(pre, axis=-1, keepdims=True)
            cen = pre - mu
            var = jnp.mean(cen * cen, axis=-1, keepdims=True)
            h = cen * lax.rsqrt(var + 1e-5) * lng_ref[...] + lnb_ref[...]
            h_scr[rows, :] = h
            hb_scr[rows, :] = h.astype(BF16)
            e = _dot(p_ref[rows, :], wple_ref[...])
            rs = lax.rsqrt(jnp.mean(e * e, axis=-1, keepdims=True) + 1e-5)
            rs_scr[rows, :] = jnp.broadcast_to(rs, (NORM_ROWS, LANES))
            return carry

        lax.fori_loop(0, tm // NORM_ROWS, norm_rows, 0)

    @pl.when(j >= nj)
    def _():
        off = pl.multiple_of((j - nj) * tn, tn)
        gate = jax.nn.sigmoid(_dot(hb_scr[...], w_ref[0]))
        e = _dot(p_ref[...], wple_ref[:, pl.ds(off, tn)]) * rs_scr[:, 0:1] * pg_ref[:, pl.ds(off, tn)]
        out = h_scr[:, pl.ds(off, tn)] + gate * e
        o_ref[...] = out
        ob_ref[...] = out.astype(BF16)


def _final_call(merged, w2, x, pb, w_ple, ln_g, ln_b, ple_g, tm, tn):
    m = merged.shape[0]
    nj = D_MODEL // tn
    full = lambda i, j: (0, 0)
    out_idx = lambda i, j: (i, jnp.maximum(j - nj, 0))
    return pl.pallas_call(
        functools.partial(_final_kernel, nj, tn),
        grid=(m // tm, 2 * nj),
        in_specs=[pl.BlockSpec((tm, D_MODEL), lambda i, j: (i, 0)),
                  pl.BlockSpec((1, D_MODEL, tn), lambda i, j: (j // nj, 0, j % nj)),
                  pl.BlockSpec((tm, tn), lambda i, j: (i, jnp.minimum(j, nj - 1))),
                  pl.BlockSpec((tm, PLE_DIM), lambda i, j: (i, 0)),
                  pl.BlockSpec(w_ple.shape, full),
                  pl.BlockSpec(ln_g.shape, full),
                  pl.BlockSpec(ln_b.shape, full),
                  pl.BlockSpec(ple_g.shape, full)],
        out_specs=[pl.BlockSpec((tm, tn), out_idx), pl.BlockSpec((tm, tn), out_idx)],
        out_shape=[jax.ShapeDtypeStruct((m, D_MODEL), F32), jax.ShapeDtypeStruct((m, D_MODEL), BF16)],
        scratch_shapes=[pltpu.VMEM((tm, D_MODEL), F32),
                        pltpu.VMEM((tm, D_MODEL), BF16),
                        pltpu.VMEM((tm, LANES), F32)],
        compiler_params=_cparams(("parallel", "arbitrary")),
        name="outproj_deepnorm_ple",
    )(merged, w2, x, pb, w_ple, ln_g, ln_b, ple_g)


def _row_tile(m, pref):
    return pref if m % pref == 0 else m


def _layer(x, xb, p_i, li, w_in, conv_w, conv_b, dt_bias, a_log, d_skip, ssm_norm_g,
           diff_lambda, diff_norm_g, mlstm_gate_b, w_branch, w_out, ln_g, ln_b,
           w_ple, ple_norm_g, w_ple_gate):
    bsz, s, _ = x.shape
    t = bsz * s
    x2 = x.reshape(t, D_MODEL)
    xb2 = xb.reshape(t, D_MODEL)

    w_a = w_in[:, OFF_XBC:OFF_DT].astype(BF16)
    w_qkv = w_in[:, OFF_QB:OFF_ZB].astype(BF16)
    w_zb = w_in[:, OFF_ZB:OFF_QC].astype(BF16)
    w_c = w_in[:, OFF_QC:OFF_IC].astype(BF16)
    w_small = jnp.concatenate(
        [w_in[:, OFF_DT:OFF_QB], w_in[:, OFF_IC:OFF_GATES],
         jnp.zeros((D_MODEL, LANES - SSM_HEADS - 2 * MLSTM_HEADS), w_in.dtype)], axis=1).astype(BF16)
    w_gates = jnp.transpose(w_in[:, OFF_GATES:].reshape(D_MODEL, N_BRANCH, D_MODEL), (1, 0, 2)).astype(BF16)
    w_br = w_branch.astype(BF16)
    w2 = jnp.stack([w_out, w_ple_gate]).astype(BF16)

    tm = _row_tile(t, 1024)
    pa = _matmul(xb2, w_a, F32, tm, 1024).reshape(bsz, s, -1)
    qkv = _matmul(xb2, w_qkv, BF16, tm, 1024).reshape(bsz, s, -1)
    pz = _matmul(xb2, w_zb, F32, tm, 1024).reshape(bsz, s, -1)
    pc = _matmul(xb2, w_c, F32, tm, 1024).reshape(bsz, s, -1)
    ps = _matmul(xb2, w_small, F32, tm, LANES).reshape(bsz, s, -1)

    pad = lambda v, lane0: jnp.zeros((1, LANES), F32).at[0, lane0:lane0 + v.shape[0]].set(v.astype(F32))
    out_a = _ssd_call(pa, ps, conv_w, conv_b.reshape(1, -1), pad(dt_bias, LANE_DT), pad(a_log, LANE_DT),
                      jnp.repeat(d_skip.astype(F32), SSM_HEAD_DIM).reshape(1, -1), ssm_norm_g.reshape(1, -1))

    lam_init = 0.8 - 0.6 * math.exp(-0.3 * li)
    slopes = 2.0 ** (-8.0 * jnp.arange(1, DIFF_HEADS + 1, dtype=F32) / DIFF_HEADS)
    slopes = jnp.broadcast_to(slopes[:, None, None], (DIFF_HEADS, 1, LANES))
    out_b = _attn_call(qkv, pz, slopes, diff_lambda, diff_norm_g.reshape(1, -1), lam_init)

    gate_bias = pad(mlstm_gate_b[0], LANE_I) + pad(mlstm_gate_b[1], LANE_F)
    out_c = _mlstm_call(pc, ps, gate_bias)

    tm2 = _row_tile(t, 512)
    merged = _merge_call(xb2, out_a.reshape(t, -1), out_b.reshape(t, -1), out_c.reshape(t, -1), w_gates,
                         w_br[:SSM_WIDTH], w_br[SSM_WIDTH:SSM_WIDTH + DIFF_WIDTH], w_br[SSM_WIDTH + DIFF_WIDTH:],
                         tm2, 256)
    out, out_b16 = _final_call(merged, w2, x2, p_i.reshape(t, PLE_DIM).astype(BF16), w_ple.astype(BF16),
                               ln_g.reshape(1, -1), ln_b.reshape(1, -1), ple_norm_g.reshape(1, -1), tm2, 512)
    return out.reshape(bsz, s, D_MODEL), out_b16.reshape(bsz, s, D_MODEL)


def kernel(x, p, w_in, conv_w, conv_b, dt_bias, a_log, d_skip, ssm_norm_g, diff_lambda, diff_norm_g,
           mlstm_gate_b, w_branch, w_out, ln_g, ln_b, w_ple, ple_norm_g, w_ple_gate):
    h = x
    hb = x.astype(BF16)
    for li in range(w_in.shape[0]):
        h, hb = _layer(h, hb, p[li], li, w_in[li], conv_w[li], conv_b[li], dt_bias[li], a_log[li],
                       d_skip[li], ssm_norm_g[li], diff_lambda[li], diff_norm_g[li],
                       mlstm_gate_b[li], w_branch[li], w_out[li], ln_g[li], ln_b[li],
                       w_ple[li], ple_norm_g[li], w_ple_gate[li])
    return h
```

```python
import functools
import math

import jax
import jax.numpy as jnp
from jax import lax
from jax.experimental import pallas as pl
from jax.experimental.pallas import tpu as pltpu

F32 = jnp.float32
BF16 = jnp.bfloat16

V7X_VMEM_LIMIT_BYTES = 56 * 1024 * 1024
LANES = 128

D_MODEL = 4096
CHUNK = 64
SSM_WIDTH = 2048
SSM_HEAD_DIM = 64
SSM_HEADS = 32
SSM_GROUPS = 8
SSM_STATE = 128
SSM_XBC = SSM_WIDTH + 2 * SSM_GROUPS * SSM_STATE
GROUP_WIDTH = SSM_WIDTH // SSM_GROUPS
HEADS_PER_GROUP = SSM_HEADS // SSM_GROUPS
DIFF_WIDTH = 1024
DIFF_HEAD_DIM = 64
DIFF_HEADS = 8
MLSTM_WIDTH = 1024
MLSTM_HEAD_DIM = 128
MLSTM_HEADS = 8
N_BRANCH = 3
PLE_DIM = 256
DEPTH = 2
DEEPNORM_ALPHA = (2.0 * DEPTH) ** 0.25

OFF_XBC = 0
OFF_ZA = OFF_XBC + SSM_XBC
OFF_DT = OFF_ZA + SSM_WIDTH
OFF_QB = OFF_DT + SSM_HEADS
OFF_ZB = OFF_QB + 3 * DIFF_WIDTH
OFF_QC = OFF_ZB + DIFF_WIDTH
OFF_IC = OFF_QC + 5 * MLSTM_WIDTH
OFF_FC = OFF_IC + MLSTM_HEADS
OFF_GATES = OFF_FC + MLSTM_HEADS

P_OFF_XBC = 0
P_OFF_ZA = P_OFF_XBC + SSM_XBC
P_OFF_QB = P_OFF_ZA + SSM_WIDTH
P_OFF_KB = P_OFF_QB + DIFF_WIDTH
P_OFF_VB = P_OFF_KB + DIFF_WIDTH
P_OFF_ZB = P_OFF_VB + DIFF_WIDTH
P_OFF_C = P_OFF_ZB + DIFF_WIDTH
PROJ_WIDTH = P_OFF_C + 5 * MLSTM_WIDTH
W_OFF_GATES = PROJ_WIDTH
W_OFF_SMALL = W_OFF_GATES + N_BRANCH * D_MODEL

LANE_DT = 0
LANE_I = SSM_HEADS
LANE_F = SSM_HEADS + MLSTM_HEADS

SCAN_CHUNK = 256
ATT_BLOCK = 512
ATT_AUG = 2 * LANES
ATT_ACC_ROWS = LANES + 16
POS_SPLIT = 256
NORM_ROWS = 64
NEG_BIG = -1e30
LOG2E = 1.4426950408889634


def _cparams(sem):
    return pltpu.CompilerParams(dimension_semantics=sem, vmem_limit_bytes=V7X_VMEM_LIMIT_BYTES)


def _silu(x):
    return x * jax.nn.sigmoid(x)


def _softplus(x):
    return jnp.maximum(x, 0.0) + jnp.log1p(jnp.exp(-jnp.abs(x)))


def _dot(a, b):
    return jnp.dot(a, b, preferred_element_type=F32)


def _dot_nt(a, b):
    return lax.dot_general(a, b, (((1,), (1,)), ((), ())), preferred_element_type=F32)


def _dot_tn(a, b):
    return lax.dot_general(a, b, (((0,), (0,)), ((), ())), preferred_element_type=F32)


def _cumsum_rows(x, tril_bf16):
    hi = x.astype(BF16)
    r1 = x - hi.astype(F32)
    mid = r1.astype(BF16)
    lo = (r1 - mid.astype(F32)).astype(BF16)
    return _dot(tril_bf16, hi) + _dot(tril_bf16, mid) + _dot(tril_bf16, lo)


def _proj_kernel(x_ref, w_ref, ws_ref, o_ref, os_ref):
    o_ref[...] = _dot(x_ref[...], w_ref[...])

    @pl.when(pl.program_id(1) == 0)
    def _():
        os_ref[...] = _dot(x_ref[...], ws_ref[...])


def _proj_call(xb, w_all, li, tm, tn):
    m, k = xb.shape
    return pl.pallas_call(
        _proj_kernel,
        grid=(m // tm, PROJ_WIDTH // tn),
        in_specs=[pl.BlockSpec((tm, k), lambda i, j: (i, 0)),
                  pl.BlockSpec((None, k, tn), lambda i, j: (li, 0, j)),
                  pl.BlockSpec((None, k, LANES), lambda i, j: (li, 0, W_OFF_SMALL // LANES))],
        out_specs=[pl.BlockSpec((tm, tn), lambda i, j: (i, j)),
                   pl.BlockSpec((tm, LANES), lambda i, j: (i, 0))],
        out_shape=[jax.ShapeDtypeStruct((m, PROJ_WIDTH), F32), jax.ShapeDtypeStruct((m, LANES), F32)],
        compiler_params=_cparams(("parallel", "arbitrary")),
        name="in_proj",
    )(xb, w_all, w_all)


def _ssd_kernel(xbc_ref, z_ref, ps_ref, cw_ref, cb_ref, dtb_ref, alog_ref, dskip_ref, ng_ref,
                o_ref, xpad, state, ybuf):
    L = SCAN_CHUNK
    c = pl.program_id(1)

    @pl.when(c == 0)
    def _():
        xpad[0:8, :] = jnp.zeros((8, SSM_XBC), F32)
        state[...] = jnp.zeros(state.shape, F32)

    @pl.when(c > 0)
    def _():
        xpad[0:8, :] = xpad[L:L + 8, :]

    xpad[8:8 + L, :] = xbc_ref[0]

    def conv_silu(c0, width):
        acc = cb_ref[:, c0:c0 + width] + cw_ref[3:4, c0:c0 + width] * xpad[8:8 + L, c0:c0 + width]
        acc = acc + cw_ref[2:3, c0:c0 + width] * xpad[7:7 + L, c0:c0 + width]
        acc = acc + cw_ref[1:2, c0:c0 + width] * xpad[6:6 + L, c0:c0 + width]
        acc = acc + cw_ref[0:1, c0:c0 + width] * xpad[5:5 + L, c0:c0 + width]
        return _silu(acc)

    rows = lax.broadcasted_iota(jnp.int32, (L, L), 0)
    cols = lax.broadcasted_iota(jnp.int32, (L, L), 1)
    causal = rows >= cols
    tril = jnp.where(causal, 1.0, 0.0).astype(BF16)

    dt = _softplus(ps_ref[0] + dtb_ref[...])
    la = dt * (-jnp.exp(alog_ref[...]))
    lc = _cumsum_rows(la, tril)
    lc_t = lc.T

    head_of_lane = lax.broadcasted_iota(jnp.int32, (L, GROUP_WIDTH), 1) // SSM_HEAD_DIM

    def expand(vals, hol):
        out = vals[HEADS_PER_GROUP - 1]
        for r in range(HEADS_PER_GROUP - 2, -1, -1):
            out = jnp.where(hol == r, vals[r], out)
        return out

    ssq = jnp.zeros((L, 1), F32)
    for g in range(SSM_GROUPS):
        h0 = g * HEADS_PER_GROUP
        xg = conv_silu(g * GROUP_WIDTH, GROUP_WIDTH)
        bg = conv_silu(SSM_WIDTH + g * SSM_STATE, SSM_STATE).astype(BF16)
        cg = conv_silu(SSM_WIDTH + SSM_GROUPS * SSM_STATE + g * SSM_STATE, SSM_STATE).astype(BF16)

        dt4 = expand([dt[:, h0 + r:h0 + r + 1] for r in range(HEADS_PER_GROUP)], head_of_lane)
        lc4 = expand([lc[:, h0 + r:h0 + r + 1] for r in range(HEADS_PER_GROUP)], head_of_lane)
        lc_end4 = expand([lc[L - 1:L, h0 + r:h0 + r + 1] for r in range(HEADS_PER_GROUP)],
                         head_of_lane[0:1, :])
        xdt = xg * dt4

        cb = _dot_nt(cg, bg)
        y = jnp.zeros((L, GROUP_WIDTH), F32)
        for r in range(HEADS_PER_GROUP):
            h = h0 + r
            seg = lc[:, h:h + 1] - lc_t[h:h + 1, :]
            decay = jnp.where(causal, jnp.exp(seg), 0.0)
            m_r = (cb * decay).astype(BF16)
            x_r = jnp.where(head_of_lane == r, xdt, 0.0).astype(BF16)
            y = y + _dot(m_r, x_r)

        prev = state[g]
        y = y + _dot(cg, prev.astype(BF16)) * jnp.exp(lc4)
        xs = (xdt * jnp.exp(lc_end4 - lc4)).astype(BF16)
        state[g] = prev * jnp.exp(lc_end4) + _dot_tn(bg, xs)

        y = y + dskip_ref[:, g * GROUP_WIDTH:(g + 1) * GROUP_WIDTH] * xg
        yz = y * _silu(z_ref[0, :, g * GROUP_WIDTH:(g + 1) * GROUP_WIDTH])
        ybuf[:, g * GROUP_WIDTH:(g + 1) * GROUP_WIDTH] = yz
        ssq = ssq + jnp.sum(yz * yz, axis=-1, keepdims=True)

    rs = lax.rsqrt(ssq * (1.0 / SSM_WIDTH) + 1e-5)
    o_ref[0] = (ybuf[...] * rs * ng_ref[...]).astype(o_ref.dtype)


def _ssd_call(pa, ps, conv_w, conv_b, dtb, alog, dskip, norm_g):
    bsz, s, _ = pa.shape
    L = SCAN_CHUNK
    full = lambda b, c: (0, 0)
    return pl.pallas_call(
        _ssd_kernel,
        grid=(bsz, s // L),
        in_specs=[pl.BlockSpec((1, L, SSM_XBC), lambda b, c: (b, c, 0)),
                  pl.BlockSpec((1, L, SSM_WIDTH), lambda b, c: (b, c, SSM_XBC // SSM_WIDTH)),
                  pl.BlockSpec((1, L, LANES), lambda b, c: (b, c, 0)),
                  pl.BlockSpec(conv_w.shape, full),
                  pl.BlockSpec(conv_b.shape, full),
                  pl.BlockSpec(dtb.shape, full),
                  pl.BlockSpec(alog.shape, full),
                  pl.BlockSpec(dskip.shape, full),
                  pl.BlockSpec(norm_g.shape, full)],
        out_specs=pl.BlockSpec((1, L, SSM_WIDTH), lambda b, c: (b, c, 0)),
        out_shape=jax.ShapeDtypeStruct((bsz, s, SSM_WIDTH), BF16),
        scratch_shapes=[pltpu.VMEM((L + 8, SSM_XBC), F32),
                        pltpu.VMEM((SSM_GROUPS, SSM_STATE, GROUP_WIDTH), F32),
                        pltpu.VMEM((L, SSM_WIDTH), F32)],
        compiler_params=_cparams(("parallel", "arbitrary")),
        name="ssd_scan",
    )(pa, pa, ps, conv_w, conv_b, dtb, alog, dskip, norm_g)


def _attn_kernel(lam_init, q_ref, k_ref, v_ref, z_ref, slope_ref, corr_ref, dl_ref, ng_ref, o_ref,
                 k_aug, vt_aug, acc1, acc2):
    T = ATT_BLOCK
    D2 = 2 * DIFF_HEAD_DIM
    qi = pl.program_id(2)
    n_kv = k_ref.shape[1] // T

    @pl.when(qi == 0)
    def _():
        lane = lax.broadcasted_iota(jnp.int32, (T, LANES), 1)
        row = lax.broadcasted_iota(jnp.int32, (T, LANES), 0)
        for c in range(n_kv):
            rows = slice(c * T, (c + 1) * T)
            pos = row + c * T
            pos_lo = jnp.bitwise_and(pos, POS_SPLIT - 1)
            pos_hi = pos - pos_lo
            pos_cols = jnp.where(lane < 3, pos_lo, jnp.where(lane < 6, pos_hi, 0))
            k_aug[rows, 0:D2] = k_ref[0, rows, :].astype(BF16)
            k_aug[rows, D2:ATT_AUG] = pos_cols.astype(F32).astype(BF16)
            vt_aug[0:D2, rows] = v_ref[0, rows, :].T.astype(BF16)
        ones_row = lax.broadcasted_iota(jnp.int32, (ATT_ACC_ROWS - D2, k_ref.shape[1]), 0) == 0
        vt_aug[D2:ATT_ACC_ROWS, :] = jnp.where(ones_row, 1.0, 0.0).astype(BF16)

    lane = lax.broadcasted_iota(jnp.int32, (T, D2), 1)
    qs = q_ref[0] * (DIFF_HEAD_DIM ** -0.5 * LOG2E)
    sl2 = slope_ref[0][:, 0:1] * LOG2E
    sl_hi = sl2.astype(BF16).astype(F32)
    sl_mid = (sl2 - sl_hi).astype(BF16).astype(F32)
    sl_lo = sl2 - sl_hi - sl_mid
    lane1 = lax.broadcasted_iota(jnp.int32, (1, LANES), 1)
    ext = jnp.where((lane1 == 0) | (lane1 == 3), sl_hi,
                    jnp.where((lane1 == 1) | (lane1 == 4), sl_mid,
                              jnp.where((lane1 == 2) | (lane1 == 5), sl_lo, 0.0)))
    ext = jnp.broadcast_to(ext, (T, LANES)).astype(BF16)
    qa1 = jnp.concatenate([jnp.where(lane < DIFF_HEAD_DIM, qs, 0.0).astype(BF16), ext], axis=1)
    qa2 = jnp.concatenate([jnp.where(lane >= DIFF_HEAD_DIM, qs, 0.0).astype(BF16), ext], axis=1)

    acc1[...] = jnp.zeros(acc1.shape, F32)
    acc2[...] = jnp.zeros(acc2.shape, F32)

    def block(start, ms, corr):
        kb = k_aug[pl.ds(start, T), :]
        vb = vt_aug[:, pl.ds(start, T)]

        def one(qa, m, acc):
            s = _dot_nt(kb, qa)
            if corr is not None:
                s = s + corr
            m_new = jnp.maximum(m, jnp.max(s, axis=0, keepdims=True))
            p = jnp.exp2(s - m_new).astype(BF16)
            acc[...] = jnp.exp2(m - m_new) * acc[...] + _dot(vb, p)
            return m_new

        return one(qa1, ms[0], acc1), one(qa2, ms[1], acc2)

    m0 = jnp.full((1, T), NEG_BIG, F32)
    ms = lax.fori_loop(0, qi, lambda kb, ms: block(pl.multiple_of(kb * T, T), ms, None), (m0, m0))
    block(pl.multiple_of(qi * T, T), ms, corr_ref[...] * sl2)

    dl = dl_ref[...]
    lam = (jnp.exp(jnp.sum(dl[0:1] * dl[1:2], axis=-1, keepdims=True))
           - jnp.exp(jnp.sum(dl[2:3] * dl[3:4], axis=-1, keepdims=True)) + lam_init)
    o_t = acc1[0:D2, :] / acc1[D2:D2 + 1, :] - lam * (acc2[0:D2, :] / acc2[D2:D2 + 1, :])
    o = o_t.T
    y = o * lax.rsqrt(jnp.mean(o * o, axis=-1, keepdims=True) + 1e-5) * ng_ref[...]
    y = y * (1.0 - lam_init)
    o_ref[0] = (y * _silu(z_ref[0])).astype(o_ref.dtype)


def _attn_corr_table():
    kk = jnp.arange(ATT_BLOCK)[:, None]
    qq = jnp.arange(ATT_BLOCK)[None, :]
    visible = (kk // CHUNK) <= (qq // CHUNK)
    return jnp.where(visible, -2.0 * jnp.maximum(kk - qq, 0).astype(F32), NEG_BIG)


def _attn_call(proj, slopes, dl, norm_g, lam_init):
    bsz, s, _ = proj.shape
    T = ATT_BLOCK
    hw = 2 * DIFF_HEAD_DIM
    blk = lambda off: off // hw
    const2 = lambda b, h, i: (0, 0)
    return pl.pallas_call(
        functools.partial(_attn_kernel, lam_init),
        grid=(bsz, DIFF_HEADS, s // T),
        in_specs=[pl.BlockSpec((1, T, hw), lambda b, h, i: (b, i, blk(P_OFF_QB) + h)),
                  pl.BlockSpec((1, s, hw), lambda b, h, i: (b, 0, blk(P_OFF_KB) + h)),
                  pl.BlockSpec((1, s, hw), lambda b, h, i: (b, 0, blk(P_OFF_VB) + h)),
                  pl.BlockSpec((1, T, hw), lambda b, h, i: (b, i, blk(P_OFF_ZB) + h)),
                  pl.BlockSpec((1, 1, LANES), lambda b, h, i: (h, 0, 0)),
                  pl.BlockSpec((T, T), const2),
                  pl.BlockSpec(dl.shape, const2),
                  pl.BlockSpec(norm_g.shape, const2)],
        out_specs=pl.BlockSpec((1, T, hw), lambda b, h, i: (b, i, h)),
        out_shape=jax.ShapeDtypeStruct((bsz, s, DIFF_WIDTH), BF16),
        scratch_shapes=[pltpu.VMEM((s, ATT_AUG), BF16),
                        pltpu.VMEM((ATT_ACC_ROWS, s), BF16),
                        pltpu.VMEM((ATT_ACC_ROWS, T), F32),
                        pltpu.VMEM((ATT_ACC_ROWS, T), F32)],
        compiler_params=_cparams(("parallel", "parallel", "arbitrary")),
        name="diff_attention",
    )(proj, proj, proj, proj, slopes, _attn_corr_table(), dl, norm_g)


def _mlstm_kernel(q_ref, k_ref, v_ref, o_ref_in, z_ref, ps_ref, gb_ref, out_ref, c_state, n_state, m_state):
    L = SCAN_CHUNK
    D = MLSTM_HEAD_DIM
    c = pl.program_id(1)

    @pl.when(c == 0)
    def _():
        c_state[...] = jnp.zeros(c_state.shape, F32)
        n_state[...] = jnp.zeros(n_state.shape, F32)
        m_state[...] = jnp.zeros(m_state.shape, F32)

    rows = lax.broadcasted_iota(jnp.int32, (L, L), 0)
    cols = lax.broadcasted_iota(jnp.int32, (L, L), 1)
    causal = rows >= cols
    tril = jnp.where(causal, 1.0, 0.0).astype(BF16)

    raw = ps_ref[0] + gb_ref[...]
    logf = -_softplus(-raw)
    bcum = _cumsum_rows(logf, tril)
    bcum_t = bcum.T
    raw_t = raw.T

    for h in range(MLSTM_HEADS):
        li, lf = LANE_I + h, LANE_F + h
        sl = slice(h * D, (h + 1) * D)
        q = q_ref[0, :, sl]
        ks = k_ref[0, :, sl] * (D ** -0.5)
        v = v_ref[0, :, sl]
        qb, kb = q.astype(BF16), ks.astype(BF16)

        b_col = bcum[:, lf:lf + 1]
        b_row = bcum_t[lf:lf + 1, :]
        ig_col = raw[:, li:li + 1]
        ig_row = raw_t[li:li + 1, :]
        btot = bcum[L - 1:L, lf:lf + 1]
        m_prev = m_state[h:h + 1, 0:1]
        n_prev = n_state[h:h + 1, :]
        c_prev = c_state[h]

        dmat = jnp.where(causal, b_col - b_row + ig_row, NEG_BIG)
        m_inter = b_col + m_prev
        m_row = jnp.maximum(m_inter, jnp.max(dmat, axis=-1, keepdims=True))
        wts = jnp.exp(dmat - m_row)
        sw = wts * _dot_nt(qb, kb)
        num = _dot(sw.astype(BF16), v.astype(BF16))
        den = jnp.sum(sw, axis=-1, keepdims=True)
        a_inter = jnp.exp(m_inter - m_row)
        num = num + _dot(qb, c_prev.astype(BF16)) * a_inter
        den = den + jnp.sum(q * n_prev, axis=-1, keepdims=True) * a_inter
        hout = num / jnp.maximum(jnp.abs(den), jnp.exp(-m_row))
        gate = jax.nn.sigmoid(o_ref_in[0, :, sl]) * _silu(z_ref[0, :, sl])
        out_ref[0, :, sl] = (gate * hout).astype(out_ref.dtype)

        w_end = btot - b_col + ig_col
        m_loc = jnp.max(w_end, axis=0, keepdims=True)
        m_new = jnp.maximum(btot + m_prev, m_loc)
        a_old = jnp.exp(btot + m_prev - m_new)
        e_end = jnp.exp(w_end - m_new)
        c_state[h] = a_old * c_prev + _dot_tn(kb, (v * e_end).astype(BF16))
        n_state[h:h + 1, :] = a_old * n_prev + jnp.sum(ks * e_end, axis=0, keepdims=True)
        m_state[h:h + 1, :] = jnp.broadcast_to(m_new, (1, LANES))


def _mlstm_call(proj, ps, gate_bias):
    bsz, s, _ = proj.shape
    L = SCAN_CHUNK
    W = MLSTM_WIDTH
    blk0 = P_OFF_C // W
    return pl.pallas_call(
        _mlstm_kernel,
        grid=(bsz, s // L),
        in_specs=[pl.BlockSpec((1, L, W), lambda b, c: (b, c, blk0)),
                  pl.BlockSpec((1, L, W), lambda b, c: (b, c, blk0 + 1)),
                  pl.BlockSpec((1, L, W), lambda b, c: (b, c, blk0 + 2)),
                  pl.BlockSpec((1, L, W), lambda b, c: (b, c, blk0 + 3)),
                  pl.BlockSpec((1, L, W), lambda b, c: (b, c, blk0 + 4)),
                  pl.BlockSpec((1, L, LANES), lambda b, c: (b, c, 0)),
                  pl.BlockSpec(gate_bias.shape, lambda b, c: (0, 0))],
        out_specs=pl.BlockSpec((1, L, W), lambda b, c: (b, c, 0)),
        out_shape=jax.ShapeDtypeStruct((bsz, s, W), BF16),
        scratch_shapes=[pltpu.VMEM((MLSTM_HEADS, MLSTM_HEAD_DIM, MLSTM_HEAD_DIM), F32),
                        pltpu.VMEM((MLSTM_HEADS, MLSTM_HEAD_DIM), F32),
                        pltpu.VMEM((MLSTM_HEADS, LANES), F32)],
        compiler_params=_cparams(("parallel", "arbitrary")),
        name="mlstm_scan",
    )(proj, proj, proj, proj, proj, ps, gate_bias)


def _merge_kernel(x_ref, oa_ref, ob_ref, oc_ref, wga_ref, wgb_ref, wgc_ref, wa_ref, wb_ref, wc_ref, o_ref):
    x = x_ref[...]
    acc = jax.nn.sigmoid(_dot(x, wga_ref[...])) * _dot(oa_ref[...], wa_ref[...])
    acc = acc + jax.nn.sigmoid(_dot(x, wgb_ref[...])) * _dot(ob_ref[...], wb_ref[...])
    acc = acc + jax.nn.sigmoid(_dot(x, wgc_ref[...])) * _dot(oc_ref[...], wc_ref[...])
    o_ref[...] = acc.astype(o_ref.dtype)


def _merge_call(xb, oa, ob, oc, w_all, w_br, li, tm, tn):
    m = xb.shape[0]
    row = lambda i, j: (i, 0)
    gate_spec = lambda br: pl.BlockSpec(
        (None, D_MODEL, tn), lambda i, j: (li, 0, (W_OFF_GATES + br * D_MODEL) // tn + j))
    return pl.pallas_call(
        _merge_kernel,
        grid=(m // tm, D_MODEL // tn),
        in_specs=[pl.BlockSpec((tm, D_MODEL), row),
                  pl.BlockSpec((tm, SSM_WIDTH), row),
                  pl.BlockSpec((tm, DIFF_WIDTH), row),
                  pl.BlockSpec((tm, MLSTM_WIDTH), row),
                  gate_spec(0), gate_spec(1), gate_spec(2),
                  pl.BlockSpec((None, SSM_WIDTH, tn), lambda i, j: (li, 0, j)),
                  pl.BlockSpec((None, DIFF_WIDTH, tn), lambda i, j: (li, SSM_WIDTH // DIFF_WIDTH, j)),
                  pl.BlockSpec((None, MLSTM_WIDTH, tn),
                               lambda i, j: (li, (SSM_WIDTH + DIFF_WIDTH) // MLSTM_WIDTH, j))],
        out_specs=pl.BlockSpec((tm, tn), lambda i, j: (i, j)),
        out_shape=jax.ShapeDtypeStruct((m, D_MODEL), BF16),
        compiler_params=_cparams(("parallel", "arbitrary")),
        name="gated_merge",
    )(xb, oa, ob, oc, w_all, w_all, w_all, w_br, w_br, w_br)


def _final_kernel(nj, tn, mg_ref, wo_ref, wg_ref, x_ref, p_ref, wple_ref, lng_ref, lnb_ref, pg_ref,
                  o_ref, ob_ref, h_scr, hb_scr, rs_scr):
    j = pl.program_id(1)
    tm = h_scr.shape[0]

    @pl.when(j < nj)
    def _():
        off = pl.multiple_of(j * tn, tn)
        h_scr[:, pl.ds(off, tn)] = DEEPNORM_ALPHA * x_ref[...] + _dot(mg_ref[...], wo_ref[...])

    @pl.when(j == nj)
    def _():
        def norm_rows(r, carry):
            rows = pl.ds(pl.multiple_of(r * NORM_ROWS, NORM_ROWS), NORM_ROWS)
            pre = h_scr[rows, :]
            mu = jnp.mean(pre, axis=-1, keepdims=True)
            cen = pre - mu
            var = jnp.mean(cen * cen, axis=-1, keepdims=True)
            h = cen * lax.rsqrt(var + 1e-5) * lng_ref[...] + lnb_ref[...]
            h_scr[rows, :] = h
            hb_scr[rows, :] = h.astype(BF16)
            e = _dot(p_ref[rows, :], wple_ref[...])
            rs = lax.rsqrt(jnp.mean(e * e, axis=-1, keepdims=True) + 1e-5)
            rs_scr[rows, :] = jnp.broadcast_to(rs, (NORM_ROWS, LANES))
            return carry

        lax.fori_loop(0, tm // NORM_ROWS, norm_rows, 0)

    @pl.when(j >= nj)
    def _():
        off = pl.multiple_of((j - nj) * tn, tn)
        gate = jax.nn.sigmoid(_dot(hb_scr[...], wg_ref[...]))
        e = _dot(p_ref[...], wple_ref[:, pl.ds(off, tn)]) * rs_scr[:, 0:1] * pg_ref[:, pl.ds(off, tn)]
        out = h_scr[:, pl.ds(off, tn)] + gate * e
        o_ref[...] = out
        ob_ref[...] = out.astype(BF16)


def _final_call(merged, w_out, w_pg, x, pb, w_ple, ln_g, ln_b, ple_g, li, tm, tn):
    m = merged.shape[0]
    nj = D_MODEL // tn
    full = lambda i, j: (0, 0)
    out_idx = lambda i, j: (i, jnp.maximum(j - nj, 0))
    return pl.pallas_call(
        functools.partial(_final_kernel, nj, tn),
        grid=(m // tm, 2 * nj),
        in_specs=[pl.BlockSpec((tm, D_MODEL), lambda i, j: (i, 0)),
                  pl.BlockSpec((None, D_MODEL, tn), lambda i, j: (li, 0, jnp.minimum(j, nj - 1))),
                  pl.BlockSpec((None, D_MODEL, tn), lambda i, j: (li, 0, jnp.maximum(j - nj, 0))),
                  pl.BlockSpec((tm, tn), lambda i, j: (i, jnp.minimum(j, nj - 1))),
                  pl.BlockSpec((tm, PLE_DIM), lambda i, j: (i, 0)),
                  pl.BlockSpec((None,) + w_ple.shape[1:], lambda i, j: (li, 0, 0)),
                  pl.BlockSpec(ln_g.shape, full),
                  pl.BlockSpec(ln_b.shape, full),
                  pl.BlockSpec(ple_g.shape, full)],
        out_specs=[pl.BlockSpec((tm, tn), out_idx), pl.BlockSpec((tm, tn), out_idx)],
        out_shape=[jax.ShapeDtypeStruct((m, D_MODEL), F32), jax.ShapeDtypeStruct((m, D_MODEL), BF16)],
        scratch_shapes=[pltpu.VMEM((tm, D_MODEL), F32),
                        pltpu.VMEM((tm, D_MODEL), BF16),
                        pltpu.VMEM((tm, LANES), F32)],
        compiler_params=_cparams(("parallel", "arbitrary")),
        name="outproj_deepnorm_ple",
    )(merged, w_out, w_pg, x, pb, w_ple, ln_g, ln_b, ple_g)


def _row_tile(m, pref):
    return pref if m % pref == 0 else m


def _prep_weights(w_in, w_branch, w_out, w_ple, w_ple_gate):
    small_pad = jnp.zeros(w_in.shape[:2] + (LANES - SSM_HEADS - 2 * MLSTM_HEADS,), w_in.dtype)
    w_all = jnp.concatenate(
        [w_in[..., OFF_XBC:OFF_DT], w_in[..., OFF_QB:OFF_IC], w_in[..., OFF_GATES:],
         w_in[..., OFF_DT:OFF_QB], w_in[..., OFF_IC:OFF_GATES], small_pad], axis=-1).astype(BF16)
    return w_all, w_branch.astype(BF16), w_out.astype(BF16), w_ple.astype(BF16), w_ple_gate.astype(BF16)


def _layer(x, xb, p_i, li, weights, conv_w, conv_b, dt_bias, a_log, d_skip, ssm_norm_g,
           diff_lambda, diff_norm_g, mlstm_gate_b, ln_g, ln_b, ple_norm_g):
    w_all, w_br, w_out, w_ple, w_pg = weights
    bsz, s, _ = x.shape
    t = bsz * s
    x2 = x.reshape(t, D_MODEL)
    xb2 = xb.reshape(t, D_MODEL)

    proj, ps = _proj_call(xb2, w_all, li, _row_tile(t, 1024), 1024)
    proj = proj.reshape(bsz, s, PROJ_WIDTH)
    ps = ps.reshape(bsz, s, LANES)

    pad = lambda v, lane0: jnp.zeros((1, LANES), F32).at[0, lane0:lane0 + v.shape[0]].set(v.astype(F32))
    out_a = _ssd_call(proj, ps, conv_w, conv_b.reshape(1, -1), pad(dt_bias, LANE_DT), pad(a_log, LANE_DT),
                      jnp.repeat(d_skip.astype(F32), SSM_HEAD_DIM).reshape(1, -1), ssm_norm_g.reshape(1, -1))

    lam_init = 0.8 - 0.6 * math.exp(-0.3 * li)
    slopes = 2.0 ** (-8.0 * jnp.arange(1, DIFF_HEADS + 1, dtype=F32) / DIFF_HEADS)
    slopes = jnp.broadcast_to(slopes[:, None, None], (DIFF_HEADS, 1, LANES))
    out_b = _attn_call(proj, slopes, diff_lambda, diff_norm_g.reshape(1, -1), lam_init)

    gate_bias = pad(mlstm_gate_b[0], LANE_I) + pad(mlstm_gate_b[1], LANE_F)
    out_c = _mlstm_call(proj, ps, gate_bias)

    tm2 = _row_tile(t, 512)
    merged = _merge_call(xb2, out_a.reshape(t, -1), out_b.reshape(t, -1), out_c.reshape(t, -1),
                         w_all, w_br, li, tm2, 256)
    out, out_b16 = _final_call(merged, w_out, w_pg, x2, p_i.reshape(t, PLE_DIM).astype(BF16), w_ple,
                               ln_g.reshape(1, -1), ln_b.reshape(1, -1), ple_norm_g.reshape(1, -1), li, tm2, 512)
    return out.reshape(bsz, s, D_MODEL), out_b16.reshape(bsz, s, D_MODEL)


def kernel(x, p, w_in, conv_w, conv_b, dt_bias, a_log, d_skip, ssm_norm_g, diff_lambda, diff_norm_g,
           mlstm_gate_b, w_branch, w_out, ln_g, ln_b, w_ple, ple_norm_g, w_ple_gate):
    weights = _prep_weights(w_in, w_branch, w_out, w_ple, w_ple_gate)
    h = x
    hb = x.astype(BF16)
    for li in range(w_in.shape[0]):
        h, hb = _layer(h, hb, p[li], li, weights, conv_w[li], conv_b[li], dt_bias[li], a_log[li],
                       d_skip[li], ssm_norm_g[li], diff_lambda[li], diff_norm_g[li],
                       mlstm_gate_b[li], ln_g[li], ln_b[li], ple_norm_g[li])
    return h
```

```python
import functools
import math

import jax
import jax.numpy as jnp
from jax import lax
from jax.experimental import pallas as pl
from jax.experimental.pallas import tpu as pltpu

F32 = jnp.float32
BF16 = jnp.bfloat16

V7X_VMEM_LIMIT_BYTES = 56 * 1024 * 1024
LANES = 128

D_MODEL = 4096
CHUNK = 64
SSM_WIDTH = 2048
SSM_HEAD_DIM = 64
SSM_HEADS = 32
SSM_GROUPS = 8
SSM_STATE = 128
SSM_XBC = SSM_WIDTH + 2 * SSM_GROUPS * SSM_STATE
GROUP_WIDTH = SSM_WIDTH // SSM_GROUPS
HEADS_PER_GROUP = SSM_HEADS // SSM_GROUPS
DIFF_WIDTH = 1024
DIFF_HEAD_DIM = 64
DIFF_HEADS = 8
MLSTM_WIDTH = 1024
MLSTM_HEAD_DIM = 128
MLSTM_HEADS = 8
N_BRANCH = 3
PLE_DIM = 256
DEPTH = 2
DEEPNORM_ALPHA = (2.0 * DEPTH) ** 0.25

OFF_XBC = 0
OFF_ZA = OFF_XBC + SSM_XBC
OFF_DT = OFF_ZA + SSM_WIDTH
OFF_QB = OFF_DT + SSM_HEADS
OFF_ZB = OFF_QB + 3 * DIFF_WIDTH
OFF_QC = OFF_ZB + DIFF_WIDTH
OFF_IC = OFF_QC + 5 * MLSTM_WIDTH
OFF_FC = OFF_IC + MLSTM_HEADS
OFF_GATES = OFF_FC + MLSTM_HEADS

P_OFF_XBC = 0
P_OFF_ZA = P_OFF_XBC + SSM_XBC
P_OFF_QB = P_OFF_ZA + SSM_WIDTH
P_OFF_KB = P_OFF_QB + DIFF_WIDTH
P_OFF_VB = P_OFF_KB + DIFF_WIDTH
P_OFF_ZB = P_OFF_VB + DIFF_WIDTH
P_OFF_C = P_OFF_ZB + DIFF_WIDTH
PROJ_WIDTH = P_OFF_C + 5 * MLSTM_WIDTH
W_OFF_GATES = PROJ_WIDTH
W_OFF_SMALL = W_OFF_GATES + N_BRANCH * D_MODEL

LANE_DT = 0
LANE_I = SSM_HEADS
LANE_F = SSM_HEADS + MLSTM_HEADS

SCAN_CHUNK = 256
ATT_BLOCK = 512
ATT_AUG = 2 * LANES
ATT_ACC_ROWS = LANES + 16
POS_SPLIT = 256
NORM_ROWS = 64
NEG_BIG = -1e30
LOG2E = 1.4426950408889634


def _cparams(sem):
    return pltpu.CompilerParams(dimension_semantics=sem, vmem_limit_bytes=V7X_VMEM_LIMIT_BYTES)


def _silu(x):
    return x * jax.nn.sigmoid(x)


def _softplus(x):
    return jnp.maximum(x, 0.0) + jnp.log1p(jnp.exp(-jnp.abs(x)))


def _dot(a, b):
    return jnp.dot(a, b, preferred_element_type=F32)


def _dot_nt(a, b):
    return lax.dot_general(a, b, (((1,), (1,)), ((), ())), preferred_element_type=F32)


def _dot_tn(a, b):
    return lax.dot_general(a, b, (((0,), (0,)), ((), ())), preferred_element_type=F32)


def _cumsum_rows(x, tril_bf16):
    hi = x.astype(BF16)
    r1 = x - hi.astype(F32)
    mid = r1.astype(BF16)
    lo = (r1 - mid.astype(F32)).astype(BF16)
    return _dot(tril_bf16, hi) + _dot(tril_bf16, mid) + _dot(tril_bf16, lo)


def _proj_kernel(x_ref, w_ref, ws_ref, o_ref, os_ref):
    o_ref[...] = _dot(x_ref[...], w_ref[...])

    @pl.when(pl.program_id(1) == 0)
    def _():
        os_ref[...] = _dot(x_ref[...], ws_ref[...])


def _proj_call(xb, w_main, w_small, li, tm, tn):
    m, k = xb.shape
    return pl.pallas_call(
        _proj_kernel,
        grid=(m // tm, PROJ_WIDTH // tn),
        in_specs=[pl.BlockSpec((tm, k), lambda i, j: (i, 0)),
                  pl.BlockSpec((None, k, tn), lambda i, j: (li, 0, j)),
                  pl.BlockSpec((None, k, LANES), lambda i, j: (li, 0, 0))],
        out_specs=[pl.BlockSpec((tm, tn), lambda i, j: (i, j)),
                   pl.BlockSpec((tm, LANES), lambda i, j: (i, 0))],
        out_shape=[jax.ShapeDtypeStruct((m, PROJ_WIDTH), F32), jax.ShapeDtypeStruct((m, LANES), F32)],
        compiler_params=_cparams(("parallel", "arbitrary")),
        name="in_proj",
    )(xb, w_main, w_small)


def _ssd_kernel(xbc_ref, z_ref, ps_ref, cw_ref, cb_ref, dtb_ref, alog_ref, dskip_ref, ng_ref,
                o_ref, xpad, state, ybuf):
    L = SCAN_CHUNK
    c = pl.program_id(1)

    @pl.when(c == 0)
    def _():
        xpad[0:8, :] = jnp.zeros((8, SSM_XBC), F32)
        state[...] = jnp.zeros(state.shape, F32)

    @pl.when(c > 0)
    def _():
        xpad[0:8, :] = xpad[L:L + 8, :]

    xpad[8:8 + L, :] = xbc_ref[0]

    def conv_silu(c0, width):
        acc = cb_ref[:, c0:c0 + width] + cw_ref[3:4, c0:c0 + width] * xpad[8:8 + L, c0:c0 + width]
        acc = acc + cw_ref[2:3, c0:c0 + width] * xpad[7:7 + L, c0:c0 + width]
        acc = acc + cw_ref[1:2, c0:c0 + width] * xpad[6:6 + L, c0:c0 + width]
        acc = acc + cw_ref[0:1, c0:c0 + width] * xpad[5:5 + L, c0:c0 + width]
        return _silu(acc)

    rows = lax.broadcasted_iota(jnp.int32, (L, L), 0)
    cols = lax.broadcasted_iota(jnp.int32, (L, L), 1)
    causal = rows >= cols
    tril = jnp.where(causal, 1.0, 0.0).astype(BF16)

    dt = _softplus(ps_ref[0] + dtb_ref[...])
    la = dt * (-jnp.exp(alog_ref[...]))
    lc = _cumsum_rows(la, tril)
    lc_t = lc.T

    head_of_lane = lax.broadcasted_iota(jnp.int32, (L, GROUP_WIDTH), 1) // SSM_HEAD_DIM

    def expand(vals, hol):
        out = vals[HEADS_PER_GROUP - 1]
        for r in range(HEADS_PER_GROUP - 2, -1, -1):
            out = jnp.where(hol == r, vals[r], out)
        return out

    ssq = jnp.zeros((L, 1), F32)
    for g in range(SSM_GROUPS):
        h0 = g * HEADS_PER_GROUP
        xg = conv_silu(g * GROUP_WIDTH, GROUP_WIDTH)
        bg = conv_silu(SSM_WIDTH + g * SSM_STATE, SSM_STATE).astype(BF16)
        cg = conv_silu(SSM_WIDTH + SSM_GROUPS * SSM_STATE + g * SSM_STATE, SSM_STATE).astype(BF16)

        dt4 = expand([dt[:, h0 + r:h0 + r + 1] for r in range(HEADS_PER_GROUP)], head_of_lane)
        lc4 = expand([lc[:, h0 + r:h0 + r + 1] for r in range(HEADS_PER_GROUP)], head_of_lane)
        lc_end4 = expand([lc[L - 1:L, h0 + r:h0 + r + 1] for r in range(HEADS_PER_GROUP)],
                         head_of_lane[0:1, :])
        xdt = xg * dt4

        cb = _dot_nt(cg, bg)
        y = jnp.zeros((L, GROUP_WIDTH), F32)
        for r in range(HEADS_PER_GROUP):
            h = h0 + r
            seg = lc[:, h:h + 1] - lc_t[h:h + 1, :]
            decay = jnp.where(causal, jnp.exp(seg), 0.0)
            m_r = (cb * decay).astype(BF16)
            x_r = jnp.where(head_of_lane == r, xdt, 0.0).astype(BF16)
            y = y + _dot(m_r, x_r)

        prev = state[g]
        y = y + _dot(cg, prev.astype(BF16)) * jnp.exp(lc4)
        xs = (xdt * jnp.exp(lc_end4 - lc4)).astype(BF16)
        state[g] = prev * jnp.exp(lc_end4) + _dot_tn(bg, xs)

        y = y + dskip_ref[:, g * GROUP_WIDTH:(g + 1) * GROUP_WIDTH] * xg
        yz = y * _silu(z_ref[0, :, g * GROUP_WIDTH:(g + 1) * GROUP_WIDTH])
        ybuf[:, g * GROUP_WIDTH:(g + 1) * GROUP_WIDTH] = yz
        ssq = ssq + jnp.sum(yz * yz, axis=-1, keepdims=True)

    rs = lax.rsqrt(ssq * (1.0 / SSM_WIDTH) + 1e-5)
    o_ref[0] = (ybuf[...] * rs * ng_ref[...]).astype(o_ref.dtype)


def _ssd_call(pa, ps, conv_w, conv_b, dtb, alog, dskip, norm_g):
    bsz, s, _ = pa.shape
    L = SCAN_CHUNK
    full = lambda b, c: (0, 0)
    return pl.pallas_call(
        _ssd_kernel,
        grid=(bsz, s // L),
        in_specs=[pl.BlockSpec((1, L, SSM_XBC), lambda b, c: (b, c, 0)),
                  pl.BlockSpec((1, L, SSM_WIDTH), lambda b, c: (b, c, SSM_XBC // SSM_WIDTH)),
                  pl.BlockSpec((1, L, LANES), lambda b, c: (b, c, 0)),
                  pl.BlockSpec(conv_w.shape, full),
                  pl.BlockSpec(conv_b.shape, full),
                  pl.BlockSpec(dtb.shape, full),
                  pl.BlockSpec(alog.shape, full),
                  pl.BlockSpec(dskip.shape, full),
                  pl.BlockSpec(norm_g.shape, full)],
        out_specs=pl.BlockSpec((1, L, SSM_WIDTH), lambda b, c: (b, c, 0)),
        out_shape=jax.ShapeDtypeStruct((bsz, s, SSM_WIDTH), BF16),
        scratch_shapes=[pltpu.VMEM((L + 8, SSM_XBC), F32),
                        pltpu.VMEM((SSM_GROUPS, SSM_STATE, GROUP_WIDTH), F32),
                        pltpu.VMEM((L, SSM_WIDTH), F32)],
        compiler_params=_cparams(("parallel", "arbitrary")),
        name="ssd_scan",
    )(pa, pa, ps, conv_w, conv_b, dtb, alog, dskip, norm_g)


def _attn_kernel(lam_init, q_ref, k_ref, v_ref, z_ref, slope_ref, corr_ref, dl_ref, ng_ref, o_ref,
                 k_aug, vt_aug, acc1, acc2):
    T = ATT_BLOCK
    D2 = 2 * DIFF_HEAD_DIM
    qi = pl.program_id(2)
    n_kv = k_ref.shape[1] // T

    @pl.when(qi == 0)
    def _():
        lane = lax.broadcasted_iota(jnp.int32, (T, LANES), 1)
        row = lax.broadcasted_iota(jnp.int32, (T, LANES), 0)
        for c in range(n_kv):
            rows = slice(c * T, (c + 1) * T)
            pos = row + c * T
            pos_lo = jnp.bitwise_and(pos, POS_SPLIT - 1)
            pos_hi = pos - pos_lo
            pos_cols = jnp.where(lane < 3, pos_lo, jnp.where(lane < 6, pos_hi, 0))
            k_aug[rows, 0:D2] = k_ref[0, rows, :].astype(BF16)
            k_aug[rows, D2:ATT_AUG] = pos_cols.astype(F32).astype(BF16)
            vt_aug[0:D2, rows] = v_ref[0, rows, :].T.astype(BF16)
        ones_row = lax.broadcasted_iota(jnp.int32, (ATT_ACC_ROWS - D2, k_ref.shape[1]), 0) == 0
        vt_aug[D2:ATT_ACC_ROWS, :] = jnp.where(ones_row, 1.0, 0.0).astype(BF16)

    lane = lax.broadcasted_iota(jnp.int32, (T, D2), 1)
    qs = q_ref[0] * (DIFF_HEAD_DIM ** -0.5 * LOG2E)
    sl2 = slope_ref[0][:, 0:1] * LOG2E
    sl_hi = sl2.astype(BF16).astype(F32)
    sl_mid = (sl2 - sl_hi).astype(BF16).astype(F32)
    sl_lo = sl2 - sl_hi - sl_mid
    lane1 = lax.broadcasted_iota(jnp.int32, (1, LANES), 1)
    ext = jnp.where((lane1 == 0) | (lane1 == 3), sl_hi,
                    jnp.where((lane1 == 1) | (lane1 == 4), sl_mid,
                              jnp.where((lane1 == 2) | (lane1 == 5), sl_lo, 0.0)))
    ext = jnp.broadcast_to(ext, (T, LANES)).astype(BF16)
    qa1 = jnp.concatenate([jnp.where(lane < DIFF_HEAD_DIM, qs, 0.0).astype(BF16), ext], axis=1)
    qa2 = jnp.concatenate([jnp.where(lane >= DIFF_HEAD_DIM, qs, 0.0).astype(BF16), ext], axis=1)

    acc1[...] = jnp.zeros(acc1.shape, F32)
    acc2[...] = jnp.zeros(acc2.shape, F32)

    def block(first, n, ms, corr):
        start = pl.multiple_of(first * T, T)
        kb = k_aug[pl.ds(start, n * T), :]
        vb = vt_aug[:, pl.ds(start, n * T)]

        def one(qa, m, acc):
            s = _dot_nt(kb, qa)
            if corr is not None and n == 1:
                s = s + corr
            elif corr is not None:
                s = jnp.concatenate([s[:(n - 1) * T], s[(n - 1) * T:] + corr], axis=0)
            m_new = jnp.maximum(m, jnp.max(s, axis=0, keepdims=True))
            p = jnp.exp2(s - m_new).astype(BF16)
            acc[...] = jnp.exp2(m - m_new) * acc[...] + _dot(vb, p)
            return m_new

        return one(qa1, ms[0], acc1), one(qa2, ms[1], acc2)

    m0 = jnp.full((1, T), NEG_BIG, F32)
    ms = lax.fori_loop(0, qi // 2, lambda i, ms: block(2 * i, 2, ms, None), (m0, m0))
    corr = corr_ref[...] * sl2

    @pl.when(qi % 2 == 1)
    def _():
        block(qi - 1, 2, ms, corr)

    @pl.when(qi % 2 == 0)
    def _():
        block(qi, 1, ms, corr)

    dl = dl_ref[...]
    lam = (jnp.exp(jnp.sum(dl[0:1] * dl[1:2], axis=-1, keepdims=True))
           - jnp.exp(jnp.sum(dl[2:3] * dl[3:4], axis=-1, keepdims=True)) + lam_init)
    o_t = acc1[0:D2, :] / acc1[D2:D2 + 1, :] - lam * (acc2[0:D2, :] / acc2[D2:D2 + 1, :])
    o = o_t.T
    y = o * lax.rsqrt(jnp.mean(o * o, axis=-1, keepdims=True) + 1e-5) * ng_ref[...]
    y = y * (1.0 - lam_init)
    o_ref[0] = (y * _silu(z_ref[0])).astype(o_ref.dtype)


def _attn_corr_table():
    kk = jnp.arange(ATT_BLOCK)[:, None]
    qq = jnp.arange(ATT_BLOCK)[None, :]
    visible = (kk // CHUNK) <= (qq // CHUNK)
    return jnp.where(visible, -2.0 * jnp.maximum(kk - qq, 0).astype(F32), NEG_BIG)


def _attn_call(proj, slopes, dl, norm_g, lam_init):
    bsz, s, _ = proj.shape
    T = ATT_BLOCK
    hw = 2 * DIFF_HEAD_DIM
    blk = lambda off: off // hw
    const2 = lambda b, h, i: (0, 0)
    return pl.pallas_call(
        functools.partial(_attn_kernel, lam_init),
        grid=(bsz, DIFF_HEADS, s // T),
        in_specs=[pl.BlockSpec((1, T, hw), lambda b, h, i: (b, i, blk(P_OFF_QB) + h)),
                  pl.BlockSpec((1, s, hw), lambda b, h, i: (b, 0, blk(P_OFF_KB) + h)),
                  pl.BlockSpec((1, s, hw), lambda b, h, i: (b, 0, blk(P_OFF_VB) + h)),
                  pl.BlockSpec((1, T, hw), lambda b, h, i: (b, i, blk(P_OFF_ZB) + h)),
                  pl.BlockSpec((1, 1, LANES), lambda b, h, i: (h, 0, 0)),
                  pl.BlockSpec((T, T), const2),
                  pl.BlockSpec(dl.shape, const2),
                  pl.BlockSpec(norm_g.shape, const2)],
        out_specs=pl.BlockSpec((1, T, hw), lambda b, h, i: (b, i, h)),
        out_shape=jax.ShapeDtypeStruct((bsz, s, DIFF_WIDTH), BF16),
        scratch_shapes=[pltpu.VMEM((s, ATT_AUG), BF16),
                        pltpu.VMEM((ATT_ACC_ROWS, s), BF16),
                        pltpu.VMEM((ATT_ACC_ROWS, T), F32),
                        pltpu.VMEM((ATT_ACC_ROWS, T), F32)],
        compiler_params=_cparams(("parallel", "parallel", "arbitrary")),
        name="diff_attention",
    )(proj, proj, proj, proj, slopes, _attn_corr_table(), dl, norm_g)


def _mlstm_kernel(q_ref, k_ref, v_ref, o_ref_in, z_ref, ps_ref, gb_ref, out_ref, c_state, m_state):
    L = SCAN_CHUNK
    D = MLSTM_HEAD_DIM
    c = pl.program_id(1)

    @pl.when(c == 0)
    def _():
        c_state[...] = jnp.zeros(c_state.shape, F32)
        m_state[...] = jnp.zeros(m_state.shape, F32)

    ones_col = jnp.where(lax.broadcasted_iota(jnp.int32, (L, D), 1) == 0, 1.0, 0.0)

    rows = lax.broadcasted_iota(jnp.int32, (L, L), 0)
    cols = lax.broadcasted_iota(jnp.int32, (L, L), 1)
    causal = rows >= cols
    tril = jnp.where(causal, 1.0, 0.0).astype(BF16)

    raw = ps_ref[0] + gb_ref[...]
    logf = -_softplus(-raw)
    bcum = _cumsum_rows(logf, tril)
    bcum_t = bcum.T
    raw_t = raw.T

    for h in range(MLSTM_HEADS):
        li, lf = LANE_I + h, LANE_F + h
        sl = slice(h * D, (h + 1) * D)
        q = q_ref[0, :, sl]
        ks = k_ref[0, :, sl] * (D ** -0.5)
        v_ext = jnp.concatenate([v_ref[0, :, sl], ones_col], axis=1)
        qb, kb = q.astype(BF16), ks.astype(BF16)

        b_col = bcum[:, lf:lf + 1]
        b_row = bcum_t[lf:lf + 1, :]
        ig_col = raw[:, li:li + 1]
        ig_row = raw_t[li:li + 1, :]
        btot = bcum[L - 1:L, lf:lf + 1]
        m_prev = m_state[h:h + 1, 0:1]
        c_prev = c_state[h]

        dmat = jnp.where(causal, b_col - b_row + ig_row, NEG_BIG)
        m_inter = b_col + m_prev
        m_row = jnp.maximum(m_inter, jnp.max(dmat, axis=-1, keepdims=True))
        wts = jnp.exp(dmat - m_row)
        sw = wts * _dot_nt(qb, kb)
        a_inter = jnp.exp(m_inter - m_row)
        nd = _dot(sw.astype(BF16), v_ext.astype(BF16)) + _dot(qb, c_prev.astype(BF16)) * a_inter
        hout = nd[:, :D] / jnp.maximum(jnp.abs(nd[:, D:D + 1]), jnp.exp(-m_row))
        gate = jax.nn.sigmoid(o_ref_in[0, :, sl]) * _silu(z_ref[0, :, sl])
        out_ref[0, :, sl] = (gate * hout).astype(out_ref.dtype)

        w_end = btot - b_col + ig_col
        m_loc = jnp.max(w_end, axis=0, keepdims=True)
        m_new = jnp.maximum(btot + m_prev, m_loc)
        a_old = jnp.exp(btot + m_prev - m_new)
        e_end = jnp.exp(w_end - m_new)
        c_state[h] = a_old * c_prev + _dot_tn(kb, (v_ext * e_end).astype(BF16))
        m_state[h:h + 1, :] = jnp.broadcast_to(m_new, (1, LANES))


def _mlstm_call(proj, ps, gate_bias):
    bsz, s, _ = proj.shape
    L = SCAN_CHUNK
    W = MLSTM_WIDTH
    blk0 = P_OFF_C // W
    return pl.pallas_call(
        _mlstm_kernel,
        grid=(bsz, s // L),
        in_specs=[pl.BlockSpec((1, L, W), lambda b, c: (b, c, blk0)),
                  pl.BlockSpec((1, L, W), lambda b, c: (b, c, blk0 + 1)),
                  pl.BlockSpec((1, L, W), lambda b, c: (b, c, blk0 + 2)),
                  pl.BlockSpec((1, L, W), lambda b, c: (b, c, blk0 + 3)),
                  pl.BlockSpec((1, L, W), lambda b, c: (b, c, blk0 + 4)),
                  pl.BlockSpec((1, L, LANES), lambda b, c: (b, c, 0)),
                  pl.BlockSpec(gate_bias.shape, lambda b, c: (0, 0))],
        out_specs=pl.BlockSpec((1, L, W), lambda b, c: (b, c, 0)),
        out_shape=jax.ShapeDtypeStruct((bsz, s, W), BF16),
        scratch_shapes=[pltpu.VMEM((MLSTM_HEADS, MLSTM_HEAD_DIM, 2 * MLSTM_HEAD_DIM), F32),
                        pltpu.VMEM((MLSTM_HEADS, LANES), F32)],
        compiler_params=_cparams(("parallel", "arbitrary")),
        name="mlstm_scan",
    )(proj, proj, proj, proj, proj, ps, gate_bias)


def _merge_kernel(x_ref, oa_ref, ob_ref, oc_ref, wga_ref, wgb_ref, wgc_ref, wa_ref, wb_ref, wc_ref, o_ref):
    x = x_ref[...]
    acc = jax.nn.sigmoid(_dot(x, wga_ref[...])) * _dot(oa_ref[...], wa_ref[...])
    acc = acc + jax.nn.sigmoid(_dot(x, wgb_ref[...])) * _dot(ob_ref[...], wb_ref[...])
    acc = acc + jax.nn.sigmoid(_dot(x, wgc_ref[...])) * _dot(oc_ref[...], wc_ref[...])
    o_ref[...] = acc.astype(o_ref.dtype)


def _merge_call(xb, oa, ob, oc, w_all, w_br, li, tm, tn):
    m = xb.shape[0]
    row = lambda i, j: (i, 0)
    gate_spec = lambda br: pl.BlockSpec(
        (None, D_MODEL, tn), lambda i, j: (li, 0, (W_OFF_GATES + br * D_MODEL) // tn + j))
    return pl.pallas_call(
        _merge_kernel,
        grid=(m // tm, D_MODEL // tn),
        in_specs=[pl.BlockSpec((tm, D_MODEL), row),
                  pl.BlockSpec((tm, SSM_WIDTH), row),
                  pl.BlockSpec((tm, DIFF_WIDTH), row),
                  pl.BlockSpec((tm, MLSTM_WIDTH), row),
                  gate_spec(0), gate_spec(1), gate_spec(2),
                  pl.BlockSpec((None, SSM_WIDTH, tn), lambda i, j: (li, 0, j)),
                  pl.BlockSpec((None, DIFF_WIDTH, tn), lambda i, j: (li, SSM_WIDTH // DIFF_WIDTH, j)),
                  pl.BlockSpec((None, MLSTM_WIDTH, tn),
                               lambda i, j: (li, (SSM_WIDTH + DIFF_WIDTH) // MLSTM_WIDTH, j))],
        out_specs=pl.BlockSpec((tm, tn), lambda i, j: (i, j)),
        out_shape=jax.ShapeDtypeStruct((m, D_MODEL), BF16),
        compiler_params=_cparams(("parallel", "arbitrary")),
        name="gated_merge",
    )(xb, oa, ob, oc, w_all, w_all, w_all, w_br, w_br, w_br)


def _final_kernel(nj, tn, mg_ref, wo_ref, wg_ref, x_ref, p_ref, wple_ref, lng_ref, lnb_ref, pg_ref,
                  o_ref, ob_ref, h_scr, hb_scr, rs_scr):
    j = pl.program_id(1)
    tm = h_scr.shape[0]

    @pl.when(j < nj)
    def _():
        off = pl.multiple_of(j * tn, tn)
        h_scr[:, pl.ds(off, tn)] = DEEPNORM_ALPHA * x_ref[...] + _dot(mg_ref[...], wo_ref[...])

    @pl.when(j == nj)
    def _():
        def norm_rows(r, carry):
            rows = pl.ds(pl.multiple_of(r * NORM_ROWS, NORM_ROWS), NORM_ROWS)
            pre = h_scr[rows, :]
            mu = jnp.mean(pre, axis=-1, keepdims=True)
            cen = pre - mu
            var = jnp.mean(cen * cen, axis=-1, keepdims=True)
            h = cen * lax.rsqrt(var + 1e-5) * lng_ref[...] + lnb_ref[...]
            h_scr[rows, :] = h
            hb_scr[rows, :] = h.astype(BF16)
            e = _dot(p_ref[rows, :], wple_ref[...])
            rs = lax.rsqrt(jnp.mean(e * e, axis=-1, keepdims=True) + 1e-5)
            rs_scr[rows, :] = jnp.broadcast_to(rs, (NORM_ROWS, LANES))
            return carry

        lax.fori_loop(0, tm // NORM_ROWS, norm_rows, 0)

    @pl.when(j >= nj)
    def _():
        off = pl.multiple_of((j - nj) * tn, tn)
        gate = jax.nn.sigmoid(_dot(hb_scr[...], wg_ref[...]))
        e = _dot(p_ref[...], wple_ref[:, pl.ds(off, tn)]) * rs_scr[:, 0:1] * pg_ref[:, pl.ds(off, tn)]
        out = h_scr[:, pl.ds(off, tn)] + gate * e
        o_ref[...] = out
        ob_ref[...] = out.astype(BF16)


def _final_call(merged, w_out, w_pg, x, pb, w_ple, ln_g, ln_b, ple_g, li, tm, tn):
    m = merged.shape[0]
    nj = D_MODEL // tn
    full = lambda i, j: (0, 0)
    out_idx = lambda i, j: (i, jnp.maximum(j - nj, 0))
    return pl.pallas_call(
        functools.partial(_final_kernel, nj, tn),
        grid=(m // tm, 2 * nj),
        in_specs=[pl.BlockSpec((tm, D_MODEL), lambda i, j: (i, 0)),
                  pl.BlockSpec((None, D_MODEL, tn), lambda i, j: (li, 0, jnp.minimum(j, nj - 1))),
                  pl.BlockSpec((None, D_MODEL, tn), lambda i, j: (li, 0, jnp.maximum(j - nj, 0))),
                  pl.BlockSpec((tm, tn), lambda i, j: (i, jnp.minimum(j, nj - 1))),
                  pl.BlockSpec((tm, PLE_DIM), lambda i, j: (i, 0)),
                  pl.BlockSpec((None,) + w_ple.shape[1:], lambda i, j: (li, 0, 0)),
                  pl.BlockSpec(ln_g.shape, full),
                  pl.BlockSpec(ln_b.shape, full),
                  pl.BlockSpec(ple_g.shape, full)],
        out_specs=[pl.BlockSpec((tm, tn), out_idx), pl.BlockSpec((tm, tn), out_idx)],
        out_shape=[jax.ShapeDtypeStruct((m, D_MODEL), F32), jax.ShapeDtypeStruct((m, D_MODEL), BF16)],
        scratch_shapes=[pltpu.VMEM((tm, D_MODEL), F32),
                        pltpu.VMEM((tm, D_MODEL), BF16),
                        pltpu.VMEM((tm, LANES), F32)],
        compiler_params=_cparams(("parallel", "arbitrary")),
        name="outproj_deepnorm_ple",
    )(merged, w_out, w_pg, x, pb, w_ple, ln_g, ln_b, ple_g)


def _row_tile(m, pref):
    return pref if m % pref == 0 else m


def _relayout_kernel(a_ref, b_ref, dt_ref, if_ref, o_ref, os_ref):
    jb = pl.program_id(2)
    cols = o_ref.shape[1]

    def shifted(shift):
        win = jnp.concatenate([a_ref[...], b_ref[...]], axis=1)
        return pltpu.roll(win, win.shape[1] - shift, axis=1)[:, :cols].astype(BF16)

    @pl.when(jb < P_OFF_QB // cols)
    def _():
        o_ref[...] = a_ref[...].astype(BF16)

    @pl.when((jb >= P_OFF_QB // cols) & (jb < W_OFF_GATES // cols))
    def _():
        o_ref[...] = shifted(OFF_QB - P_OFF_QB)

    @pl.when(jb >= W_OFF_GATES // cols)
    def _():
        o_ref[...] = shifted(OFF_GATES - W_OFF_GATES)

    @pl.when(jb == 0)
    def _():
        lane = lax.broadcasted_iota(jnp.int32, os_ref.shape, 1)
        small = jnp.where(lane < LANE_I, dt_ref[...], jnp.where(lane < LANE_F + MLSTM_HEADS, if_ref[...], 0.0))
        os_ref[...] = small.astype(BF16)


def _relayout_call(w_in, rows, cols):
    depth, d, _ = w_in.shape
    return pl.pallas_call(
        _relayout_kernel,
        grid=(depth, d // rows, W_OFF_SMALL // cols),
        in_specs=[pl.BlockSpec((None, rows, cols), lambda l, r, j: (l, r, j)),
                  pl.BlockSpec((None, rows, LANES), lambda l, r, j: (l, r, (j + 1) * (cols // LANES))),
                  pl.BlockSpec((None, rows, LANES), lambda l, r, j: (l, r, OFF_DT // LANES)),
                  pl.BlockSpec((None, rows, LANES), lambda l, r, j: (l, r, OFF_IC // LANES))],
        out_specs=[pl.BlockSpec((None, rows, cols), lambda l, r, j: (l, r, j)),
                   pl.BlockSpec((None, rows, LANES), lambda l, r, j: (l, r, 0))],
        out_shape=[jax.ShapeDtypeStruct((depth, d, W_OFF_SMALL), BF16),
                   jax.ShapeDtypeStruct((depth, d, LANES), BF16)],
        compiler_params=_cparams(("parallel", "parallel", "arbitrary")),
        name="weight_relayout",
    )(w_in, w_in, w_in, w_in)


def _prep_weights(w_in, w_branch, w_out, w_ple, w_ple_gate):
    w_main, w_small = _relayout_call(w_in, 1024, 1024)
    return (w_main, w_small, w_branch.astype(BF16), w_out.astype(BF16), w_ple.astype(BF16),
            w_ple_gate.astype(BF16))


def _layer(x, xb, p_i, li, weights, conv_w, conv_b, dt_bias, a_log, d_skip, ssm_norm_g,
           diff_lambda, diff_norm_g, mlstm_gate_b, ln_g, ln_b, ple_norm_g):
    w_all, w_small, w_br, w_out, w_ple, w_pg = weights
    bsz, s, _ = x.shape
    t = bsz * s
    x2 = x.reshape(t, D_MODEL)
    xb2 = xb.reshape(t, D_MODEL)

    proj, ps = _proj_call(xb2, w_all, w_small, li, _row_tile(t, 1024), 1024)
    proj = proj.reshape(bsz, s, PROJ_WIDTH)
    ps = ps.reshape(bsz, s, LANES)

    pad = lambda v, lane0: jnp.zeros((1, LANES), F32).at[0, lane0:lane0 + v.shape[0]].set(v.astype(F32))
    out_a = _ssd_call(proj, ps, conv_w, conv_b.reshape(1, -1), pad(dt_bias, LANE_DT), pad(a_log, LANE_DT),
                      jnp.repeat(d_skip.astype(F32), SSM_HEAD_DIM).reshape(1, -1), ssm_norm_g.reshape(1, -1))

    lam_init = 0.8 - 0.6 * math.exp(-0.3 * li)
    slopes = 2.0 ** (-8.0 * jnp.arange(1, DIFF_HEADS + 1, dtype=F32) / DIFF_HEADS)
    slopes = jnp.broadcast_to(slopes[:, None, None], (DIFF_HEADS, 1, LANES))
    out_b = _attn_call(proj, slopes, diff_lambda, diff_norm_g.reshape(1, -1), lam_init)

    gate_bias = pad(mlstm_gate_b[0], LANE_I) + pad(mlstm_gate_b[1], LANE_F)
    out_c = _mlstm_call(proj, ps, gate_bias)

    tm2 = _row_tile(t, 512)
    merged = _merge_call(xb2, out_a.reshape(t, -1), out_b.reshape(t, -1), out_c.reshape(t, -1),
                         w_all, w_br, li, tm2, 256)
    out, out_b16 = _final_call(merged, w_out, w_pg, x2, p_i.reshape(t, PLE_DIM).astype(BF16), w_ple,
                               ln_g.reshape(1, -1), ln_b.reshape(1, -1), ple_norm_g.reshape(1, -1), li, tm2, 512)
    return out.reshape(bsz, s, D_MODEL), out_b16.reshape(bsz, s, D_MODEL)


def kernel(x, p, w_in, conv_w, conv_b, dt_bias, a_log, d_skip, ssm_norm_g, diff_lambda, diff_norm_g,
           mlstm_gate_b, w_branch, w_out, ln_g, ln_b, w_ple, ple_norm_g, w_ple_gate):
    weights = _prep_weights(w_in, w_branch, w_out, w_ple, w_ple_gate)
    h = x
    hb = x.astype(BF16)
    for li in range(w_in.shape[0]):
        h, hb = _layer(h, hb, p[li], li, weights, conv_w[li], conv_b[li], dt_bias[li], a_log[li],
                       d_skip[li], ssm_norm_g[li], diff_lambda[li], diff_norm_g[li],
                       mlstm_gate_b[li], ln_g[li], ln_b[li], ple_norm_g[li])
    return h
```

```python
import functools
import math

import jax
import jax.numpy as jnp
from jax import lax
from jax.experimental import pallas as pl
from jax.experimental.pallas import tpu as pltpu

F32 = jnp.float32
BF16 = jnp.bfloat16

V7X_VMEM_LIMIT_BYTES = 56 * 1024 * 1024
LANES = 128
F32_SUBLANES = 8

D_MODEL = 4096
CHUNK = 64
SSM_WIDTH = 2048
SSM_HEAD_DIM = 64
SSM_HEADS = 32
SSM_GROUPS = 8
SSM_STATE = 128
SSM_XBC = SSM_WIDTH + 2 * SSM_GROUPS * SSM_STATE
GROUP_WIDTH = SSM_WIDTH // SSM_GROUPS
HEADS_PER_GROUP = SSM_HEADS // SSM_GROUPS
DIFF_WIDTH = 1024
DIFF_HEAD_DIM = 64
DIFF_HEADS = 8
MLSTM_WIDTH = 1024
MLSTM_HEAD_DIM = 128
MLSTM_HEADS = 8
N_BRANCH = 3
PLE_DIM = 256
DEPTH = 2
DEEPNORM_ALPHA = (2.0 * DEPTH) ** 0.25

OFF_XBC = 0
OFF_ZA = OFF_XBC + SSM_XBC
OFF_DT = OFF_ZA + SSM_WIDTH
OFF_QB = OFF_DT + SSM_HEADS
OFF_ZB = OFF_QB + 3 * DIFF_WIDTH
OFF_QC = OFF_ZB + DIFF_WIDTH
OFF_IC = OFF_QC + 5 * MLSTM_WIDTH
OFF_FC = OFF_IC + MLSTM_HEADS
OFF_GATES = OFF_FC + MLSTM_HEADS

P_OFF_XBC = 0
P_OFF_ZA = P_OFF_XBC + SSM_XBC
P_OFF_QB = P_OFF_ZA + SSM_WIDTH
P_OFF_KB = P_OFF_QB + DIFF_WIDTH
P_OFF_VB = P_OFF_KB + DIFF_WIDTH
P_OFF_ZB = P_OFF_VB + DIFF_WIDTH
P_OFF_C = P_OFF_ZB + DIFF_WIDTH
PROJ_WIDTH = P_OFF_C + 5 * MLSTM_WIDTH
W_OFF_GATES = PROJ_WIDTH
W_OFF_SMALL = W_OFF_GATES + N_BRANCH * D_MODEL

LANE_DT = 0
LANE_I = SSM_HEADS
LANE_F = SSM_HEADS + MLSTM_HEADS

SCAN_CHUNK = 256
ATT_BLOCK = 512
ATT_AUG = 2 * LANES
ATT_ACC_ROWS = LANES + 16
POS_SPLIT = 256
NORM_ROWS = 64
NEG_BIG = -1e30
LOG2E = 1.4426950408889634


def _cparams(sem):
    return pltpu.CompilerParams(dimension_semantics=sem, vmem_limit_bytes=V7X_VMEM_LIMIT_BYTES)


def _silu(x):
    return x * jax.nn.sigmoid(x)


def _softplus(x):
    return jnp.maximum(x, 0.0) + jnp.log1p(jnp.exp(-jnp.abs(x)))


def _dot(a, b):
    return jnp.dot(a, b, preferred_element_type=F32)


def _dot_nt(a, b):
    return lax.dot_general(a, b, (((1,), (1,)), ((), ())), preferred_element_type=F32)


def _dot_tn(a, b):
    return lax.dot_general(a, b, (((0,), (0,)), ((), ())), preferred_element_type=F32)


def _cumsum_rows(x, tril_bf16):
    hi = x.astype(BF16)
    r1 = x - hi.astype(F32)
    mid = r1.astype(BF16)
    lo = (r1 - mid.astype(F32)).astype(BF16)
    return _dot(tril_bf16, hi) + _dot(tril_bf16, mid) + _dot(tril_bf16, lo)


def _proj_kernel(x_ref, w_ref, ws_ref, o_ref, os_ref):
    o_ref[...] = _dot_nt(x_ref[...], w_ref[...])

    @pl.when(pl.program_id(1) == 0)
    def _():
        os_ref[...] = _dot_nt(x_ref[...], ws_ref[...])


def _proj_call(xb, w_main, w_small, li, tm, tn):
    m, k = xb.shape
    return pl.pallas_call(
        _proj_kernel,
        grid=(m // tm, PROJ_WIDTH // tn),
        in_specs=[pl.BlockSpec((tm, k), lambda i, j: (i, 0)),
                  pl.BlockSpec((None, tn, k), lambda i, j: (li, j, 0)),
                  pl.BlockSpec((None, LANES, k), lambda i, j: (li, 0, 0))],
        out_specs=[pl.BlockSpec((tm, tn), lambda i, j: (i, j)),
                   pl.BlockSpec((tm, LANES), lambda i, j: (i, 0))],
        out_shape=[jax.ShapeDtypeStruct((m, PROJ_WIDTH), F32), jax.ShapeDtypeStruct((m, LANES), F32)],
        compiler_params=_cparams(("parallel", "arbitrary")),
        name="in_proj",
    )(xb, w_main, w_small)


def _ssd_kernel(xbc_ref, z_ref, ps_ref, cw_ref, cb_ref, dtb_ref, alog_ref, dskip_ref, ng_ref,
                o_ref, xpad, state, ybuf):
    L = SCAN_CHUNK
    c = pl.program_id(1)

    @pl.when(c == 0)
    def _():
        xpad[0:8, :] = jnp.zeros((8, SSM_XBC), F32)
        state[...] = jnp.zeros(state.shape, F32)

    @pl.when(c > 0)
    def _():
        xpad[0:8, :] = xpad[L:L + 8, :]

    xpad[8:8 + L, :] = xbc_ref[0]

    def conv_silu(c0, width):
        acc = cb_ref[:, c0:c0 + width] + cw_ref[3:4, c0:c0 + width] * xpad[8:8 + L, c0:c0 + width]
        acc = acc + cw_ref[2:3, c0:c0 + width] * xpad[7:7 + L, c0:c0 + width]
        acc = acc + cw_ref[1:2, c0:c0 + width] * xpad[6:6 + L, c0:c0 + width]
        acc = acc + cw_ref[0:1, c0:c0 + width] * xpad[5:5 + L, c0:c0 + width]
        return _silu(acc)

    rows = lax.broadcasted_iota(jnp.int32, (L, L), 0)
    cols = lax.broadcasted_iota(jnp.int32, (L, L), 1)
    causal = rows >= cols
    tril = jnp.where(causal, 1.0, 0.0).astype(BF16)

    dt = _softplus(ps_ref[0] + dtb_ref[...])
    la = dt * (-jnp.exp(alog_ref[...]))
    lc = _cumsum_rows(la, tril)
    lc_t = lc.T

    head_of_lane = lax.broadcasted_iota(jnp.int32, (L, GROUP_WIDTH), 1) // SSM_HEAD_DIM

    def expand(vals, hol):
        out = vals[HEADS_PER_GROUP - 1]
        for r in range(HEADS_PER_GROUP - 2, -1, -1):
            out = jnp.where(hol == r, vals[r], out)
        return out

    ssq = jnp.zeros((L, 1), F32)
    for g in range(SSM_GROUPS):
        h0 = g * HEADS_PER_GROUP
        xg = conv_silu(g * GROUP_WIDTH, GROUP_WIDTH)
        bg = conv_silu(SSM_WIDTH + g * SSM_STATE, SSM_STATE).astype(BF16)
        cg = conv_silu(SSM_WIDTH + SSM_GROUPS * SSM_STATE + g * SSM_STATE, SSM_STATE).astype(BF16)

        dt4 = expand([dt[:, h0 + r:h0 + r + 1] for r in range(HEADS_PER_GROUP)], head_of_lane)
        lc4 = expand([lc[:, h0 + r:h0 + r + 1] for r in range(HEADS_PER_GROUP)], head_of_lane)
        lc_end4 = expand([lc[L - 1:L, h0 + r:h0 + r + 1] for r in range(HEADS_PER_GROUP)],
                         head_of_lane[0:1, :])
        xdt = xg * dt4

        cb = _dot_nt(cg, bg)
        y = jnp.zeros((L, GROUP_WIDTH), F32)
        for r in range(HEADS_PER_GROUP):
            h = h0 + r
            seg = lc[:, h:h + 1] - lc_t[h:h + 1, :]
            decay = jnp.where(causal, jnp.exp(seg), 0.0)
            m_r = (cb * decay).astype(BF16)
            x_r = jnp.where(head_of_lane == r, xdt, 0.0).astype(BF16)
            y = y + _dot(m_r, x_r)

        prev = state[g]
        y = y + _dot(cg, prev.astype(BF16)) * jnp.exp(lc4)
        xs = (xdt * jnp.exp(lc_end4 - lc4)).astype(BF16)
        state[g] = prev * jnp.exp(lc_end4) + _dot_tn(bg, xs)

        y = y + dskip_ref[:, g * GROUP_WIDTH:(g + 1) * GROUP_WIDTH] * xg
        yz = y * _silu(z_ref[0, :, g * GROUP_WIDTH:(g + 1) * GROUP_WIDTH])
        ybuf[:, g * GROUP_WIDTH:(g + 1) * GROUP_WIDTH] = yz
        ssq = ssq + jnp.sum(yz * yz, axis=-1, keepdims=True)

    rs = lax.rsqrt(ssq * (1.0 / SSM_WIDTH) + 1e-5)
    o_ref[0] = (ybuf[...] * rs * ng_ref[...]).astype(o_ref.dtype)


def _ssd_call(pa, ps, conv_w, conv_b, dtb, alog, dskip, norm_g):
    bsz, s, _ = pa.shape
    L = SCAN_CHUNK
    full = lambda b, c: (0, 0)
    return pl.pallas_call(
        _ssd_kernel,
        grid=(bsz, s // L),
        in_specs=[pl.BlockSpec((1, L, SSM_XBC), lambda b, c: (b, c, 0)),
                  pl.BlockSpec((1, L, SSM_WIDTH), lambda b, c: (b, c, SSM_XBC // SSM_WIDTH)),
                  pl.BlockSpec((1, L, LANES), lambda b, c: (b, c, 0)),
                  pl.BlockSpec(conv_w.shape, full),
                  pl.BlockSpec(conv_b.shape, full),
                  pl.BlockSpec(dtb.shape, full),
                  pl.BlockSpec(alog.shape, full),
                  pl.BlockSpec(dskip.shape, full),
                  pl.BlockSpec(norm_g.shape, full)],
        out_specs=pl.BlockSpec((1, L, SSM_WIDTH), lambda b, c: (b, c, 0)),
        out_shape=jax.ShapeDtypeStruct((bsz, s, SSM_WIDTH), BF16),
        scratch_shapes=[pltpu.VMEM((L + 8, SSM_XBC), F32),
                        pltpu.VMEM((SSM_GROUPS, SSM_STATE, GROUP_WIDTH), F32),
                        pltpu.VMEM((L, SSM_WIDTH), F32)],
        compiler_params=_cparams(("parallel", "arbitrary")),
        name="ssd_scan",
    )(pa, pa, ps, conv_w, conv_b, dtb, alog, dskip, norm_g)


def _attn_kernel(lam_init, q_ref, k_ref, v_ref, z_ref, slope_ref, corr_ref, dl_ref, ng_ref, o_ref,
                 k_aug, vt_aug, acc1, acc2):
    T = ATT_BLOCK
    D2 = 2 * DIFF_HEAD_DIM
    qi = pl.program_id(2)
    n_kv = k_ref.shape[1] // T

    @pl.when(qi == 0)
    def _():
        lane = lax.broadcasted_iota(jnp.int32, (T, LANES), 1)
        row = lax.broadcasted_iota(jnp.int32, (T, LANES), 0)
        for c in range(n_kv):
            rows = slice(c * T, (c + 1) * T)
            pos = row + c * T
            pos_lo = jnp.bitwise_and(pos, POS_SPLIT - 1)
            pos_hi = pos - pos_lo
            pos_cols = jnp.where(lane < 3, pos_lo, jnp.where(lane < 6, pos_hi, 0))
            k_aug[rows, 0:D2] = k_ref[0, rows, :].astype(BF16)
            k_aug[rows, D2:ATT_AUG] = pos_cols.astype(F32).astype(BF16)
            vt_aug[0:D2, rows] = v_ref[0, rows, :].T.astype(BF16)
        ones_row = lax.broadcasted_iota(jnp.int32, (ATT_ACC_ROWS - D2, k_ref.shape[1]), 0) == 0
        vt_aug[D2:ATT_ACC_ROWS, :] = jnp.where(ones_row, 1.0, 0.0).astype(BF16)

    lane = lax.broadcasted_iota(jnp.int32, (T, D2), 1)
    qs = q_ref[0] * (DIFF_HEAD_DIM ** -0.5 * LOG2E)
    sl2 = slope_ref[0][:, 0:1] * LOG2E
    sl_hi = sl2.astype(BF16).astype(F32)
    sl_mid = (sl2 - sl_hi).astype(BF16).astype(F32)
    sl_lo = sl2 - sl_hi - sl_mid
    lane1 = lax.broadcasted_iota(jnp.int32, (1, LANES), 1)
    ext = jnp.where((lane1 == 0) | (lane1 == 3), sl_hi,
                    jnp.where((lane1 == 1) | (lane1 == 4), sl_mid,
                              jnp.where((lane1 == 2) | (lane1 == 5), sl_lo, 0.0)))
    ext = jnp.broadcast_to(ext, (T, LANES)).astype(BF16)
    qa1 = jnp.concatenate([jnp.where(lane < DIFF_HEAD_DIM, qs, 0.0).astype(BF16), ext], axis=1)
    qa2 = jnp.concatenate([jnp.where(lane >= DIFF_HEAD_DIM, qs, 0.0).astype(BF16), ext], axis=1)

    acc1[...] = jnp.zeros(acc1.shape, F32)
    acc2[...] = jnp.zeros(acc2.shape, F32)

    def block(first, n, ms, corr):
        start = pl.multiple_of(first * T, T)
        kb = k_aug[pl.ds(start, n * T), :]
        vb = vt_aug[:, pl.ds(start, n * T)]

        def one(qa, m, acc):
            s = _dot_nt(kb, qa)
            if corr is not None and n == 1:
                s = s + corr
            elif corr is not None:
                s = jnp.concatenate([s[:(n - 1) * T], s[(n - 1) * T:] + corr], axis=0)
            m_new = jnp.maximum(m, jnp.max(s, axis=0, keepdims=True))
            p = jnp.exp2(s - m_new).astype(BF16)
            acc[...] = jnp.exp2(m - m_new) * acc[...] + _dot(vb, p)
            return m_new

        return one(qa1, ms[0], acc1), one(qa2, ms[1], acc2)

    m0 = jnp.full((1, T), NEG_BIG, F32)
    ms = lax.fori_loop(0, qi // 2, lambda i, ms: block(2 * i, 2, ms, None), (m0, m0))
    corr = corr_ref[...] * sl2

    @pl.when(qi % 2 == 1)
    def _():
        block(qi - 1, 2, ms, corr)

    @pl.when(qi % 2 == 0)
    def _():
        block(qi, 1, ms, corr)

    dl = dl_ref[...]
    lam = (jnp.exp(jnp.sum(dl[0:1] * dl[1:2], axis=-1, keepdims=True))
           - jnp.exp(jnp.sum(dl[2:3] * dl[3:4], axis=-1, keepdims=True)) + lam_init)
    o_t = acc1[0:D2, :] / acc1[D2:D2 + 1, :] - lam * (acc2[0:D2, :] / acc2[D2:D2 + 1, :])
    o = o_t.T
    y = o * lax.rsqrt(jnp.mean(o * o, axis=-1, keepdims=True) + 1e-5) * ng_ref[...]
    y = y * (1.0 - lam_init)
    o_ref[0] = (y * _silu(z_ref[0])).astype(o_ref.dtype)


def _attn_corr_table():
    kk = jnp.arange(ATT_BLOCK)[:, None]
    qq = jnp.arange(ATT_BLOCK)[None, :]
    visible = (kk // CHUNK) <= (qq // CHUNK)
    return jnp.where(visible, -2.0 * jnp.maximum(kk - qq, 0).astype(F32), NEG_BIG)


def _attn_call(proj, slopes, dl, norm_g, lam_init):
    bsz, s, _ = proj.shape
    T = ATT_BLOCK
    hw = 2 * DIFF_HEAD_DIM
    blk = lambda off: off // hw
    const2 = lambda b, h, i: (0, 0)
    return pl.pallas_call(
        functools.partial(_attn_kernel, lam_init),
        grid=(bsz, DIFF_HEADS, s // T),
        in_specs=[pl.BlockSpec((1, T, hw), lambda b, h, i: (b, i, blk(P_OFF_QB) + h)),
                  pl.BlockSpec((1, s, hw), lambda b, h, i: (b, 0, blk(P_OFF_KB) + h)),
                  pl.BlockSpec((1, s, hw), lambda b, h, i: (b, 0, blk(P_OFF_VB) + h)),
                  pl.BlockSpec((1, T, hw), lambda b, h, i: (b, i, blk(P_OFF_ZB) + h)),
                  pl.BlockSpec((1, 1, LANES), lambda b, h, i: (h, 0, 0)),
                  pl.BlockSpec((T, T), const2),
                  pl.BlockSpec(dl.shape, const2),
                  pl.BlockSpec(norm_g.shape, const2)],
        out_specs=pl.BlockSpec((1, T, hw), lambda b, h, i: (b, i, h)),
        out_shape=jax.ShapeDtypeStruct((bsz, s, DIFF_WIDTH), BF16),
        scratch_shapes=[pltpu.VMEM((s, ATT_AUG), BF16),
                        pltpu.VMEM((ATT_ACC_ROWS, s), BF16),
                        pltpu.VMEM((ATT_ACC_ROWS, T), F32),
                        pltpu.VMEM((ATT_ACC_ROWS, T), F32)],
        compiler_params=_cparams(("parallel", "parallel", "arbitrary")),
        name="diff_attention",
    )(proj, proj, proj, proj, slopes, _attn_corr_table(), dl, norm_g)


def _mlstm_kernel(q_ref, k_ref, v_ref, o_ref_in, z_ref, ps_ref, gb_ref, out_ref, c_state, m_state):
    L = SCAN_CHUNK
    D = MLSTM_HEAD_DIM
    c = pl.program_id(1)

    @pl.when(c == 0)
    def _():
        c_state[...] = jnp.zeros(c_state.shape, F32)
        m_state[...] = jnp.zeros(m_state.shape, F32)

    ones_col = jnp.where(lax.broadcasted_iota(jnp.int32, (L, D), 1) == 0, 1.0, 0.0)

    rows = lax.broadcasted_iota(jnp.int32, (L, L), 0)
    cols = lax.broadcasted_iota(jnp.int32, (L, L), 1)
    causal = rows >= cols
    tril = jnp.where(causal, 1.0, 0.0).astype(BF16)

    raw = ps_ref[0] + gb_ref[...]
    logf = -_softplus(-raw)
    bcum = _cumsum_rows(logf, tril)
    bcum_t = bcum.T
    raw_t = raw.T

    for h in range(MLSTM_HEADS):
        li, lf = LANE_I + h, LANE_F + h
        sl = slice(h * D, (h + 1) * D)
        q = q_ref[0, :, sl]
        ks = k_ref[0, :, sl] * (D ** -0.5)
        v_ext = jnp.concatenate([v_ref[0, :, sl], ones_col], axis=1)
        qb, kb = q.astype(BF16), ks.astype(BF16)

        b_col = bcum[:, lf:lf + 1]
        b_row = bcum_t[lf:lf + 1, :]
        ig_col = raw[:, li:li + 1]
        ig_row = raw_t[li:li + 1, :]
        btot = bcum[L - 1:L, lf:lf + 1]
        m_prev = m_state[h:h + 1, 0:1]
        c_prev = c_state[h]

        dmat = jnp.where(causal, b_col - b_row + ig_row, NEG_BIG)
        m_inter = b_col + m_prev
        m_row = jnp.maximum(m_inter, jnp.max(dmat, axis=-1, keepdims=True))
        wts = jnp.exp(dmat - m_row)
        sw = wts * _dot_nt(qb, kb)
        a_inter = jnp.exp(m_inter - m_row)
        nd = _dot(sw.astype(BF16), v_ext.astype(BF16)) + _dot(qb, c_prev.astype(BF16)) * a_inter
        hout = nd[:, :D] / jnp.maximum(jnp.abs(nd[:, D:D + 1]), jnp.exp(-m_row))
        gate = jax.nn.sigmoid(o_ref_in[0, :, sl]) * _silu(z_ref[0, :, sl])
        out_ref[0, :, sl] = (gate * hout).astype(out_ref.dtype)

        w_end = btot - b_col + ig_col
        m_loc = jnp.max(w_end, axis=0, keepdims=True)
        m_new = jnp.maximum(btot + m_prev, m_loc)
        a_old = jnp.exp(btot + m_prev - m_new)
        e_end = jnp.exp(w_end - m_new)
        c_state[h] = a_old * c_prev + _dot_tn(kb, (v_ext * e_end).astype(BF16))
        m_state[h:h + 1, :] = jnp.broadcast_to(m_new, (1, LANES))


def _mlstm_call(proj, ps, gate_bias):
    bsz, s, _ = proj.shape
    L = SCAN_CHUNK
    W = MLSTM_WIDTH
    blk0 = P_OFF_C // W
    return pl.pallas_call(
        _mlstm_kernel,
        grid=(bsz, s // L),
        in_specs=[pl.BlockSpec((1, L, W), lambda b, c: (b, c, blk0)),
                  pl.BlockSpec((1, L, W), lambda b, c: (b, c, blk0 + 1)),
                  pl.BlockSpec((1, L, W), lambda b, c: (b, c, blk0 + 2)),
                  pl.BlockSpec((1, L, W), lambda b, c: (b, c, blk0 + 3)),
                  pl.BlockSpec((1, L, W), lambda b, c: (b, c, blk0 + 4)),
                  pl.BlockSpec((1, L, LANES), lambda b, c: (b, c, 0)),
                  pl.BlockSpec(gate_bias.shape, lambda b, c: (0, 0))],
        out_specs=pl.BlockSpec((1, L, W), lambda b, c: (b, c, 0)),
        out_shape=jax.ShapeDtypeStruct((bsz, s, W), BF16),
        scratch_shapes=[pltpu.VMEM((MLSTM_HEADS, MLSTM_HEAD_DIM, 2 * MLSTM_HEAD_DIM), F32),
                        pltpu.VMEM((MLSTM_HEADS, LANES), F32)],
        compiler_params=_cparams(("parallel", "arbitrary")),
        name="mlstm_scan",
    )(proj, proj, proj, proj, proj, ps, gate_bias)


def _merge_kernel(x_ref, oa_ref, ob_ref, oc_ref, wga_ref, wgb_ref, wgc_ref, wa_ref, wb_ref, wc_ref, o_ref):
    x = x_ref[...]
    acc = jax.nn.sigmoid(_dot_nt(x, wga_ref[...])) * _dot(oa_ref[...], wa_ref[...])
    acc = acc + jax.nn.sigmoid(_dot_nt(x, wgb_ref[...])) * _dot(ob_ref[...], wb_ref[...])
    acc = acc + jax.nn.sigmoid(_dot_nt(x, wgc_ref[...])) * _dot(oc_ref[...], wc_ref[...])
    o_ref[...] = acc.astype(o_ref.dtype)


def _merge_call(xb, oa, ob, oc, w_all, w_br, li, tm, tn):
    m = xb.shape[0]
    row = lambda i, j: (i, 0)
    gate_spec = lambda br: pl.BlockSpec(
        (None, tn, D_MODEL), lambda i, j: (li, (W_OFF_GATES + br * D_MODEL) // tn + j, 0))
    return pl.pallas_call(
        _merge_kernel,
        grid=(m // tm, D_MODEL // tn),
        in_specs=[pl.BlockSpec((tm, D_MODEL), row),
                  pl.BlockSpec((tm, SSM_WIDTH), row, pipeline_mode=pl.Buffered(1)),
                  pl.BlockSpec((tm, DIFF_WIDTH), row, pipeline_mode=pl.Buffered(1)),
                  pl.BlockSpec((tm, MLSTM_WIDTH), row, pipeline_mode=pl.Buffered(1)),
                  gate_spec(0), gate_spec(1), gate_spec(2),
                  pl.BlockSpec((None, SSM_WIDTH, tn), lambda i, j: (li, 0, j)),
                  pl.BlockSpec((None, DIFF_WIDTH, tn), lambda i, j: (li, SSM_WIDTH // DIFF_WIDTH, j)),
                  pl.BlockSpec((None, MLSTM_WIDTH, tn),
                               lambda i, j: (li, (SSM_WIDTH + DIFF_WIDTH) // MLSTM_WIDTH, j))],
        out_specs=pl.BlockSpec((tm, tn), lambda i, j: (i, j)),
        out_shape=jax.ShapeDtypeStruct((m, D_MODEL), BF16),
        compiler_params=_cparams(("parallel", "arbitrary")),
        name="gated_merge",
    )(xb, oa, ob, oc, w_all, w_all, w_all, w_br, w_br, w_br)


def _final_kernel(nj, tn, mg_ref, wo_ref, wg_ref, x_ref, p_ref, wple_ref, lng_ref, lnb_ref, pg_ref,
                  o_ref, ob_ref, h_scr, hb_scr, st_scr):
    j = pl.program_id(1)
    tm = h_scr.shape[0]

    @pl.when(j == 0)
    def _():
        st_scr[...] = jnp.zeros(st_scr.shape, F32)

    @pl.when(j < nj)
    def _():
        off = pl.multiple_of(j * tn, tn)
        pre = DEEPNORM_ALPHA * x_ref[...] + _dot(mg_ref[...], wo_ref[...])
        h_scr[:, pl.ds(off, tn)] = pre
        st_scr[:, 0:LANES] += jnp.sum(pre, axis=-1, keepdims=True)

    @pl.when(j == nj)
    def _():
        def norm_rows(r, carry):
            rows = pl.ds(pl.multiple_of(r * NORM_ROWS, NORM_ROWS), NORM_ROWS)
            cen = h_scr[rows, :] - st_scr[rows, 0:1] * (1.0 / D_MODEL)
            var = jnp.mean(cen * cen, axis=-1, keepdims=True)
            h = cen * lax.rsqrt(var + 1e-5) * lng_ref[...] + lnb_ref[...]
            h_scr[rows, :] = h
            hb_scr[rows, :] = h.astype(BF16)
            e = _dot(p_ref[rows, :], wple_ref[...])
            st_scr[rows, LANES:2 * LANES] = jnp.broadcast_to(
                jnp.sum(e * e, axis=-1, keepdims=True), (NORM_ROWS, LANES))
            return carry

        lax.fori_loop(0, tm // NORM_ROWS, norm_rows, 0)

    @pl.when(j >= nj)
    def _():
        off = pl.multiple_of((j - nj) * tn, tn)
        gate = jax.nn.sigmoid(_dot(hb_scr[...], wg_ref[...]))
        rs = lax.rsqrt(st_scr[:, LANES:LANES + 1] * (1.0 / D_MODEL) + 1e-5)
        e = _dot(p_ref[...], wple_ref[:, pl.ds(off, tn)]) * rs * pg_ref[:, pl.ds(off, tn)]
        out = h_scr[:, pl.ds(off, tn)] + gate * e
        o_ref[...] = out
        ob_ref[...] = out.astype(BF16)


def _final_call(merged, w_out, w_pg, x, pb, w_ple, ln_g, ln_b, ple_g, li, tm, tn):
    m = merged.shape[0]
    nj = D_MODEL // tn
    full = lambda i, j: (0, 0)
    out_idx = lambda i, j: (i, jnp.maximum(j - nj, 0))
    return pl.pallas_call(
        functools.partial(_final_kernel, nj, tn),
        grid=(m // tm, 2 * nj),
        in_specs=[pl.BlockSpec((tm, D_MODEL), lambda i, j: (i, 0)),
                  pl.BlockSpec((None, D_MODEL, tn), lambda i, j: (li, 0, jnp.minimum(j, nj - 1))),
                  pl.BlockSpec((None, D_MODEL, tn), lambda i, j: (li, 0, jnp.maximum(j - nj, 0))),
                  pl.BlockSpec((tm, tn), lambda i, j: (i, jnp.minimum(j, nj - 1))),
                  pl.BlockSpec((tm, PLE_DIM), lambda i, j: (i, 0)),
                  pl.BlockSpec((None,) + w_ple.shape[1:], lambda i, j: (li, 0, 0)),
                  pl.BlockSpec(ln_g.shape, full),
                  pl.BlockSpec(ln_b.shape, full),
                  pl.BlockSpec(ple_g.shape, full)],
        out_specs=[pl.BlockSpec((tm, tn), out_idx), pl.BlockSpec((tm, tn), out_idx)],
        out_shape=[jax.ShapeDtypeStruct((m, D_MODEL), F32), jax.ShapeDtypeStruct((m, D_MODEL), BF16)],
        scratch_shapes=[pltpu.VMEM((tm, D_MODEL), F32),
                        pltpu.VMEM((tm, D_MODEL), BF16),
                        pltpu.VMEM((tm, 2 * LANES), F32)],
        compiler_params=_cparams(("parallel", "arbitrary")),
        name="outproj_deepnorm_ple",
    )(merged, w_out, w_pg, x, pb, w_ple, ln_g, ln_b, ple_g)


def _row_tile(m, pref):
    return pref if m % pref == 0 else m


def _relayout_kernel(a_ref, dt_ref, if_ref, o_ref, os_ref):
    o_ref[...] = a_ref[0].astype(BF16)

    @pl.when(pl.program_id(1) == 0)
    def _():
        os_ref[LANE_DT:LANE_I, :] = dt_ref[0].astype(BF16)
        os_ref[LANE_I:LANE_F + MLSTM_HEADS, :] = if_ref[0].astype(BF16)
        os_ref[LANE_F + MLSTM_HEADS:, :] = jnp.zeros((LANES - LANE_F - MLSTM_HEADS, os_ref.shape[1]), BF16)


def _relayout_call(w_t, rows):
    depth, _, d = w_t.shape

    def src_row(j):
        r = j * rows
        shift = jnp.where(r < P_OFF_QB, 0, jnp.where(r < W_OFF_GATES, (OFF_QB - P_OFF_QB) // F32_SUBLANES,
                                                     (OFF_GATES - W_OFF_GATES) // F32_SUBLANES))
        return (j * (rows // F32_SUBLANES) + shift) * F32_SUBLANES

    return pl.pallas_call(
        _relayout_kernel,
        grid=(depth, W_OFF_SMALL // rows),
        in_specs=[pl.BlockSpec((pl.Element(1), pl.Element(rows), pl.Element(d)), lambda l, j: (l, src_row(j), 0)),
                  pl.BlockSpec((pl.Element(1), pl.Element(SSM_HEADS), pl.Element(d)), lambda l, j: (l, OFF_DT, 0)),
                  pl.BlockSpec((pl.Element(1), pl.Element(2 * MLSTM_HEADS), pl.Element(d)),
                               lambda l, j: (l, OFF_IC, 0))],
        out_specs=[pl.BlockSpec((None, rows, d), lambda l, j: (l, j, 0)),
                   pl.BlockSpec((None, LANES, d), lambda l, j: (l, 0, 0))],
        out_shape=[jax.ShapeDtypeStruct((depth, W_OFF_SMALL, d), BF16),
                   jax.ShapeDtypeStruct((depth, LANES, d), BF16)],
        compiler_params=_cparams(("parallel", "arbitrary")),
        name="weight_relayout",
    )(w_t, w_t, w_t)


def _prep_weights(w_in, w_branch, w_out, w_ple, w_ple_gate):
    w_main, w_small = _relayout_call(jnp.swapaxes(w_in, 1, 2), 512)
    return (w_main, w_small, w_branch.astype(BF16), w_out.astype(BF16), w_ple.astype(BF16),
            w_ple_gate.astype(BF16))


def _layer(x, xb, p_i, li, weights, conv_w, conv_b, dt_bias, a_log, d_skip, ssm_norm_g,
           diff_lambda, diff_norm_g, mlstm_gate_b, ln_g, ln_b, ple_norm_g):
    w_all, w_small, w_br, w_out, w_ple, w_pg = weights
    bsz, s, _ = x.shape
    t = bsz * s
    x2 = x.reshape(t, D_MODEL)
    xb2 = xb.reshape(t, D_MODEL)

    proj, ps = _proj_call(xb2, w_all, w_small, li, _row_tile(t, 1024), 1024)
    proj = proj.reshape(bsz, s, PROJ_WIDTH)
    ps = ps.reshape(bsz, s, LANES)

    pad = lambda v, lane0: jnp.zeros((1, LANES), F32).at[0, lane0:lane0 + v.shape[0]].set(v.astype(F32))
    out_a = _ssd_call(proj, ps, conv_w, conv_b.reshape(1, -1), pad(dt_bias, LANE_DT), pad(a_log, LANE_DT),
                      jnp.repeat(d_skip.astype(F32), SSM_HEAD_DIM).reshape(1, -1), ssm_norm_g.reshape(1, -1))

    lam_init = 0.8 - 0.6 * math.exp(-0.3 * li)
    slopes = 2.0 ** (-8.0 * jnp.arange(1, DIFF_HEADS + 1, dtype=F32) / DIFF_HEADS)
    slopes = jnp.broadcast_to(slopes[:, None, None], (DIFF_HEADS, 1, LANES))
    out_b = _attn_call(proj, slopes, diff_lambda, diff_norm_g.reshape(1, -1), lam_init)

    gate_bias = pad(mlstm_gate_b[0], LANE_I) + pad(mlstm_gate_b[1], LANE_F)
    out_c = _mlstm_call(proj, ps, gate_bias)

    tm2 = _row_tile(t, 512)
    merged = _merge_call(xb2, out_a.reshape(t, -1), out_b.reshape(t, -1), out_c.reshape(t, -1),
                         w_all, w_br, li, _row_tile(t, 1024), 256)
    out, out_b16 = _final_call(merged, w_out, w_pg, x2, p_i.reshape(t, PLE_DIM).astype(BF16), w_ple,
                               ln_g.reshape(1, -1), ln_b.reshape(1, -1), ple_norm_g.reshape(1, -1), li, tm2, 512)
    return out.reshape(bsz, s, D_MODEL), out_b16.reshape(bsz, s, D_MODEL)


def kernel(x, p, w_in, conv_w, conv_b, dt_bias, a_log, d_skip, ssm_norm_g, diff_lambda, diff_norm_g,
           mlstm_gate_b, w_branch, w_out, ln_g, ln_b, w_ple, ple_norm_g, w_ple_gate):
    weights = _prep_weights(w_in, w_branch, w_out, w_ple, w_ple_gate)
    h = x
    hb = x.astype(BF16)
    for li in range(w_in.shape[0]):
        h, hb = _layer(h, hb, p[li], li, weights, conv_w[li], conv_b[li], dt_bias[li], a_log[li],
                       d_skip[li], ssm_norm_g[li], diff_lambda[li], diff_norm_g[li],
                       mlstm_gate_b[li], ln_g[li], ln_b[li], ple_norm_g[li])
    return h
```

```python
import functools
import math

import jax
import jax.numpy as jnp
from jax import lax
from jax.experimental import pallas as pl
from jax.experimental.pallas import tpu as pltpu

F32 = jnp.float32
BF16 = jnp.bfloat16

V7X_VMEM_LIMIT_BYTES = 56 * 1024 * 1024
LANES = 128
F32_SUBLANES = 8

D_MODEL = 4096
CHUNK = 64
SSM_WIDTH = 2048
SSM_HEAD_DIM = 64
SSM_HEADS = 32
SSM_GROUPS = 8
SSM_STATE = 128
SSM_XBC = SSM_WIDTH + 2 * SSM_GROUPS * SSM_STATE
GROUP_WIDTH = SSM_WIDTH // SSM_GROUPS
HEADS_PER_GROUP = SSM_HEADS // SSM_GROUPS
DIFF_WIDTH = 1024
DIFF_HEAD_DIM = 64
DIFF_HEADS = 8
MLSTM_WIDTH = 1024
MLSTM_HEAD_DIM = 128
MLSTM_HEADS = 8
N_BRANCH = 3
PLE_DIM = 256
DEPTH = 2
DEEPNORM_ALPHA = (2.0 * DEPTH) ** 0.25

OFF_XBC = 0
OFF_ZA = OFF_XBC + SSM_XBC
OFF_DT = OFF_ZA + SSM_WIDTH
OFF_QB = OFF_DT + SSM_HEADS
OFF_ZB = OFF_QB + 3 * DIFF_WIDTH
OFF_QC = OFF_ZB + DIFF_WIDTH
OFF_IC = OFF_QC + 5 * MLSTM_WIDTH
OFF_FC = OFF_IC + MLSTM_HEADS
OFF_GATES = OFF_FC + MLSTM_HEADS

P_OFF_XBC = 0
P_OFF_ZA = P_OFF_XBC + SSM_XBC
P_OFF_QB = P_OFF_ZA + SSM_WIDTH
P_OFF_KB = P_OFF_QB + DIFF_WIDTH
P_OFF_VB = P_OFF_KB + DIFF_WIDTH
P_OFF_ZB = P_OFF_VB + DIFF_WIDTH
P_OFF_C = P_OFF_ZB + DIFF_WIDTH
PROJ_WIDTH = P_OFF_C + 5 * MLSTM_WIDTH
W_OFF_GATES = PROJ_WIDTH
W_OFF_SMALL = W_OFF_GATES + N_BRANCH * D_MODEL

LANE_DT = 0
LANE_I = SSM_HEADS
LANE_F = SSM_HEADS + MLSTM_HEADS

SCAN_CHUNK = 256
ATT_BLOCK = 512
ATT_GROUP = 4
ATT_AUG = 2 * LANES
ATT_ACC_ROWS = LANES + 16
POS_SPLIT = 256
NORM_ROWS = 64
NEG_BIG = -1e30
LOG2E = 1.4426950408889634


def _cparams(sem):
    return pltpu.CompilerParams(dimension_semantics=sem, vmem_limit_bytes=V7X_VMEM_LIMIT_BYTES)


def _silu(x):
    return x * jax.nn.sigmoid(x)


def _softplus(x):
    return jnp.maximum(x, 0.0) + jnp.log1p(jnp.exp(-jnp.abs(x)))


def _dot(a, b):
    return jnp.dot(a, b, preferred_element_type=F32)


def _dot_nt(a, b):
    return lax.dot_general(a, b, (((1,), (1,)), ((), ())), preferred_element_type=F32)


def _dot_tn(a, b):
    return lax.dot_general(a, b, (((0,), (0,)), ((), ())), preferred_element_type=F32)


def _cumsum_rows(x, tril_bf16):
    hi = x.astype(BF16)
    r1 = x - hi.astype(F32)
    mid = r1.astype(BF16)
    lo = (r1 - mid.astype(F32)).astype(BF16)
    return _dot(tril_bf16, hi) + _dot(tril_bf16, mid) + _dot(tril_bf16, lo)


def _proj_kernel(x_ref, w_ref, ws_ref, o_ref, os_ref):
    o_ref[...] = _dot_nt(x_ref[...], w_ref[...])

    @pl.when(pl.program_id(1) == 0)
    def _():
        os_ref[...] = _dot_nt(x_ref[...], ws_ref[...])


def _proj_call(xb, w_main, w_small, li, tm, tn):
    m, k = xb.shape
    return pl.pallas_call(
        _proj_kernel,
        grid=(m // tm, PROJ_WIDTH // tn),
        in_specs=[pl.BlockSpec((tm, k), lambda i, j: (i, 0)),
                  pl.BlockSpec((None, tn, k), lambda i, j: (li, j, 0)),
                  pl.BlockSpec((None, LANES, k), lambda i, j: (li, 0, 0))],
        out_specs=[pl.BlockSpec((tm, tn), lambda i, j: (i, j)),
                   pl.BlockSpec((tm, LANES), lambda i, j: (i, 0))],
        out_shape=[jax.ShapeDtypeStruct((m, PROJ_WIDTH), F32), jax.ShapeDtypeStruct((m, LANES), F32)],
        compiler_params=_cparams(("parallel", "arbitrary")),
        name="in_proj",
    )(xb, w_main, w_small)


def _ssd_kernel(xbc_ref, z_ref, ps_ref, cw_ref, cb_ref, dtb_ref, alog_ref, dskip_ref, ng_ref,
                o_ref, xpad, state, ybuf):
    L = SCAN_CHUNK
    c = pl.program_id(1)

    @pl.when(c == 0)
    def _():
        xpad[0:8, :] = jnp.zeros((8, SSM_XBC), F32)
        state[...] = jnp.zeros(state.shape, F32)

    @pl.when(c > 0)
    def _():
        xpad[0:8, :] = xpad[L:L + 8, :]

    xpad[8:8 + L, :] = xbc_ref[0]

    def conv_silu(c0, width):
        acc = cb_ref[:, c0:c0 + width] + cw_ref[3:4, c0:c0 + width] * xpad[8:8 + L, c0:c0 + width]
        acc = acc + cw_ref[2:3, c0:c0 + width] * xpad[7:7 + L, c0:c0 + width]
        acc = acc + cw_ref[1:2, c0:c0 + width] * xpad[6:6 + L, c0:c0 + width]
        acc = acc + cw_ref[0:1, c0:c0 + width] * xpad[5:5 + L, c0:c0 + width]
        return _silu(acc)

    rows = lax.broadcasted_iota(jnp.int32, (L, L), 0)
    cols = lax.broadcasted_iota(jnp.int32, (L, L), 1)
    causal = rows >= cols
    tril = jnp.where(causal, 1.0, 0.0).astype(BF16)

    dt = _softplus(ps_ref[0] + dtb_ref[...])
    la = dt * (-jnp.exp(alog_ref[...]))
    lc = _cumsum_rows(la, tril)
    lc_t = lc.T

    head_of_lane = lax.broadcasted_iota(jnp.int32, (L, GROUP_WIDTH), 1) // SSM_HEAD_DIM

    def expand(vals, hol):
        out = vals[HEADS_PER_GROUP - 1]
        for r in range(HEADS_PER_GROUP - 2, -1, -1):
            out = jnp.where(hol == r, vals[r], out)
        return out

    ssq = jnp.zeros((L, 1), F32)
    for g in range(SSM_GROUPS):
        h0 = g * HEADS_PER_GROUP
        xg = conv_silu(g * GROUP_WIDTH, GROUP_WIDTH)
        bg = conv_silu(SSM_WIDTH + g * SSM_STATE, SSM_STATE).astype(BF16)
        cg = conv_silu(SSM_WIDTH + SSM_GROUPS * SSM_STATE + g * SSM_STATE, SSM_STATE).astype(BF16)

        dt4 = expand([dt[:, h0 + r:h0 + r + 1] for r in range(HEADS_PER_GROUP)], head_of_lane)
        lc4 = expand([lc[:, h0 + r:h0 + r + 1] for r in range(HEADS_PER_GROUP)], head_of_lane)
        lc_end4 = expand([lc[L - 1:L, h0 + r:h0 + r + 1] for r in range(HEADS_PER_GROUP)],
                         head_of_lane[0:1, :])
        xdt = xg * dt4

        cb = _dot_nt(cg, bg)
        y = jnp.zeros((L, GROUP_WIDTH), F32)
        for r in range(HEADS_PER_GROUP):
            h = h0 + r
            seg = lc[:, h:h + 1] - lc_t[h:h + 1, :]
            decay = jnp.where(causal, jnp.exp(seg), 0.0)
            m_r = (cb * decay).astype(BF16)
            x_r = jnp.where(head_of_lane == r, xdt, 0.0).astype(BF16)
            y = y + _dot(m_r, x_r)

        prev = state[g]
        y = y + _dot(cg, prev.astype(BF16)) * jnp.exp(lc4)
        xs = (xdt * jnp.exp(lc_end4 - lc4)).astype(BF16)
        state[g] = prev * jnp.exp(lc_end4) + _dot_tn(bg, xs)

        y = y + dskip_ref[:, g * GROUP_WIDTH:(g + 1) * GROUP_WIDTH] * xg
        yz = y * _silu(z_ref[0, :, g * GROUP_WIDTH:(g + 1) * GROUP_WIDTH])
        ybuf[:, g * GROUP_WIDTH:(g + 1) * GROUP_WIDTH] = yz
        ssq = ssq + jnp.sum(yz * yz, axis=-1, keepdims=True)

    rs = lax.rsqrt(ssq * (1.0 / SSM_WIDTH) + 1e-5)
    o_ref[0] = (ybuf[...] * rs * ng_ref[...]).astype(o_ref.dtype)


def _ssd_call(pa, ps, conv_w, conv_b, dtb, alog, dskip, norm_g):
    bsz, s, _ = pa.shape
    L = SCAN_CHUNK
    full = lambda b, c: (0, 0)
    return pl.pallas_call(
        _ssd_kernel,
        grid=(bsz, s // L),
        in_specs=[pl.BlockSpec((1, L, SSM_XBC), lambda b, c: (b, c, 0)),
                  pl.BlockSpec((1, L, SSM_WIDTH), lambda b, c: (b, c, SSM_XBC // SSM_WIDTH)),
                  pl.BlockSpec((1, L, LANES), lambda b, c: (b, c, 0)),
                  pl.BlockSpec(conv_w.shape, full),
                  pl.BlockSpec(conv_b.shape, full),
                  pl.BlockSpec(dtb.shape, full),
                  pl.BlockSpec(alog.shape, full),
                  pl.BlockSpec(dskip.shape, full),
                  pl.BlockSpec(norm_g.shape, full)],
        out_specs=pl.BlockSpec((1, L, SSM_WIDTH), lambda b, c: (b, c, 0)),
        out_shape=jax.ShapeDtypeStruct((bsz, s, SSM_WIDTH), BF16),
        scratch_shapes=[pltpu.VMEM((L + 8, SSM_XBC), F32),
                        pltpu.VMEM((SSM_GROUPS, SSM_STATE, GROUP_WIDTH), F32),
                        pltpu.VMEM((L, SSM_WIDTH), F32)],
        compiler_params=_cparams(("parallel", "arbitrary")),
        name="ssd_scan",
    )(pa, pa, ps, conv_w, conv_b, dtb, alog, dskip, norm_g)


def _attn_kernel(lam_init, q_ref, k_ref, v_ref, z_ref, slope_ref, corr_ref, dl_ref, ng_ref, o_ref,
                 k_aug, vt_aug, acc1, acc2):
    T = ATT_BLOCK
    D2 = 2 * DIFF_HEAD_DIM
    qi = pl.program_id(2)
    n_kv = k_ref.shape[1] // T

    @pl.when(qi == 0)
    def _():
        lane = lax.broadcasted_iota(jnp.int32, (T, LANES), 1)
        row = lax.broadcasted_iota(jnp.int32, (T, LANES), 0)
        for c in range(n_kv):
            rows = slice(c * T, (c + 1) * T)
            pos = row + c * T
            pos_lo = jnp.bitwise_and(pos, POS_SPLIT - 1)
            pos_hi = pos - pos_lo
            pos_cols = jnp.where(lane < 3, pos_lo, jnp.where(lane < 6, pos_hi, 0))
            k_aug[rows, 0:D2] = k_ref[0, rows, :].astype(BF16)
            k_aug[rows, D2:ATT_AUG] = pos_cols.astype(F32).astype(BF16)
            vt_aug[0:D2, rows] = v_ref[0, rows, :].T.astype(BF16)
        ones_row = lax.broadcasted_iota(jnp.int32, (ATT_ACC_ROWS - D2, k_ref.shape[1]), 0) == 0
        vt_aug[D2:ATT_ACC_ROWS, :] = jnp.where(ones_row, 1.0, 0.0).astype(BF16)

    lane = lax.broadcasted_iota(jnp.int32, (T, D2), 1)
    qs = q_ref[0] * (DIFF_HEAD_DIM ** -0.5 * LOG2E)
    sl2 = slope_ref[0][:, 0:1] * LOG2E
    sl_hi = sl2.astype(BF16).astype(F32)
    sl_mid = (sl2 - sl_hi).astype(BF16).astype(F32)
    sl_lo = sl2 - sl_hi - sl_mid
    lane1 = lax.broadcasted_iota(jnp.int32, (1, LANES), 1)
    ext = jnp.where((lane1 == 0) | (lane1 == 3), sl_hi,
                    jnp.where((lane1 == 1) | (lane1 == 4), sl_mid,
                              jnp.where((lane1 == 2) | (lane1 == 5), sl_lo, 0.0)))
    ext = jnp.broadcast_to(ext, (T, LANES)).astype(BF16)
    qa1 = jnp.concatenate([jnp.where(lane < DIFF_HEAD_DIM, qs, 0.0).astype(BF16), ext], axis=1)
    qa2 = jnp.concatenate([jnp.where(lane >= DIFF_HEAD_DIM, qs, 0.0).astype(BF16), ext], axis=1)

    acc1[...] = jnp.zeros(acc1.shape, F32)
    acc2[...] = jnp.zeros(acc2.shape, F32)

    def block(first, n, ms, corr):
        start = pl.multiple_of(first * T, T)
        kb = k_aug[pl.ds(start, n * T), :]
        vb = vt_aug[:, pl.ds(start, n * T)]

        def one(qa, m, acc):
            s = _dot_nt(kb, qa)
            if corr is not None and n == 1:
                s = s + corr
            elif corr is not None:
                s = jnp.concatenate([s[:(n - 1) * T], s[(n - 1) * T:] + corr], axis=0)
            m_new = jnp.maximum(m, jnp.max(s, axis=0, keepdims=True))
            p = jnp.exp2(s - m_new).astype(BF16)
            acc[...] = jnp.exp2(m - m_new) * acc[...] + _dot(vb, p)
            return m_new

        return one(qa1, ms[0], acc1), one(qa2, ms[1], acc2)

    m0 = jnp.full((1, T), NEG_BIG, F32)
    ms = lax.fori_loop(0, qi // ATT_GROUP, lambda i, ms: block(ATT_GROUP * i, ATT_GROUP, ms, None), (m0, m0))
    corr = corr_ref[...] * sl2
    for left in range(ATT_GROUP):
        @pl.when(qi % ATT_GROUP == left)
        def _():
            block(qi - left, left + 1, ms, corr)

    dl = dl_ref[...]
    lam = (jnp.exp(jnp.sum(dl[0:1] * dl[1:2], axis=-1, keepdims=True))
           - jnp.exp(jnp.sum(dl[2:3] * dl[3:4], axis=-1, keepdims=True)) + lam_init)
    o_t = acc1[0:D2, :] / acc1[D2:D2 + 1, :] - lam * (acc2[0:D2, :] / acc2[D2:D2 + 1, :])
    o = o_t.T
    y = o * lax.rsqrt(jnp.mean(o * o, axis=-1, keepdims=True) + 1e-5) * ng_ref[...]
    y = y * (1.0 - lam_init)
    o_ref[0] = (y * _silu(z_ref[0])).astype(o_ref.dtype)


def _attn_corr_table():
    kk = jnp.arange(ATT_BLOCK)[:, None]
    qq = jnp.arange(ATT_BLOCK)[None, :]
    visible = (kk // CHUNK) <= (qq // CHUNK)
    return jnp.where(visible, -2.0 * jnp.maximum(kk - qq, 0).astype(F32), NEG_BIG)


def _attn_call(proj, slopes, dl, norm_g, lam_init):
    bsz, s, _ = proj.shape
    T = ATT_BLOCK
    hw = 2 * DIFF_HEAD_DIM
    blk = lambda off: off // hw
    const2 = lambda b, h, i: (0, 0)
    return pl.pallas_call(
        functools.partial(_attn_kernel, lam_init),
        grid=(bsz, DIFF_HEADS, s // T),
        in_specs=[pl.BlockSpec((1, T, hw), lambda b, h, i: (b, i, blk(P_OFF_QB) + h)),
                  pl.BlockSpec((1, s, hw), lambda b, h, i: (b, 0, blk(P_OFF_KB) + h)),
                  pl.BlockSpec((1, s, hw), lambda b, h, i: (b, 0, blk(P_OFF_VB) + h)),
                  pl.BlockSpec((1, T, hw), lambda b, h, i: (b, i, blk(P_OFF_ZB) + h)),
                  pl.BlockSpec((1, 1, LANES), lambda b, h, i: (h, 0, 0)),
                  pl.BlockSpec((T, T), const2),
                  pl.BlockSpec(dl.shape, const2),
                  pl.BlockSpec(norm_g.shape, const2)],
        out_specs=pl.BlockSpec((1, T, hw), lambda b, h, i: (b, i, h)),
        out_shape=jax.ShapeDtypeStruct((bsz, s, DIFF_WIDTH), BF16),
        scratch_shapes=[pltpu.VMEM((s, ATT_AUG), BF16),
                        pltpu.VMEM((ATT_ACC_ROWS, s), BF16),
                        pltpu.VMEM((ATT_ACC_ROWS, T), F32),
                        pltpu.VMEM((ATT_ACC_ROWS, T), F32)],
        compiler_params=_cparams(("parallel", "parallel", "arbitrary")),
        name="diff_attention",
    )(proj, proj, proj, proj, slopes, _attn_corr_table(), dl, norm_g)


def _mlstm_kernel(q_ref, k_ref, v_ref, o_ref_in, z_ref, ps_ref, gb_ref, out_ref, c_state, m_state):
    L = SCAN_CHUNK
    D = MLSTM_HEAD_DIM
    c = pl.program_id(1)

    @pl.when(c == 0)
    def _():
        c_state[...] = jnp.zeros(c_state.shape, F32)
        m_state[...] = jnp.zeros(m_state.shape, F32)

    ones_col = jnp.where(lax.broadcasted_iota(jnp.int32, (L, D), 1) == 0, 1.0, 0.0)

    rows = lax.broadcasted_iota(jnp.int32, (L, L), 0)
    cols = lax.broadcasted_iota(jnp.int32, (L, L), 1)
    causal = rows >= cols
    tril = jnp.where(causal, 1.0, 0.0).astype(BF16)

    raw = ps_ref[0] + gb_ref[...]
    logf = -_softplus(-raw)
    bcum = _cumsum_rows(logf, tril)
    bcum_t = bcum.T
    raw_t = raw.T

    for h in range(MLSTM_HEADS):
        li, lf = LANE_I + h, LANE_F + h
        sl = slice(h * D, (h + 1) * D)
        q = q_ref[0, :, sl]
        ks = k_ref[0, :, sl] * (D ** -0.5)
        v_ext = jnp.concatenate([v_ref[0, :, sl], ones_col], axis=1)
        qb, kb = q.astype(BF16), ks.astype(BF16)

        b_col = bcum[:, lf:lf + 1]
        b_row = bcum_t[lf:lf + 1, :]
        ig_col = raw[:, li:li + 1]
        ig_row = raw_t[li:li + 1, :]
        btot = bcum[L - 1:L, lf:lf + 1]
        m_prev = m_state[h:h + 1, 0:1]
        c_prev = c_state[h]

        dmat = jnp.where(causal, b_col - b_row + ig_row, NEG_BIG)
        m_inter = b_col + m_prev
        m_row = jnp.maximum(m_inter, jnp.max(dmat, axis=-1, keepdims=True))
        wts = jnp.exp(dmat - m_row)
        sw = wts * _dot_nt(qb, kb)
        a_inter = jnp.exp(m_inter - m_row)
        nd = _dot(sw.astype(BF16), v_ext.astype(BF16)) + _dot(qb, c_prev.astype(BF16)) * a_inter
        hout = nd[:, :D] / jnp.maximum(jnp.abs(nd[:, D:D + 1]), jnp.exp(-m_row))
        gate = jax.nn.sigmoid(o_ref_in[0, :, sl]) * _silu(z_ref[0, :, sl])
        out_ref[0, :, sl] = (gate * hout).astype(out_ref.dtype)

        w_end = btot - b_col + ig_col
        m_loc = jnp.max(w_end, axis=0, keepdims=True)
        m_new = jnp.maximum(btot + m_prev, m_loc)
        a_old = jnp.exp(btot + m_prev - m_new)
        e_end = jnp.exp(w_end - m_new)
        c_state[h] = a_old * c_prev + _dot_tn(kb, (v_ext * e_end).astype(BF16))
        m_state[h:h + 1, :] = jnp.broadcast_to(m_new, (1, LANES))


def _mlstm_call(proj, ps, gate_bias):
    bsz, s, _ = proj.shape
    L = SCAN_CHUNK
    W = MLSTM_WIDTH
    blk0 = P_OFF_C // W
    return pl.pallas_call(
        _mlstm_kernel,
        grid=(bsz, s // L),
        in_specs=[pl.BlockSpec((1, L, W), lambda b, c: (b, c, blk0)),
                  pl.BlockSpec((1, L, W), lambda b, c: (b, c, blk0 + 1)),
                  pl.BlockSpec((1, L, W), lambda b, c: (b, c, blk0 + 2)),
                  pl.BlockSpec((1, L, W), lambda b, c: (b, c, blk0 + 3)),
                  pl.BlockSpec((1, L, W), lambda b, c: (b, c, blk0 + 4)),
                  pl.BlockSpec((1, L, LANES), lambda b, c: (b, c, 0)),
                  pl.BlockSpec(gate_bias.shape, lambda b, c: (0, 0))],
        out_specs=pl.BlockSpec((1, L, W), lambda b, c: (b, c, 0)),
        out_shape=jax.ShapeDtypeStruct((bsz, s, W), BF16),
        scratch_shapes=[pltpu.VMEM((MLSTM_HEADS, MLSTM_HEAD_DIM, 2 * MLSTM_HEAD_DIM), F32),
                        pltpu.VMEM((MLSTM_HEADS, LANES), F32)],
        compiler_params=_cparams(("parallel", "arbitrary")),
        name="mlstm_scan",
    )(proj, proj, proj, proj, proj, ps, gate_bias)


def _merge_kernel(x_ref, oa_ref, ob_ref, oc_ref, wga_ref, wgb_ref, wgc_ref, wa_ref, wb_ref, wc_ref, o_ref):
    x = x_ref[...]
    acc = jax.nn.sigmoid(_dot_nt(x, wga_ref[...])) * _dot_nt(oa_ref[...], wa_ref[...])
    acc = acc + jax.nn.sigmoid(_dot_nt(x, wgb_ref[...])) * _dot_nt(ob_ref[...], wb_ref[...])
    acc = acc + jax.nn.sigmoid(_dot_nt(x, wgc_ref[...])) * _dot_nt(oc_ref[...], wc_ref[...])
    o_ref[...] = acc.astype(o_ref.dtype)


def _merge_call(xb, oa, ob, oc, w_all, w_br, li, tm, tn):
    m = xb.shape[0]
    row = lambda i, j: (i, 0)
    gate_spec = lambda br: pl.BlockSpec(
        (None, tn, D_MODEL), lambda i, j: (li, (W_OFF_GATES + br * D_MODEL) // tn + j, 0))
    return pl.pallas_call(
        _merge_kernel,
        grid=(m // tm, D_MODEL // tn),
        in_specs=[pl.BlockSpec((tm, D_MODEL), row),
                  pl.BlockSpec((tm, SSM_WIDTH), row, pipeline_mode=pl.Buffered(1)),
                  pl.BlockSpec((tm, DIFF_WIDTH), row, pipeline_mode=pl.Buffered(1)),
                  pl.BlockSpec((tm, MLSTM_WIDTH), row, pipeline_mode=pl.Buffered(1)),
                  gate_spec(0), gate_spec(1), gate_spec(2),
                  pl.BlockSpec((None, tn, SSM_WIDTH), lambda i, j: (li, j, 0)),
                  pl.BlockSpec((None, tn, DIFF_WIDTH), lambda i, j: (li, j, SSM_WIDTH // DIFF_WIDTH)),
                  pl.BlockSpec((None, tn, MLSTM_WIDTH),
                               lambda i, j: (li, j, (SSM_WIDTH + DIFF_WIDTH) // MLSTM_WIDTH))],
        out_specs=pl.BlockSpec((tm, tn), lambda i, j: (i, j)),
        out_shape=jax.ShapeDtypeStruct((m, D_MODEL), BF16),
        compiler_params=_cparams(("parallel", "arbitrary")),
        name="gated_merge",
    )(xb, oa, ob, oc, w_all, w_all, w_all, w_br, w_br, w_br)


def _final_kernel(nj, tn, mg_ref, wo_ref, wg_ref, x_ref, p_ref, wple_ref, lng_ref, lnb_ref, pg_ref,
                  o_ref, ob_ref, h_scr, hb_scr, st_scr):
    j = pl.program_id(1)
    tm = h_scr.shape[0]

    @pl.when(j == 0)
    def _():
        st_scr[...] = jnp.zeros(st_scr.shape, F32)

    @pl.when(j < nj)
    def _():
        off = pl.multiple_of(j * tn, tn)
        pre = DEEPNORM_ALPHA * x_ref[...] + _dot_nt(mg_ref[...], wo_ref[...])
        h_scr[:, pl.ds(off, tn)] = pre
        st_scr[:, 0:LANES] += jnp.sum(pre, axis=-1, keepdims=True)

    @pl.when(j == nj)
    def _():
        def norm_rows(r, carry):
            rows = pl.ds(pl.multiple_of(r * NORM_ROWS, NORM_ROWS), NORM_ROWS)
            cen = h_scr[rows, :] - st_scr[rows, 0:1] * (1.0 / D_MODEL)
            var = jnp.mean(cen * cen, axis=-1, keepdims=True)
            h = cen * lax.rsqrt(var + 1e-5) * lng_ref[...] + lnb_ref[...]
            h_scr[rows, :] = h
            hb_scr[rows, :] = h.astype(BF16)
            e = _dot(p_ref[rows, :], wple_ref[...])
            st_scr[rows, LANES:2 * LANES] = jnp.broadcast_to(
                jnp.sum(e * e, axis=-1, keepdims=True), (NORM_ROWS, LANES))
            return carry

        lax.fori_loop(0, tm // NORM_ROWS, norm_rows, 0)

    @pl.when(j >= nj)
    def _():
        off = pl.multiple_of((j - nj) * tn, tn)
        gate = jax.nn.sigmoid(_dot_nt(hb_scr[...], wg_ref[...]))
        rs = lax.rsqrt(st_scr[:, LANES:LANES + 1] * (1.0 / D_MODEL) + 1e-5)
        e = _dot(p_ref[...], wple_ref[:, pl.ds(off, tn)]) * rs * pg_ref[:, pl.ds(off, tn)]
        out = h_scr[:, pl.ds(off, tn)] + gate * e
        o_ref[...] = out
        ob_ref[...] = out.astype(BF16)


def _final_call(merged, w_out, w_pg, x, pb, w_ple, ln_g, ln_b, ple_g, li, tm, tn):
    m = merged.shape[0]
    nj = D_MODEL // tn
    full = lambda i, j: (0, 0)
    out_idx = lambda i, j: (i, jnp.maximum(j - nj, 0))
    return pl.pallas_call(
        functools.partial(_final_kernel, nj, tn),
        grid=(m // tm, 2 * nj),
        in_specs=[pl.BlockSpec((tm, D_MODEL), lambda i, j: (i, 0)),
                  pl.BlockSpec((None, tn, D_MODEL), lambda i, j: (li, jnp.minimum(j, nj - 1), 0)),
                  pl.BlockSpec((None, tn, D_MODEL), lambda i, j: (li, jnp.maximum(j - nj, 0), 0)),
                  pl.BlockSpec((tm, tn), lambda i, j: (i, jnp.minimum(j, nj - 1))),
                  pl.BlockSpec((tm, PLE_DIM), lambda i, j: (i, 0)),
                  pl.BlockSpec((None,) + w_ple.shape[1:], lambda i, j: (li, 0, 0)),
                  pl.BlockSpec(ln_g.shape, full),
                  pl.BlockSpec(ln_b.shape, full),
                  pl.BlockSpec(ple_g.shape, full)],
        out_specs=[pl.BlockSpec((tm, tn), out_idx), pl.BlockSpec((tm, tn), out_idx)],
        out_shape=[jax.ShapeDtypeStruct((m, D_MODEL), F32), jax.ShapeDtypeStruct((m, D_MODEL), BF16)],
        scratch_shapes=[pltpu.VMEM((tm, D_MODEL), F32),
                        pltpu.VMEM((tm, D_MODEL), BF16),
                        pltpu.VMEM((tm, 2 * LANES), F32)],
        compiler_params=_cparams(("parallel", "arbitrary")),
        name="outproj_deepnorm_ple",
    )(merged, w_out, w_pg, x, pb, w_ple, ln_g, ln_b, ple_g)


def _row_tile(m, pref):
    return pref if m % pref == 0 else m


def _relayout_kernel(a_ref, dt_ref, if_ref, o_ref, os_ref):
    o_ref[...] = a_ref[0].astype(BF16)

    @pl.when(pl.program_id(1) == 0)
    def _():
        os_ref[LANE_DT:LANE_I, :] = dt_ref[0].astype(BF16)
        os_ref[LANE_I:LANE_F + MLSTM_HEADS, :] = if_ref[0].astype(BF16)
        os_ref[LANE_F + MLSTM_HEADS:, :] = jnp.zeros((LANES - LANE_F - MLSTM_HEADS, os_ref.shape[1]), BF16)


def _relayout_call(w_t, rows):
    depth, _, d = w_t.shape

    def src_row(j):
        r = j * rows
        shift = jnp.where(r < P_OFF_QB, 0, jnp.where(r < W_OFF_GATES, (OFF_QB - P_OFF_QB) // F32_SUBLANES,
                                                     (OFF_GATES - W_OFF_GATES) // F32_SUBLANES))
        return (j * (rows // F32_SUBLANES) + shift) * F32_SUBLANES

    return pl.pallas_call(
        _relayout_kernel,
        grid=(depth, W_OFF_SMALL // rows),
        in_specs=[pl.BlockSpec((pl.Element(1), pl.Element(rows), pl.Element(d)), lambda l, j: (l, src_row(j), 0)),
                  pl.BlockSpec((pl.Element(1), pl.Element(SSM_HEADS), pl.Element(d)), lambda l, j: (l, OFF_DT, 0)),
                  pl.BlockSpec((pl.Element(1), pl.Element(2 * MLSTM_HEADS), pl.Element(d)),
                               lambda l, j: (l, OFF_IC, 0))],
        out_specs=[pl.BlockSpec((None, rows, d), lambda l, j: (l, j, 0)),
                   pl.BlockSpec((None, LANES, d), lambda l, j: (l, 0, 0))],
        out_shape=[jax.ShapeDtypeStruct((depth, W_OFF_SMALL, d), BF16),
                   jax.ShapeDtypeStruct((depth, LANES, d), BF16)],
        compiler_params=_cparams(("parallel", "arbitrary")),
        name="weight_relayout",
    )(w_t, w_t, w_t)


def _prep_weights(w_in, w_branch, w_out, w_ple, w_ple_gate):
    w_main, w_small = _relayout_call(jnp.swapaxes(w_in, 1, 2), 512)
    as_rows = lambda w: jnp.swapaxes(w, 1, 2).astype(BF16)
    return w_main, w_small, as_rows(w_branch), as_rows(w_out), w_ple.astype(BF16), as_rows(w_ple_gate)


def _layer(x, xb, p_i, li, weights, conv_w, conv_b, dt_bias, a_log, d_skip, ssm_norm_g,
           diff_lambda, diff_norm_g, mlstm_gate_b, ln_g, ln_b, ple_norm_g):
    w_all, w_small, w_br, w_out, w_ple, w_pg = weights
    bsz, s, _ = x.shape
    t = bsz * s
    x2 = x.reshape(t, D_MODEL)
    xb2 = xb.reshape(t, D_MODEL)

    proj, ps = _proj_call(xb2, w_all, w_small, li, _row_tile(t, 1024), 1024)
    proj = proj.reshape(bsz, s, PROJ_WIDTH)
    ps = ps.reshape(bsz, s, LANES)

    pad = lambda v, lane0: jnp.zeros((1, LANES), F32).at[0, lane0:lane0 + v.shape[0]].set(v.astype(F32))
    out_a = _ssd_call(proj, ps, conv_w, conv_b.reshape(1, -1), pad(dt_bias, LANE_DT), pad(a_log, LANE_DT),
                      jnp.repeat(d_skip.astype(F32), SSM_HEAD_DIM).reshape(1, -1), ssm_norm_g.reshape(1, -1))

    lam_init = 0.8 - 0.6 * math.exp(-0.3 * li)
    slopes = 2.0 ** (-8.0 * jnp.arange(1, DIFF_HEADS + 1, dtype=F32) / DIFF_HEADS)
    slopes = jnp.broadcast_to(slopes[:, None, None], (DIFF_HEADS, 1, LANES))
    out_b = _attn_call(proj, slopes, diff_lambda, diff_norm_g.reshape(1, -1), lam_init)

    gate_bias = pad(mlstm_gate_b[0], LANE_I) + pad(mlstm_gate_b[1], LANE_F)
    out_c = _mlstm_call(proj, ps, gate_bias)

    tm2 = _row_tile(t, 512)
    merged = _merge_call(xb2, out_a.reshape(t, -1), out_b.reshape(t, -1), out_c.reshape(t, -1),
                         w_all, w_br, li, _row_tile(t, 1024), 256)
    out, out_b16 = _final_call(merged, w_out, w_pg, x2, p_i.reshape(t, PLE_DIM).astype(BF16), w_ple,
                               ln_g.reshape(1, -1), ln_b.reshape(1, -1), ple_norm_g.reshape(1, -1), li, tm2, 512)
    return out.reshape(bsz, s, D_MODEL), out_b16.reshape(bsz, s, D_MODEL)


def kernel(x, p, w_in, conv_w, conv_b, dt_bias, a_log, d_skip, ssm_norm_g, diff_lambda, diff_norm_g,
           mlstm_gate_b, w_branch, w_out, ln_g, ln_b, w_ple, ple_norm_g, w_ple_gate):
    weights = _prep_weights(w_in, w_branch, w_out, w_ple, w_ple_gate)
    h = x
    hb = x.astype(BF16)
    for li in range(w_in.shape[0]):
        h, hb = _layer(h, hb, p[li], li, weights, conv_w[li], conv_b[li], dt_bias[li], a_log[li],
                       d_skip[li], ssm_norm_g[li], diff_lambda[li], diff_norm_g[li],
                       mlstm_gate_b[li], ln_g[li], ln_b[li], ple_norm_g[li])
    return h
```

```python
import functools
import math

import jax
import jax.numpy as jnp
from jax import lax
from jax.experimental import pallas as pl
from jax.experimental.pallas import tpu as pltpu

F32 = jnp.float32
BF16 = jnp.bfloat16

V7X_VMEM_LIMIT_BYTES = 56 * 1024 * 1024
LANES = 128
F32_SUBLANES = 8

D_MODEL = 4096
CHUNK = 64
SSM_WIDTH = 2048
SSM_HEAD_DIM = 64
SSM_HEADS = 32
SSM_GROUPS = 8
SSM_STATE = 128
SSM_XBC = SSM_WIDTH + 2 * SSM_GROUPS * SSM_STATE
GROUP_WIDTH = SSM_WIDTH // SSM_GROUPS
HEADS_PER_GROUP = SSM_HEADS // SSM_GROUPS
DIFF_WIDTH = 1024
DIFF_HEAD_DIM = 64
DIFF_HEADS = 8
MLSTM_WIDTH = 1024
MLSTM_HEAD_DIM = 128
MLSTM_HEADS = 8
N_BRANCH = 3
PLE_DIM = 256
DEPTH = 2
DEEPNORM_ALPHA = (2.0 * DEPTH) ** 0.25

OFF_XBC = 0
OFF_ZA = OFF_XBC + SSM_XBC
OFF_DT = OFF_ZA + SSM_WIDTH
OFF_QB = OFF_DT + SSM_HEADS
OFF_ZB = OFF_QB + 3 * DIFF_WIDTH
OFF_QC = OFF_ZB + DIFF_WIDTH
OFF_IC = OFF_QC + 5 * MLSTM_WIDTH
OFF_FC = OFF_IC + MLSTM_HEADS
OFF_GATES = OFF_FC + MLSTM_HEADS

P_OFF_XBC = 0
P_OFF_ZA = P_OFF_XBC + SSM_XBC
P_OFF_QB = P_OFF_ZA + SSM_WIDTH
P_OFF_KB = P_OFF_QB + DIFF_WIDTH
P_OFF_VB = P_OFF_KB + DIFF_WIDTH
P_OFF_ZB = P_OFF_VB + DIFF_WIDTH
P_OFF_C = P_OFF_ZB + DIFF_WIDTH
PROJ_WIDTH = P_OFF_C + 5 * MLSTM_WIDTH
W_OFF_GATES = PROJ_WIDTH
W_OFF_SMALL = W_OFF_GATES + N_BRANCH * D_MODEL

LANE_DT = 0
LANE_I = SSM_HEADS
LANE_F = SSM_HEADS + MLSTM_HEADS

SCAN_CHUNK = 256
ATT_BLOCK = 512
ATT_GROUP = 4
ATT_AUG = 2 * LANES
ATT_ACC_ROWS = LANES + 16
POS_SPLIT = 256
NORM_ROWS = 64
NEG_BIG = -1e30
LOG2E = 1.4426950408889634


def _cparams(sem):
    return pltpu.CompilerParams(dimension_semantics=sem, vmem_limit_bytes=V7X_VMEM_LIMIT_BYTES)


def _silu(x):
    return x * jax.nn.sigmoid(x)


def _softplus(x):
    return jnp.maximum(x, 0.0) + jnp.log1p(jnp.exp(-jnp.abs(x)))


def _dot(a, b):
    return jnp.dot(a, b, preferred_element_type=F32)


def _dot_nt(a, b):
    return lax.dot_general(a, b, (((1,), (1,)), ((), ())), preferred_element_type=F32)


def _dot_tn(a, b):
    return lax.dot_general(a, b, (((0,), (0,)), ((), ())), preferred_element_type=F32)


def _cumsum_rows(x, tril_bf16):
    hi = x.astype(BF16)
    r1 = x - hi.astype(F32)
    mid = r1.astype(BF16)
    lo = (r1 - mid.astype(F32)).astype(BF16)
    return _dot(tril_bf16, hi) + _dot(tril_bf16, mid) + _dot(tril_bf16, lo)


def _proj_kernel(x_ref, w_ref, ws_ref, o_ref, os_ref):
    o_ref[...] = _dot_nt(x_ref[...], w_ref[...])

    @pl.when(pl.program_id(1) == 0)
    def _():
        os_ref[...] = _dot_nt(x_ref[...], ws_ref[...])


def _proj_call(xb, w_main, w_small, li, tm, tn):
    m, k = xb.shape
    return pl.pallas_call(
        _proj_kernel,
        grid=(m // tm, PROJ_WIDTH // tn),
        in_specs=[pl.BlockSpec((tm, k), lambda i, j: (i, 0)),
                  pl.BlockSpec((None, tn, k), lambda i, j: (li, j, 0)),
                  pl.BlockSpec((None, LANES, k), lambda i, j: (li, 0, 0))],
        out_specs=[pl.BlockSpec((tm, tn), lambda i, j: (i, j)),
                   pl.BlockSpec((tm, LANES), lambda i, j: (i, 0))],
        out_shape=[jax.ShapeDtypeStruct((m, PROJ_WIDTH), F32), jax.ShapeDtypeStruct((m, LANES), F32)],
        compiler_params=_cparams(("parallel", "arbitrary")),
        name="in_proj",
    )(xb, w_main, w_small)


def _ssd_kernel(xbc_ref, z_ref, ps_ref, cw_ref, cb_ref, dtb_ref, alog_ref, dskip_ref, ng_ref,
                o_ref, xpad, state, ybuf):
    L = SCAN_CHUNK
    c = pl.program_id(1)

    @pl.when(c == 0)
    def _():
        xpad[0:8, :] = jnp.zeros((8, SSM_XBC), F32)
        state[...] = jnp.zeros(state.shape, F32)

    @pl.when(c > 0)
    def _():
        xpad[0:8, :] = xpad[L:L + 8, :]

    xpad[8:8 + L, :] = xbc_ref[0]

    def conv_silu(c0, width):
        acc = cb_ref[:, c0:c0 + width] + cw_ref[3:4, c0:c0 + width] * xpad[8:8 + L, c0:c0 + width]
        acc = acc + cw_ref[2:3, c0:c0 + width] * xpad[7:7 + L, c0:c0 + width]
        acc = acc + cw_ref[1:2, c0:c0 + width] * xpad[6:6 + L, c0:c0 + width]
        acc = acc + cw_ref[0:1, c0:c0 + width] * xpad[5:5 + L, c0:c0 + width]
        return _silu(acc)

    rows = lax.broadcasted_iota(jnp.int32, (L, L), 0)
    cols = lax.broadcasted_iota(jnp.int32, (L, L), 1)
    causal = rows >= cols
    tril = jnp.where(causal, 1.0, 0.0).astype(BF16)

    dt = _softplus(ps_ref[0] + dtb_ref[...])
    la = dt * (-jnp.exp(alog_ref[...]))
    lc = _cumsum_rows(la, tril)
    lc_t = lc.T

    head_of_lane = lax.broadcasted_iota(jnp.int32, (L, GROUP_WIDTH), 1) // SSM_HEAD_DIM

    def expand(vals, hol):
        out = vals[HEADS_PER_GROUP - 1]
        for r in range(HEADS_PER_GROUP - 2, -1, -1):
            out = jnp.where(hol == r, vals[r], out)
        return out

    ssq = jnp.zeros((L, 1), F32)
    for g in range(SSM_GROUPS):
        h0 = g * HEADS_PER_GROUP
        xg = conv_silu(g * GROUP_WIDTH, GROUP_WIDTH)
        bg = conv_silu(SSM_WIDTH + g * SSM_STATE, SSM_STATE).astype(BF16)
        cg = conv_silu(SSM_WIDTH + SSM_GROUPS * SSM_STATE + g * SSM_STATE, SSM_STATE).astype(BF16)

        dt4 = expand([dt[:, h0 + r:h0 + r + 1] for r in range(HEADS_PER_GROUP)], head_of_lane)
        lc4 = expand([lc[:, h0 + r:h0 + r + 1] for r in range(HEADS_PER_GROUP)], head_of_lane)
        lc_end4 = expand([lc[L - 1:L, h0 + r:h0 + r + 1] for r in range(HEADS_PER_GROUP)],
                         head_of_lane[0:1, :])
        xdt = xg * dt4

        cb = _dot_nt(cg, bg)
        y = jnp.zeros((L, GROUP_WIDTH), F32)
        for r in range(HEADS_PER_GROUP):
            h = h0 + r
            seg = lc[:, h:h + 1] - lc_t[h:h + 1, :]
            decay = jnp.where(causal, jnp.exp(seg), 0.0)
            m_r = (cb * decay).astype(BF16)
            x_r = jnp.where(head_of_lane == r, xdt, 0.0).astype(BF16)
            y = y + _dot(m_r, x_r)

        prev = state[g]
        y = y + _dot(cg, prev.astype(BF16)) * jnp.exp(lc4)
        xs = (xdt * jnp.exp(lc_end4 - lc4)).astype(BF16)
        state[g] = prev * jnp.exp(lc_end4) + _dot_tn(bg, xs)

        y = y + dskip_ref[:, g * GROUP_WIDTH:(g + 1) * GROUP_WIDTH] * xg
        yz = y * _silu(z_ref[0, :, g * GROUP_WIDTH:(g + 1) * GROUP_WIDTH])
        ybuf[:, g * GROUP_WIDTH:(g + 1) * GROUP_WIDTH] = yz
        ssq = ssq + jnp.sum(yz * yz, axis=-1, keepdims=True)

    rs = lax.rsqrt(ssq * (1.0 / SSM_WIDTH) + 1e-5)
    o_ref[0] = (ybuf[...] * rs * ng_ref[...]).astype(o_ref.dtype)


def _ssd_call(pa, ps, conv_w, conv_b, dtb, alog, dskip, norm_g):
    bsz, s, _ = pa.shape
    L = SCAN_CHUNK
    full = lambda b, c: (0, 0)
    return pl.pallas_call(
        _ssd_kernel,
        grid=(bsz, s // L),
        in_specs=[pl.BlockSpec((1, L, SSM_XBC), lambda b, c: (b, c, 0)),
                  pl.BlockSpec((1, L, SSM_WIDTH), lambda b, c: (b, c, SSM_XBC // SSM_WIDTH)),
                  pl.BlockSpec((1, L, LANES), lambda b, c: (b, c, 0)),
                  pl.BlockSpec(conv_w.shape, full),
                  pl.BlockSpec(conv_b.shape, full),
                  pl.BlockSpec(dtb.shape, full),
                  pl.BlockSpec(alog.shape, full),
                  pl.BlockSpec(dskip.shape, full),
                  pl.BlockSpec(norm_g.shape, full)],
        out_specs=pl.BlockSpec((1, L, SSM_WIDTH), lambda b, c: (b, c, 0)),
        out_shape=jax.ShapeDtypeStruct((bsz, s, SSM_WIDTH), BF16),
        scratch_shapes=[pltpu.VMEM((L + 8, SSM_XBC), F32),
                        pltpu.VMEM((SSM_GROUPS, SSM_STATE, GROUP_WIDTH), F32),
                        pltpu.VMEM((L, SSM_WIDTH), F32)],
        compiler_params=_cparams(("parallel", "arbitrary")),
        name="ssd_scan",
    )(pa, pa, ps, conv_w, conv_b, dtb, alog, dskip, norm_g)


def _attn_kernel(lam_init, q_ref, k_ref, v_ref, z_ref, slope_ref, corr_ref, dl_ref, ng_ref, o_ref,
                 k_aug, vt_aug, acc1, acc2):
    T = ATT_BLOCK
    D2 = 2 * DIFF_HEAD_DIM
    qi = pl.program_id(2)
    n_kv = k_ref.shape[1] // T

    @pl.when(qi == 0)
    def _():
        lane = lax.broadcasted_iota(jnp.int32, (T, LANES), 1)
        row = lax.broadcasted_iota(jnp.int32, (T, LANES), 0)
        for c in range(n_kv):
            rows = slice(c * T, (c + 1) * T)
            pos = row + c * T
            pos_lo = jnp.bitwise_and(pos, POS_SPLIT - 1)
            pos_hi = pos - pos_lo
            pos_cols = jnp.where(lane < 3, pos_lo, jnp.where(lane < 6, pos_hi, 0))
            k_aug[rows, 0:D2] = k_ref[0, rows, :].astype(BF16)
            k_aug[rows, D2:ATT_AUG] = pos_cols.astype(F32).astype(BF16)
            vt_aug[0:D2, rows] = v_ref[0, rows, :].T.astype(BF16)
        ones_row = lax.broadcasted_iota(jnp.int32, (ATT_ACC_ROWS - D2, k_ref.shape[1]), 0) == 0
        vt_aug[D2:ATT_ACC_ROWS, :] = jnp.where(ones_row, 1.0, 0.0).astype(BF16)

    lane = lax.broadcasted_iota(jnp.int32, (T, D2), 1)
    qs = q_ref[0] * (DIFF_HEAD_DIM ** -0.5 * LOG2E)
    sl2 = slope_ref[0][:, 0:1] * LOG2E
    sl_hi = sl2.astype(BF16).astype(F32)
    sl_mid = (sl2 - sl_hi).astype(BF16).astype(F32)
    sl_lo = sl2 - sl_hi - sl_mid
    lane1 = lax.broadcasted_iota(jnp.int32, (1, LANES), 1)
    ext = jnp.where((lane1 == 0) | (lane1 == 3), sl_hi,
                    jnp.where((lane1 == 1) | (lane1 == 4), sl_mid,
                              jnp.where((lane1 == 2) | (lane1 == 5), sl_lo, 0.0)))
    ext = jnp.broadcast_to(ext, (T, LANES)).astype(BF16)
    qa1 = jnp.concatenate([jnp.where(lane < DIFF_HEAD_DIM, qs, 0.0).astype(BF16), ext], axis=1)
    qa2 = jnp.concatenate([jnp.where(lane >= DIFF_HEAD_DIM, qs, 0.0).astype(BF16), ext], axis=1)

    acc1[...] = jnp.zeros(acc1.shape, F32)
    acc2[...] = jnp.zeros(acc2.shape, F32)

    def block(first, n, ms, corr):
        start = pl.multiple_of(first * T, T)
        kb = k_aug[pl.ds(start, n * T), :]
        vb = vt_aug[:, pl.ds(start, n * T)]

        def one(qa, m, acc):
            s = _dot_nt(kb, qa)
            if corr is not None and n == 1:
                s = s + corr
            elif corr is not None:
                s = jnp.concatenate([s[:(n - 1) * T], s[(n - 1) * T:] + corr], axis=0)
            m_new = jnp.maximum(m, jnp.max(s, axis=0, keepdims=True))
            p = jnp.exp2(s - m_new).astype(BF16)
            acc[...] = jnp.exp2(m - m_new) * acc[...] + _dot(vb, p)
            return m_new

        return one(qa1, ms[0], acc1), one(qa2, ms[1], acc2)

    m0 = jnp.full((1, T), NEG_BIG, F32)
    ms = lax.fori_loop(0, qi // ATT_GROUP, lambda i, ms: block(ATT_GROUP * i, ATT_GROUP, ms, None), (m0, m0))
    corr = corr_ref[...] * sl2
    for left in range(ATT_GROUP):
        @pl.when(qi % ATT_GROUP == left)
        def _():
            block(qi - left, left + 1, ms, corr)

    dl = dl_ref[...]
    lam = (jnp.exp(jnp.sum(dl[0:1] * dl[1:2], axis=-1, keepdims=True))
           - jnp.exp(jnp.sum(dl[2:3] * dl[3:4], axis=-1, keepdims=True)) + lam_init)
    o_t = acc1[0:D2, :] / acc1[D2:D2 + 1, :] - lam * (acc2[0:D2, :] / acc2[D2:D2 + 1, :])
    o = o_t.T
    y = o * lax.rsqrt(jnp.mean(o * o, axis=-1, keepdims=True) + 1e-5) * ng_ref[...]
    y = y * (1.0 - lam_init)
    o_ref[0] = (y * _silu(z_ref[0])).astype(o_ref.dtype)


def _attn_corr_table():
    kk = jnp.arange(ATT_BLOCK)[:, None]
    qq = jnp.arange(ATT_BLOCK)[None, :]
    visible = (kk // CHUNK) <= (qq // CHUNK)
    return jnp.where(visible, -2.0 * jnp.maximum(kk - qq, 0).astype(F32), NEG_BIG)


def _attn_call(proj, slopes, dl, norm_g, lam_init):
    bsz, s, _ = proj.shape
    T = ATT_BLOCK
    hw = 2 * DIFF_HEAD_DIM
    blk = lambda off: off // hw
    const2 = lambda b, h, i: (0, 0)
    return pl.pallas_call(
        functools.partial(_attn_kernel, lam_init),
        grid=(bsz, DIFF_HEADS, s // T),
        in_specs=[pl.BlockSpec((1, T, hw), lambda b, h, i: (b, i, blk(P_OFF_QB) + h)),
                  pl.BlockSpec((1, s, hw), lambda b, h, i: (b, 0, blk(P_OFF_KB) + h)),
                  pl.BlockSpec((1, s, hw), lambda b, h, i: (b, 0, blk(P_OFF_VB) + h)),
                  pl.BlockSpec((1, T, hw), lambda b, h, i: (b, i, blk(P_OFF_ZB) + h)),
                  pl.BlockSpec((1, 1, LANES), lambda b, h, i: (h, 0, 0)),
                  pl.BlockSpec((T, T), const2),
                  pl.BlockSpec(dl.shape, const2),
                  pl.BlockSpec(norm_g.shape, const2)],
        out_specs=pl.BlockSpec((1, T, hw), lambda b, h, i: (b, i, h)),
        out_shape=jax.ShapeDtypeStruct((bsz, s, DIFF_WIDTH), BF16),
        scratch_shapes=[pltpu.VMEM((s, ATT_AUG), BF16),
                        pltpu.VMEM((ATT_ACC_ROWS, s), BF16),
                        pltpu.VMEM((ATT_ACC_ROWS, T), F32),
                        pltpu.VMEM((ATT_ACC_ROWS, T), F32)],
        compiler_params=_cparams(("parallel", "parallel", "arbitrary")),
        name="diff_attention",
    )(proj, proj, proj, proj, slopes, _attn_corr_table(), dl, norm_g)


def _mlstm_kernel(q_ref, k_ref, v_ref, o_ref_in, z_ref, ps_ref, gb_ref, out_ref, c_state, m_state):
    L = SCAN_CHUNK
    D = MLSTM_HEAD_DIM
    c = pl.program_id(1)

    @pl.when(c == 0)
    def _():
        c_state[...] = jnp.zeros(c_state.shape, F32)
        m_state[...] = jnp.zeros(m_state.shape, F32)

    ones_col = jnp.where(lax.broadcasted_iota(jnp.int32, (L, D), 1) == 0, 1.0, 0.0)

    rows = lax.broadcasted_iota(jnp.int32, (L, L), 0)
    cols = lax.broadcasted_iota(jnp.int32, (L, L), 1)
    causal = rows >= cols
    tril = jnp.where(causal, 1.0, 0.0).astype(BF16)

    raw = ps_ref[0] + gb_ref[...]
    logf = -_softplus(-raw)
    bcum = _cumsum_rows(logf, tril)
    bcum_t = bcum.T
    raw_t = raw.T

    for h in range(MLSTM_HEADS):
        li, lf = LANE_I + h, LANE_F + h
        sl = slice(h * D, (h + 1) * D)
        q = q_ref[0, :, sl]
        ks = k_ref[0, :, sl] * (D ** -0.5)
        v_ext = jnp.concatenate([v_ref[0, :, sl], ones_col], axis=1)
        qb, kb = q.astype(BF16), ks.astype(BF16)

        b_col = bcum[:, lf:lf + 1]
        b_row = bcum_t[lf:lf + 1, :]
        ig_col = raw[:, li:li + 1]
        ig_row = raw_t[li:li + 1, :]
        btot = bcum[L - 1:L, lf:lf + 1]
        m_prev = m_state[h:h + 1, 0:1]
        c_prev = c_state[h]

        dmat = jnp.where(causal, b_col - b_row + ig_row, NEG_BIG)
        m_inter = b_col + m_prev
        m_row = jnp.maximum(m_inter, jnp.max(dmat, axis=-1, keepdims=True))
        wts = jnp.exp(dmat - m_row)
        sw = wts * _dot_nt(qb, kb)
        a_inter = jnp.exp(m_inter - m_row)
        nd = _dot(sw.astype(BF16), v_ext.astype(BF16)) + _dot(qb, c_prev.astype(BF16)) * a_inter
        hout = nd[:, :D] / jnp.maximum(jnp.abs(nd[:, D:D + 1]), jnp.exp(-m_row))
        gate = jax.nn.sigmoid(o_ref_in[0, :, sl]) * _silu(z_ref[0, :, sl])
        out_ref[0, :, sl] = (gate * hout).astype(out_ref.dtype)

        w_end = btot - b_col + ig_col
        m_loc = jnp.max(w_end, axis=0, keepdims=True)
        m_new = jnp.maximum(btot + m_prev, m_loc)
        a_old = jnp.exp(btot + m_prev - m_new)
        e_end = jnp.exp(w_end - m_new)
        c_state[h] = a_old * c_prev + _dot_tn(kb, (v_ext * e_end).astype(BF16))
        m_state[h:h + 1, :] = jnp.broadcast_to(m_new, (1, LANES))


def _mlstm_call(proj, ps, gate_bias):
    bsz, s, _ = proj.shape
    L = SCAN_CHUNK
    W = MLSTM_WIDTH
    blk0 = P_OFF_C // W
    return pl.pallas_call(
        _mlstm_kernel,
        grid=(bsz, s // L),
        in_specs=[pl.BlockSpec((1, L, W), lambda b, c: (b, c, blk0)),
                  pl.BlockSpec((1, L, W), lambda b, c: (b, c, blk0 + 1)),
                  pl.BlockSpec((1, L, W), lambda b, c: (b, c, blk0 + 2)),
                  pl.BlockSpec((1, L, W), lambda b, c: (b, c, blk0 + 3)),
                  pl.BlockSpec((1, L, W), lambda b, c: (b, c, blk0 + 4)),
                  pl.BlockSpec((1, L, LANES), lambda b, c: (b, c, 0)),
                  pl.BlockSpec(gate_bias.shape, lambda b, c: (0, 0))],
        out_specs=pl.BlockSpec((1, L, W), lambda b, c: (b, c, 0)),
        out_shape=jax.ShapeDtypeStruct((bsz, s, W), BF16),
        scratch_shapes=[pltpu.VMEM((MLSTM_HEADS, MLSTM_HEAD_DIM, 2 * MLSTM_HEAD_DIM), F32),
                        pltpu.VMEM((MLSTM_HEADS, LANES), F32)],
        compiler_params=_cparams(("parallel", "arbitrary")),
        name="mlstm_scan",
    )(proj, proj, proj, proj, proj, ps, gate_bias)


def _merge_kernel(x_ref, oa_ref, ob_ref, oc_ref, wga_ref, wgb_ref, wgc_ref, wa_ref, wb_ref, wc_ref, o_ref):
    x = x_ref[...]
    acc = jax.nn.sigmoid(_dot_nt(x, wga_ref[...])) * _dot(oa_ref[...], wa_ref[...])
    acc = acc + jax.nn.sigmoid(_dot_nt(x, wgb_ref[...])) * _dot(ob_ref[...], wb_ref[...])
    acc = acc + jax.nn.sigmoid(_dot_nt(x, wgc_ref[...])) * _dot(oc_ref[...], wc_ref[...])
    o_ref[...] = acc.astype(o_ref.dtype)


def _merge_call(xb, oa, ob, oc, w_all, w_br, li, tm, tn):
    m = xb.shape[0]
    row = lambda i, j: (i, 0)
    gate_spec = lambda br: pl.BlockSpec(
        (None, tn, D_MODEL), lambda i, j: (li, (W_OFF_GATES + br * D_MODEL) // tn + j, 0))
    return pl.pallas_call(
        _merge_kernel,
        grid=(m // tm, D_MODEL // tn),
        in_specs=[pl.BlockSpec((tm, D_MODEL), row),
                  pl.BlockSpec((tm, SSM_WIDTH), row, pipeline_mode=pl.Buffered(1)),
                  pl.BlockSpec((tm, DIFF_WIDTH), row, pipeline_mode=pl.Buffered(1)),
                  pl.BlockSpec((tm, MLSTM_WIDTH), row, pipeline_mode=pl.Buffered(1)),
                  gate_spec(0), gate_spec(1), gate_spec(2),
                  pl.BlockSpec((None, SSM_WIDTH, tn), lambda i, j: (li, 0, j)),
                  pl.BlockSpec((None, DIFF_WIDTH, tn), lambda i, j: (li, SSM_WIDTH // DIFF_WIDTH, j)),
                  pl.BlockSpec((None, MLSTM_WIDTH, tn),
                               lambda i, j: (li, (SSM_WIDTH + DIFF_WIDTH) // MLSTM_WIDTH, j))],
        out_specs=pl.BlockSpec((tm, tn), lambda i, j: (i, j)),
        out_shape=jax.ShapeDtypeStruct((m, D_MODEL), BF16),
        compiler_params=_cparams(("parallel", "arbitrary")),
        name="gated_merge",
    )(xb, oa, ob, oc, w_all, w_all, w_all, w_br, w_br, w_br)


def _final_kernel(nj, tn, mg_ref, wo_ref, wg_ref, x_ref, p_ref, wple_ref, lng_ref, lnb_ref, pg_ref,
                  o_ref, ob_ref, h_scr, hb_scr, st_scr):
    j = pl.program_id(1)
    tm = h_scr.shape[0]

    @pl.when(j == 0)
    def _():
        st_scr[...] = jnp.zeros(st_scr.shape, F32)

    @pl.when(j < nj)
    def _():
        off = pl.multiple_of(j * tn, tn)
        pre = DEEPNORM_ALPHA * x_ref[...] + _dot(mg_ref[...], wo_ref[...])
        h_scr[:, pl.ds(off, tn)] = pre
        st_scr[:, 0:LANES] += jnp.sum(pre, axis=-1, keepdims=True)

    @pl.when(j == nj)
    def _():
        def norm_rows(r, carry):
            rows = pl.ds(pl.multiple_of(r * NORM_ROWS, NORM_ROWS), NORM_ROWS)
            cen = h_scr[rows, :] - st_scr[rows, 0:1] * (1.0 / D_MODEL)
            var = jnp.mean(cen * cen, axis=-1, keepdims=True)
            h = cen * lax.rsqrt(var + 1e-5) * lng_ref[...] + lnb_ref[...]
            h_scr[rows, :] = h
            hb_scr[rows, :] = h.astype(BF16)
            e = _dot(p_ref[rows, :], wple_ref[...])
            st_scr[rows, LANES:2 * LANES] = jnp.broadcast_to(
                jnp.sum(e * e, axis=-1, keepdims=True), (NORM_ROWS, LANES))
            return carry

        lax.fori_loop(0, tm // NORM_ROWS, norm_rows, 0)

    @pl.when(j >= nj)
    def _():
        off = pl.multiple_of((j - nj) * tn, tn)
        gate = jax.nn.sigmoid(_dot(hb_scr[...], wg_ref[...]))
        rs = lax.rsqrt(st_scr[:, LANES:LANES + 1] * (1.0 / D_MODEL) + 1e-5)
        e = _dot(p_ref[...], wple_ref[:, pl.ds(off, tn)]) * rs * pg_ref[:, pl.ds(off, tn)]
        out = h_scr[:, pl.ds(off, tn)] + gate * e
        o_ref[...] = out
        ob_ref[...] = out.astype(BF16)


def _final_call(merged, w_out, w_pg, x, pb, w_ple, ln_g, ln_b, ple_g, li, tm, tn):
    m = merged.shape[0]
    nj = D_MODEL // tn
    full = lambda i, j: (0, 0)
    out_idx = lambda i, j: (i, jnp.maximum(j - nj, 0))
    return pl.pallas_call(
        functools.partial(_final_kernel, nj, tn),
        grid=(m // tm, 2 * nj),
        in_specs=[pl.BlockSpec((tm, D_MODEL), lambda i, j: (i, 0), pipeline_mode=pl.Buffered(1)),
                  pl.BlockSpec((None, D_MODEL, tn), lambda i, j: (li, 0, jnp.minimum(j, nj - 1))),
                  pl.BlockSpec((None, D_MODEL, tn), lambda i, j: (li, 0, jnp.maximum(j - nj, 0))),
                  pl.BlockSpec((tm, tn), lambda i, j: (i, jnp.minimum(j, nj - 1))),
                  pl.BlockSpec((tm, PLE_DIM), lambda i, j: (i, 0)),
                  pl.BlockSpec((None,) + w_ple.shape[1:], lambda i, j: (li, 0, 0), pipeline_mode=pl.Buffered(1)),
                  pl.BlockSpec(ln_g.shape, full),
                  pl.BlockSpec(ln_b.shape, full),
                  pl.BlockSpec(ple_g.shape, full)],
        out_specs=[pl.BlockSpec((tm, tn), out_idx), pl.BlockSpec((tm, tn), out_idx)],
        out_shape=[jax.ShapeDtypeStruct((m, D_MODEL), F32), jax.ShapeDtypeStruct((m, D_MODEL), BF16)],
        scratch_shapes=[pltpu.VMEM((tm, D_MODEL), F32),
                        pltpu.VMEM((tm, D_MODEL), BF16),
                        pltpu.VMEM((tm, 2 * LANES), F32)],
        compiler_params=_cparams(("parallel", "arbitrary")),
        name="outproj_deepnorm_ple",
    )(merged, w_out, w_pg, x, pb, w_ple, ln_g, ln_b, ple_g)


def _row_tile(m, pref):
    return pref if m % pref == 0 else m


def _relayout_kernel(a_ref, dt_ref, if_ref, o_ref, os_ref):
    o_ref[...] = a_ref[0].astype(BF16)

    @pl.when(pl.program_id(1) == 0)
    def _():
        os_ref[LANE_DT:LANE_I, :] = dt_ref[0].astype(BF16)
        os_ref[LANE_I:LANE_F + MLSTM_HEADS, :] = if_ref[0].astype(BF16)
        os_ref[LANE_F + MLSTM_HEADS:, :] = jnp.zeros((LANES - LANE_F - MLSTM_HEADS, os_ref.shape[1]), BF16)


def _relayout_call(w_t, rows):
    depth, _, d = w_t.shape

    def src_row(j):
        r = j * rows
        shift = jnp.where(r < P_OFF_QB, 0, jnp.where(r < W_OFF_GATES, (OFF_QB - P_OFF_QB) // F32_SUBLANES,
                                                     (OFF_GATES - W_OFF_GATES) // F32_SUBLANES))
        return (j * (rows // F32_SUBLANES) + shift) * F32_SUBLANES

    return pl.pallas_call(
        _relayout_kernel,
        grid=(depth, W_OFF_SMALL // rows),
        in_specs=[pl.BlockSpec((pl.Element(1), pl.Element(rows), pl.Element(d)), lambda l, j: (l, src_row(j), 0)),
                  pl.BlockSpec((pl.Element(1), pl.Element(SSM_HEADS), pl.Element(d)), lambda l, j: (l, OFF_DT, 0)),
                  pl.BlockSpec((pl.Element(1), pl.Element(2 * MLSTM_HEADS), pl.Element(d)),
                               lambda l, j: (l, OFF_IC, 0))],
        out_specs=[pl.BlockSpec((None, rows, d), lambda l, j: (l, j, 0)),
                   pl.BlockSpec((None, LANES, d), lambda l, j: (l, 0, 0))],
        out_shape=[jax.ShapeDtypeStruct((depth, W_OFF_SMALL, d), BF16),
                   jax.ShapeDtypeStruct((depth, LANES, d), BF16)],
        compiler_params=_cparams(("parallel", "arbitrary")),
        name="weight_relayout",
    )(w_t, w_t, w_t)


def _prep_weights(w_in, w_branch, w_out, w_ple, w_ple_gate):
    w_main, w_small = _relayout_call(jnp.swapaxes(w_in, 1, 2), 512)
    return (w_main, w_small, w_branch.astype(BF16), w_out.astype(BF16), w_ple.astype(BF16),
            w_ple_gate.astype(BF16))


def _layer(x, xb, p_i, li, weights, conv_w, conv_b, dt_bias, a_log, d_skip, ssm_norm_g,
           diff_lambda, diff_norm_g, mlstm_gate_b, ln_g, ln_b, ple_norm_g):
    w_all, w_small, w_br, w_out, w_ple, w_pg = weights
    bsz, s, _ = x.shape
    t = bsz * s
    x2 = x.reshape(t, D_MODEL)
    xb2 = xb.reshape(t, D_MODEL)

    proj, ps = _proj_call(xb2, w_all, w_small, li, _row_tile(t, 1024), 1024)
    proj = proj.reshape(bsz, s, PROJ_WIDTH)
    ps = ps.reshape(bsz, s, LANES)

    pad = lambda v, lane0: jnp.zeros((1, LANES), F32).at[0, lane0:lane0 + v.shape[0]].set(v.astype(F32))
    out_a = _ssd_call(proj, ps, conv_w, conv_b.reshape(1, -1), pad(dt_bias, LANE_DT), pad(a_log, LANE_DT),
                      jnp.repeat(d_skip.astype(F32), SSM_HEAD_DIM).reshape(1, -1), ssm_norm_g.reshape(1, -1))

    lam_init = 0.8 - 0.6 * math.exp(-0.3 * li)
    slopes = 2.0 ** (-8.0 * jnp.arange(1, DIFF_HEADS + 1, dtype=F32) / DIFF_HEADS)
    slopes = jnp.broadcast_to(slopes[:, None, None], (DIFF_HEADS, 1, LANES))
    out_b = _attn_call(proj, slopes, diff_lambda, diff_norm_g.reshape(1, -1), lam_init)

    gate_bias = pad(mlstm_gate_b[0], LANE_I) + pad(mlstm_gate_b[1], LANE_F)
    out_c = _mlstm_call(proj, ps, gate_bias)

    tm = _row_tile(t, 1024)
    merged = _merge_call(xb2, out_a.reshape(t, -1), out_b.reshape(t, -1), out_c.reshape(t, -1),
                         w_all, w_br, li, tm, 256)
    out, out_b16 = _final_call(merged, w_out, w_pg, x2, p_i.reshape(t, PLE_DIM).astype(BF16), w_ple,
                               ln_g.reshape(1, -1), ln_b.reshape(1, -1), ple_norm_g.reshape(1, -1), li, tm, 256)
    return out.reshape(bsz, s, D_MODEL), out_b16.reshape(bsz, s, D_MODEL)


def kernel(x, p, w_in, conv_w, conv_b, dt_bias, a_log, d_skip, ssm_norm_g, diff_lambda, diff_norm_g,
           mlstm_gate_b, w_branch, w_out, ln_g, ln_b, w_ple, ple_norm_g, w_ple_gate):
    weights = _prep_weights(w_in, w_branch, w_out, w_ple, w_ple_gate)
    h = x
    hb = x.astype(BF16)
    for li in range(w_in.shape[0]):
        h, hb = _layer(h, hb, p[li], li, weights, conv_w[li], conv_b[li], dt_bias[li], a_log[li],
                       d_skip[li], ssm_norm_g[li], diff_lambda[li], diff_norm_g[li],
                       mlstm_gate_b[li], ln_g[li], ln_b[li], ple_norm_g[li])
    return h
```

```python
import functools
import math

import jax
import jax.numpy as jnp
from jax import lax
from jax.experimental import pallas as pl
from jax.experimental.pallas import tpu as pltpu

F32 = jnp.float32
BF16 = jnp.bfloat16

V7X_VMEM_LIMIT_BYTES = 56 * 1024 * 1024
LANES = 128
F32_SUBLANES = 8

D_MODEL = 4096
CHUNK = 64
SSM_WIDTH = 2048
SSM_HEAD_DIM = 64
SSM_HEADS = 32
SSM_GROUPS = 8
SSM_STATE = 128
SSM_XBC = SSM_WIDTH + 2 * SSM_GROUPS * SSM_STATE
GROUP_WIDTH = SSM_WIDTH // SSM_GROUPS
HEADS_PER_GROUP = SSM_HEADS // SSM_GROUPS
DIFF_WIDTH = 1024
DIFF_HEAD_DIM = 64
DIFF_HEADS = 8
MLSTM_WIDTH = 1024
MLSTM_HEAD_DIM = 128
MLSTM_HEADS = 8
N_BRANCH = 3
PLE_DIM = 256
DEPTH = 2
DEEPNORM_ALPHA = (2.0 * DEPTH) ** 0.25

OFF_XBC = 0
OFF_ZA = OFF_XBC + SSM_XBC
OFF_DT = OFF_ZA + SSM_WIDTH
OFF_QB = OFF_DT + SSM_HEADS
OFF_ZB = OFF_QB + 3 * DIFF_WIDTH
OFF_QC = OFF_ZB + DIFF_WIDTH
OFF_IC = OFF_QC + 5 * MLSTM_WIDTH
OFF_FC = OFF_IC + MLSTM_HEADS
OFF_GATES = OFF_FC + MLSTM_HEADS

P_OFF_XBC = 0
P_OFF_ZA = P_OFF_XBC + SSM_XBC
P_OFF_QB = P_OFF_ZA + SSM_WIDTH
P_OFF_KB = P_OFF_QB + DIFF_WIDTH
P_OFF_VB = P_OFF_KB + DIFF_WIDTH
P_OFF_ZB = P_OFF_VB + DIFF_WIDTH
P_OFF_C = P_OFF_ZB + DIFF_WIDTH
PROJ_WIDTH = P_OFF_C + 5 * MLSTM_WIDTH
W_OFF_GATES = PROJ_WIDTH
W_OFF_SMALL = W_OFF_GATES + N_BRANCH * D_MODEL

LANE_DT = 0
LANE_I = SSM_HEADS
LANE_F = SSM_HEADS + MLSTM_HEADS

SCAN_CHUNK = 256
ATT_BLOCK = 512
ATT_GROUP = 4
ATT_STAGE = 1
ATT_AUG = 2 * LANES
ATT_ACC_ROWS = LANES + 16
POS_SPLIT = 256
NORM_ROWS = 128
NEG_BIG = -1e30
LOG2E = 1.4426950408889634


def _cparams(sem):
    return pltpu.CompilerParams(dimension_semantics=sem, vmem_limit_bytes=V7X_VMEM_LIMIT_BYTES)


def _silu(x):
    return x * jax.nn.sigmoid(x)


def _softplus(x):
    return jnp.maximum(x, 0.0) + jnp.log1p(jnp.exp(-jnp.abs(x)))


def _dot(a, b):
    return jnp.dot(a, b, preferred_element_type=F32)


def _dot_nt(a, b):
    return lax.dot_general(a, b, (((1,), (1,)), ((), ())), preferred_element_type=F32)


def _dot_tn(a, b):
    return lax.dot_general(a, b, (((0,), (0,)), ((), ())), preferred_element_type=F32)


def _cumsum_rows(x, tril_bf16):
    hi = x.astype(BF16)
    r1 = x - hi.astype(F32)
    mid = r1.astype(BF16)
    lo = (r1 - mid.astype(F32)).astype(BF16)
    return _dot(tril_bf16, hi) + _dot(tril_bf16, mid) + _dot(tril_bf16, lo)


def _proj_kernel(x_ref, w_ref, ws_ref, o_ref, os_ref):
    o_ref[...] = _dot_nt(x_ref[...], w_ref[...])

    @pl.when(pl.program_id(1) == 0)
    def _():
        os_ref[...] = _dot_nt(x_ref[...], ws_ref[...])


def _proj_call(xb, w_main, w_small, li, tm, tn):
    m, k = xb.shape
    return pl.pallas_call(
        _proj_kernel,
        grid=(m // tm, PROJ_WIDTH // tn),
        in_specs=[pl.BlockSpec((tm, k), lambda i, j: (i, 0)),
                  pl.BlockSpec((None, tn, k), lambda i, j: (li, j, 0)),
                  pl.BlockSpec((None, LANES, k), lambda i, j: (li, 0, 0))],
        out_specs=[pl.BlockSpec((tm, tn), lambda i, j: (i, j)),
                   pl.BlockSpec((tm, LANES), lambda i, j: (i, 0))],
        out_shape=[jax.ShapeDtypeStruct((m, PROJ_WIDTH), F32), jax.ShapeDtypeStruct((m, LANES), F32)],
        compiler_params=_cparams(("parallel", "arbitrary")),
        name="in_proj",
    )(xb, w_main, w_small)


def _ssd_kernel(xbc_ref, z_ref, ps_ref, cw_ref, cb_ref, dtb_ref, alog_ref, dskip_ref, ng_ref,
                o_ref, xpad, state, ybuf):
    L = SCAN_CHUNK
    c = pl.program_id(1)

    @pl.when(c == 0)
    def _():
        xpad[0:8, :] = jnp.zeros((8, SSM_XBC), F32)
        state[...] = jnp.zeros(state.shape, F32)

    @pl.when(c > 0)
    def _():
        xpad[0:8, :] = xpad[L:L + 8, :]

    xpad[8:8 + L, :] = xbc_ref[0]

    def conv_silu(c0, width):
        acc = cb_ref[:, c0:c0 + width] + cw_ref[3:4, c0:c0 + width] * xpad[8:8 + L, c0:c0 + width]
        acc = acc + cw_ref[2:3, c0:c0 + width] * xpad[7:7 + L, c0:c0 + width]
        acc = acc + cw_ref[1:2, c0:c0 + width] * xpad[6:6 + L, c0:c0 + width]
        acc = acc + cw_ref[0:1, c0:c0 + width] * xpad[5:5 + L, c0:c0 + width]
        return _silu(acc)

    rows = lax.broadcasted_iota(jnp.int32, (L, L), 0)
    cols = lax.broadcasted_iota(jnp.int32, (L, L), 1)
    causal = rows >= cols
    tril = jnp.where(causal, 1.0, 0.0).astype(BF16)

    dt = _softplus(ps_ref[0] + dtb_ref[...])
    la = dt * (-jnp.exp(alog_ref[...]))
    lc = _cumsum_rows(la, tril)
    lc_t = lc.T

    head_of_lane = lax.broadcasted_iota(jnp.int32, (L, GROUP_WIDTH), 1) // SSM_HEAD_DIM

    def expand(vals, hol):
        out = vals[HEADS_PER_GROUP - 1]
        for r in range(HEADS_PER_GROUP - 2, -1, -1):
            out = jnp.where(hol == r, vals[r], out)
        return out

    ssq = jnp.zeros((L, 1), F32)
    for g in range(SSM_GROUPS):
        h0 = g * HEADS_PER_GROUP
        xg = conv_silu(g * GROUP_WIDTH, GROUP_WIDTH)
        bg = conv_silu(SSM_WIDTH + g * SSM_STATE, SSM_STATE).astype(BF16)
        cg = conv_silu(SSM_WIDTH + SSM_GROUPS * SSM_STATE + g * SSM_STATE, SSM_STATE).astype(BF16)

        dt4 = expand([dt[:, h0 + r:h0 + r + 1] for r in range(HEADS_PER_GROUP)], head_of_lane)
        lc4 = expand([lc[:, h0 + r:h0 + r + 1] for r in range(HEADS_PER_GROUP)], head_of_lane)
        lc_end4 = expand([lc[L - 1:L, h0 + r:h0 + r + 1] for r in range(HEADS_PER_GROUP)],
                         head_of_lane[0:1, :])
        xdt = xg * dt4

        cb = _dot_nt(cg, bg)
        y = jnp.zeros((L, GROUP_WIDTH), F32)
        for r in range(HEADS_PER_GROUP):
            h = h0 + r
            seg = lc[:, h:h + 1] - lc_t[h:h + 1, :]
            decay = jnp.where(causal, jnp.exp(seg), 0.0)
            m_r = (cb * decay).astype(BF16)
            x_r = jnp.where(head_of_lane == r, xdt, 0.0).astype(BF16)
            y = y + _dot(m_r, x_r)

        prev = state[g]
        y = y + _dot(cg, prev.astype(BF16)) * jnp.exp(lc4)
        xs = (xdt * jnp.exp(lc_end4 - lc4)).astype(BF16)
        state[g] = prev * jnp.exp(lc_end4) + _dot_tn(bg, xs)

        y = y + dskip_ref[:, g * GROUP_WIDTH:(g + 1) * GROUP_WIDTH] * xg
        yz = y * _silu(z_ref[0, :, g * GROUP_WIDTH:(g + 1) * GROUP_WIDTH])
        ybuf[:, g * GROUP_WIDTH:(g + 1) * GROUP_WIDTH] = yz
        ssq = ssq + jnp.sum(yz * yz, axis=-1, keepdims=True)

    rs = lax.rsqrt(ssq * (1.0 / SSM_WIDTH) + 1e-5)
    o_ref[0] = (ybuf[...] * rs * ng_ref[...]).astype(o_ref.dtype)


def _ssd_call(pa, ps, conv_w, conv_b, dtb, alog, dskip, norm_g):
    bsz, s, _ = pa.shape
    L = SCAN_CHUNK
    full = lambda b, c: (0, 0)
    return pl.pallas_call(
        _ssd_kernel,
        grid=(bsz, s // L),
        in_specs=[pl.BlockSpec((1, L, SSM_XBC), lambda b, c: (b, c, 0)),
                  pl.BlockSpec((1, L, SSM_WIDTH), lambda b, c: (b, c, SSM_XBC // SSM_WIDTH)),
                  pl.BlockSpec((1, L, LANES), lambda b, c: (b, c, 0)),
                  pl.BlockSpec(conv_w.shape, full),
                  pl.BlockSpec(conv_b.shape, full),
                  pl.BlockSpec(dtb.shape, full),
                  pl.BlockSpec(alog.shape, full),
                  pl.BlockSpec(dskip.shape, full),
                  pl.BlockSpec(norm_g.shape, full)],
        out_specs=pl.BlockSpec((1, L, SSM_WIDTH), lambda b, c: (b, c, 0)),
        out_shape=jax.ShapeDtypeStruct((bsz, s, SSM_WIDTH), BF16),
        scratch_shapes=[pltpu.VMEM((L + 8, SSM_XBC), F32),
                        pltpu.VMEM((SSM_GROUPS, SSM_STATE, GROUP_WIDTH), F32),
                        pltpu.VMEM((L, SSM_WIDTH), F32)],
        compiler_params=_cparams(("parallel", "arbitrary")),
        name="ssd_scan",
    )(pa, pa, ps, conv_w, conv_b, dtb, alog, dskip, norm_g)


def _attn_kernel(lam_init, q_ref, k_ref, v_ref, z_ref, slope_ref, corr_ref, dl_ref, ng_ref, o_ref,
                 k_aug, vt_aug, acc1, acc2):
    T = ATT_BLOCK
    D2 = 2 * DIFF_HEAD_DIM
    qi = pl.program_id(2)
    n_kv = k_ref.shape[1] // T

    @pl.when(qi == 0)
    def _():
        lane = lax.broadcasted_iota(jnp.int32, (T, LANES), 1)
        row = lax.broadcasted_iota(jnp.int32, (T, LANES), 0)
        for c in range(n_kv):
            rows = slice(c * T, (c + 1) * T)
            pos = row + c * T
            pos_lo = jnp.bitwise_and(pos, POS_SPLIT - 1)
            pos_hi = pos - pos_lo
            pos_cols = jnp.where(lane < 3, pos_lo, jnp.where(lane < 6, pos_hi, 0))
            k_aug[rows, 0:D2] = k_ref[0, rows, :].astype(BF16)
            k_aug[rows, D2:ATT_AUG] = pos_cols.astype(F32).astype(BF16)
            vt_aug[0:D2, rows] = v_ref[0, rows, :].T.astype(BF16)
        ones_row = lax.broadcasted_iota(jnp.int32, (ATT_ACC_ROWS - D2, k_ref.shape[1]), 0) == 0
        vt_aug[D2:ATT_ACC_ROWS, :] = jnp.where(ones_row, 1.0, 0.0).astype(BF16)

    lane = lax.broadcasted_iota(jnp.int32, (T, D2), 1)
    qs = q_ref[0] * (DIFF_HEAD_DIM ** -0.5 * LOG2E)
    sl2 = slope_ref[0][:, 0:1] * LOG2E
    sl_hi = sl2.astype(BF16).astype(F32)
    sl_mid = (sl2 - sl_hi).astype(BF16).astype(F32)
    sl_lo = sl2 - sl_hi - sl_mid
    lane1 = lax.broadcasted_iota(jnp.int32, (1, LANES), 1)
    ext = jnp.where((lane1 == 0) | (lane1 == 3), sl_hi,
                    jnp.where((lane1 == 1) | (lane1 == 4), sl_mid,
                              jnp.where((lane1 == 2) | (lane1 == 5), sl_lo, 0.0)))
    ext = jnp.broadcast_to(ext, (T, LANES)).astype(BF16)
    qa1 = jnp.concatenate([jnp.where(lane < DIFF_HEAD_DIM, qs, 0.0).astype(BF16), ext], axis=1)
    qa2 = jnp.concatenate([jnp.where(lane >= DIFF_HEAD_DIM, qs, 0.0).astype(BF16), ext], axis=1)

    acc1[...] = jnp.zeros(acc1.shape, F32)
    acc2[...] = jnp.zeros(acc2.shape, F32)

    def block(first, n, ms, corr):
        qas, accs = (qa1, qa2), (acc1, acc2)
        starts = [pl.multiple_of((first + t) * T, T) for t in range(n)]
        stages = [list(range(i, min(i + ATT_STAGE, n))) for i in range(0, n, ATT_STAGE)]

        def score_step(t, m_run, out):
            kb = k_aug[pl.ds(starts[t], T), :]
            for a in range(2):
                s = _dot_nt(kb, qas[a])
                if corr is not None and t == n - 1:
                    s = s + corr
                out[a].append(s)
                m_run[a] = jnp.maximum(m_run[a], jnp.max(s, axis=0, keepdims=True))

        def value_step(t, idx, scores, m_stage, pv):
            vb = vt_aug[:, pl.ds(starts[t], T)]
            for a in range(2):
                d = _dot(vb, jnp.exp2(scores[a][idx] - m_stage[a]).astype(BF16))
                pv[a] = d if pv[a] is None else pv[a] + d

        m_prev = list(ms)
        pending = None
        for stage in stages + [None]:
            scores, m_run = [[], []], (list(pending[2]) if pending else list(ms))
            pv = [None, None]
            for idx in range(ATT_STAGE):
                if stage is not None and idx < len(stage):
                    score_step(stage[idx], m_run, scores)
                if pending is not None and idx < len(pending[0]):
                    value_step(pending[0][idx], idx, pending[1], pending[2], pv)
            if pending is not None:
                for a in range(2):
                    accs[a][...] = jnp.exp2(m_prev[a] - pending[2][a]) * accs[a][...] + pv[a]
                m_prev = list(pending[2])
            pending = (stage, scores, m_run) if stage is not None else None
        return tuple(m_prev)

    m0 = jnp.full((1, T), NEG_BIG, F32)
    ms = lax.fori_loop(0, qi // ATT_GROUP, lambda i, ms: block(ATT_GROUP * i, ATT_GROUP, ms, None), (m0, m0))
    corr = corr_ref[...] * sl2
    for left in range(ATT_GROUP):
        @pl.when(qi % ATT_GROUP == left)
        def _():
            block(qi - left, left + 1, ms, corr)

    dl = dl_ref[...]
    lam = (jnp.exp(jnp.sum(dl[0:1] * dl[1:2], axis=-1, keepdims=True))
           - jnp.exp(jnp.sum(dl[2:3] * dl[3:4], axis=-1, keepdims=True)) + lam_init)
    o_t = acc1[0:D2, :] / acc1[D2:D2 + 1, :] - lam * (acc2[0:D2, :] / acc2[D2:D2 + 1, :])
    o = o_t.T
    y = o * lax.rsqrt(jnp.mean(o * o, axis=-1, keepdims=True) + 1e-5) * ng_ref[...]
    y = y * (1.0 - lam_init)
    o_ref[0] = (y * _silu(z_ref[0])).astype(o_ref.dtype)


def _attn_corr_table():
    kk = jnp.arange(ATT_BLOCK)[:, None]
    qq = jnp.arange(ATT_BLOCK)[None, :]
    visible = (kk // CHUNK) <= (qq // CHUNK)
    return jnp.where(visible, -2.0 * jnp.maximum(kk - qq, 0).astype(F32), NEG_BIG)


def _attn_call(proj, slopes, dl, norm_g, lam_init):
    bsz, s, _ = proj.shape
    T = ATT_BLOCK
    hw = 2 * DIFF_HEAD_DIM
    blk = lambda off: off // hw
    const2 = lambda b, h, i: (0, 0)
    return pl.pallas_call(
        functools.partial(_attn_kernel, lam_init),
        grid=(bsz, DIFF_HEADS, s // T),
        in_specs=[pl.BlockSpec((1, T, hw), lambda b, h, i: (b, i, blk(P_OFF_QB) + h)),
                  pl.BlockSpec((1, s, hw), lambda b, h, i: (b, 0, blk(P_OFF_KB) + h)),
                  pl.BlockSpec((1, s, hw), lambda b, h, i: (b, 0, blk(P_OFF_VB) + h)),
                  pl.BlockSpec((1, T, hw), lambda b, h, i: (b, i, blk(P_OFF_ZB) + h)),
                  pl.BlockSpec((1, 1, LANES), lambda b, h, i: (h, 0, 0)),
                  pl.BlockSpec((T, T), const2),
                  pl.BlockSpec(dl.shape, const2),
                  pl.BlockSpec(norm_g.shape, const2)],
        out_specs=pl.BlockSpec((1, T, hw), lambda b, h, i: (b, i, h)),
        out_shape=jax.ShapeDtypeStruct((bsz, s, DIFF_WIDTH), BF16),
        scratch_shapes=[pltpu.VMEM((s, ATT_AUG), BF16),
                        pltpu.VMEM((ATT_ACC_ROWS, s), BF16),
                        pltpu.VMEM((ATT_ACC_ROWS, T), F32),
                        pltpu.VMEM((ATT_ACC_ROWS, T), F32)],
        compiler_params=_cparams(("parallel", "parallel", "arbitrary")),
        name="diff_attention",
    )(proj, proj, proj, proj, slopes, _attn_corr_table(), dl, norm_g)


def _mlstm_kernel(q_ref, k_ref, v_ref, o_ref_in, z_ref, ps_ref, gb_ref, out_ref, c_state, m_state):
    L = SCAN_CHUNK
    D = MLSTM_HEAD_DIM
    c = pl.program_id(1)

    @pl.when(c == 0)
    def _():
        c_state[...] = jnp.zeros(c_state.shape, F32)
        m_state[...] = jnp.zeros(m_state.shape, F32)

    ones_col = jnp.where(lax.broadcasted_iota(jnp.int32, (L, D), 1) == 0, 1.0, 0.0)

    rows = lax.broadcasted_iota(jnp.int32, (L, L), 0)
    cols = lax.broadcasted_iota(jnp.int32, (L, L), 1)
    causal = rows >= cols
    tril = jnp.where(causal, 1.0, 0.0).astype(BF16)

    raw = ps_ref[0] + gb_ref[...]
    logf = -_softplus(-raw)
    bcum = _cumsum_rows(logf, tril)
    bcum_t = bcum.T
    raw_t = raw.T

    for h in range(MLSTM_HEADS):
        li, lf = LANE_I + h, LANE_F + h
        sl = slice(h * D, (h + 1) * D)
        q = q_ref[0, :, sl]
        ks = k_ref[0, :, sl] * (D ** -0.5)
        v_ext = jnp.concatenate([v_ref[0, :, sl], ones_col], axis=1)
        qb, kb = q.astype(BF16), ks.astype(BF16)

        b_col = bcum[:, lf:lf + 1]
        b_row = bcum_t[lf:lf + 1, :]
        ig_col = raw[:, li:li + 1]
        ig_row = raw_t[li:li + 1, :]
        btot = bcum[L - 1:L, lf:lf + 1]
        m_prev = m_state[h:h + 1, 0:1]
        c_prev = c_state[h]

        dmat = jnp.where(causal, b_col - b_row + ig_row, NEG_BIG)
        m_inter = b_col + m_prev
        m_row = jnp.maximum(m_inter, jnp.max(dmat, axis=-1, keepdims=True))
        wts = jnp.exp(dmat - m_row)
        sw = wts * _dot_nt(qb, kb)
        a_inter = jnp.exp(m_inter - m_row)
        nd = _dot(sw.astype(BF16), v_ext.astype(BF16)) + _dot(qb, c_prev.astype(BF16)) * a_inter
        hout = nd[:, :D] / jnp.maximum(jnp.abs(nd[:, D:D + 1]), jnp.exp(-m_row))
        gate = jax.nn.sigmoid(o_ref_in[0, :, sl]) * _silu(z_ref[0, :, sl])
        out_ref[0, :, sl] = (gate * hout).astype(out_ref.dtype)

        w_end = btot - b_col + ig_col
        m_loc = jnp.max(w_end, axis=0, keepdims=True)
        m_new = jnp.maximum(btot + m_prev, m_loc)
        a_old = jnp.exp(btot + m_prev - m_new)
        e_end = jnp.exp(w_end - m_new)
        c_state[h] = a_old * c_prev + _dot_tn(kb, (v_ext * e_end).astype(BF16))
        m_state[h:h + 1, :] = jnp.broadcast_to(m_new, (1, LANES))


def _mlstm_call(proj, ps, gate_bias):
    bsz, s, _ = proj.shape
    L = SCAN_CHUNK
    W = MLSTM_WIDTH
    blk0 = P_OFF_C // W
    return pl.pallas_call(
        _mlstm_kernel,
        grid=(bsz, s // L),
        in_specs=[pl.BlockSpec((1, L, W), lambda b, c: (b, c, blk0)),
                  pl.BlockSpec((1, L, W), lambda b, c: (b, c, blk0 + 1)),
                  pl.BlockSpec((1, L, W), lambda b, c: (b, c, blk0 + 2)),
                  pl.BlockSpec((1, L, W), lambda b, c: (b, c, blk0 + 3)),
                  pl.BlockSpec((1, L, W), lambda b, c: (b, c, blk0 + 4)),
                  pl.BlockSpec((1, L, LANES), lambda b, c: (b, c, 0)),
                  pl.BlockSpec(gate_bias.shape, lambda b, c: (0, 0))],
        out_specs=pl.BlockSpec((1, L, W), lambda b, c: (b, c, 0)),
        out_shape=jax.ShapeDtypeStruct((bsz, s, W), BF16),
        scratch_shapes=[pltpu.VMEM((MLSTM_HEADS, MLSTM_HEAD_DIM, 2 * MLSTM_HEAD_DIM), F32),
                        pltpu.VMEM((MLSTM_HEADS, LANES), F32)],
        compiler_params=_cparams(("parallel", "arbitrary")),
        name="mlstm_scan",
    )(proj, proj, proj, proj, proj, ps, gate_bias)


def _merge_kernel(x_ref, oa_ref, ob_ref, oc_ref, wga_ref, wgb_ref, wgc_ref, wa_ref, wb_ref, wc_ref, o_ref):
    x = x_ref[...]
    acc = jax.nn.sigmoid(_dot_nt(x, wga_ref[...])) * _dot(oa_ref[...], wa_ref[...])
    acc = acc + jax.nn.sigmoid(_dot_nt(x, wgb_ref[...])) * _dot(ob_ref[...], wb_ref[...])
    acc = acc + jax.nn.sigmoid(_dot_nt(x, wgc_ref[...])) * _dot(oc_ref[...], wc_ref[...])
    o_ref[...] = acc.astype(o_ref.dtype)


def _merge_call(xb, oa, ob, oc, w_all, w_br, li, tm, tn):
    m = xb.shape[0]
    row = lambda i, j: (i, 0)
    gate_spec = lambda br: pl.BlockSpec(
        (None, tn, D_MODEL), lambda i, j: (li, (W_OFF_GATES + br * D_MODEL) // tn + j, 0))
    return pl.pallas_call(
        _merge_kernel,
        grid=(m // tm, D_MODEL // tn),
        in_specs=[pl.BlockSpec((tm, D_MODEL), row),
                  pl.BlockSpec((tm, SSM_WIDTH), row, pipeline_mode=pl.Buffered(1)),
                  pl.BlockSpec((tm, DIFF_WIDTH), row, pipeline_mode=pl.Buffered(1)),
                  pl.BlockSpec((tm, MLSTM_WIDTH), row, pipeline_mode=pl.Buffered(1)),
                  gate_spec(0), gate_spec(1), gate_spec(2),
                  pl.BlockSpec((None, SSM_WIDTH, tn), lambda i, j: (li, 0, j)),
                  pl.BlockSpec((None, DIFF_WIDTH, tn), lambda i, j: (li, SSM_WIDTH // DIFF_WIDTH, j)),
                  pl.BlockSpec((None, MLSTM_WIDTH, tn),
                               lambda i, j: (li, (SSM_WIDTH + DIFF_WIDTH) // MLSTM_WIDTH, j))],
        out_specs=pl.BlockSpec((tm, tn), lambda i, j: (i, j)),
        out_shape=jax.ShapeDtypeStruct((m, D_MODEL), BF16),
        compiler_params=_cparams(("parallel", "arbitrary")),
        name="gated_merge",
    )(xb, oa, ob, oc, w_all, w_all, w_all, w_br, w_br, w_br)


def _final_kernel(nj, tn, mg_ref, wo_ref, wg_ref, x_ref, p_ref, wple_ref, lng_ref, lnb_ref, pg_ref,
                  o_ref, ob_ref, h_scr, hb_scr, st_scr):
    j = pl.program_id(1)
    tm = h_scr.shape[0]

    @pl.when(j == 0)
    def _():
        st_scr[...] = jnp.zeros(st_scr.shape, F32)

    @pl.when(j < nj)
    def _():
        off = pl.multiple_of(j * tn, tn)
        pre = DEEPNORM_ALPHA * x_ref[...] + _dot(mg_ref[...], wo_ref[...])
        h_scr[:, pl.ds(off, tn)] = pre
        st_scr[:, 0:LANES] += jnp.sum(pre, axis=-1, keepdims=True)

    @pl.when(j == nj)
    def _():
        def norm_rows(r, carry):
            rows = pl.ds(pl.multiple_of(r * NORM_ROWS, NORM_ROWS), NORM_ROWS)
            cen = h_scr[rows, :] - st_scr[rows, 0:1] * (1.0 / D_MODEL)
            var = jnp.mean(cen * cen, axis=-1, keepdims=True)
            h = cen * lax.rsqrt(var + 1e-5) * lng_ref[...] + lnb_ref[...]
            h_scr[rows, :] = h
            hb_scr[rows, :] = h.astype(BF16)
            e = _dot(p_ref[rows, :], wple_ref[...])
            st_scr[rows, LANES:2 * LANES] = jnp.broadcast_to(
                jnp.sum(e * e, axis=-1, keepdims=True), (NORM_ROWS, LANES))
            return carry

        lax.fori_loop(0, tm // NORM_ROWS, norm_rows, 0)

    @pl.when(j >= nj)
    def _():
        off = pl.multiple_of((j - nj) * tn, tn)
        gate = jax.nn.sigmoid(_dot(hb_scr[...], wg_ref[...]))
        rs = lax.rsqrt(st_scr[:, LANES:LANES + 1] * (1.0 / D_MODEL) + 1e-5)
        e = _dot(p_ref[...], wple_ref[:, pl.ds(off, tn)]) * rs * pg_ref[:, pl.ds(off, tn)]
        out = h_scr[:, pl.ds(off, tn)] + gate * e
        o_ref[...] = out
        ob_ref[...] = out.astype(BF16)


def _final_call(merged, w_out, w_pg, x, pb, w_ple, ln_g, ln_b, ple_g, li, tm, tn):
    m = merged.shape[0]
    nj = D_MODEL // tn
    full = lambda i, j: (0, 0)
    out_idx = lambda i, j: (i, jnp.maximum(j - nj, 0))
    return pl.pallas_call(
        functools.partial(_final_kernel, nj, tn),
        grid=(m // tm, 2 * nj),
        in_specs=[pl.BlockSpec((tm, D_MODEL), lambda i, j: (i, 0), pipeline_mode=pl.Buffered(1)),
                  pl.BlockSpec((None, D_MODEL, tn), lambda i, j: (li, 0, jnp.minimum(j, nj - 1))),
                  pl.BlockSpec((None, D_MODEL, tn), lambda i, j: (li, 0, jnp.maximum(j - nj, 0))),
                  pl.BlockSpec((tm, tn), lambda i, j: (i, jnp.minimum(j, nj - 1))),
                  pl.BlockSpec((tm, PLE_DIM), lambda i, j: (i, 0)),
                  pl.BlockSpec((None,) + w_ple.shape[1:], lambda i, j: (li, 0, 0), pipeline_mode=pl.Buffered(1)),
                  pl.BlockSpec(ln_g.shape, full),
                  pl.BlockSpec(ln_b.shape, full),
                  pl.BlockSpec(ple_g.shape, full)],
        out_specs=[pl.BlockSpec((tm, tn), out_idx), pl.BlockSpec((tm, tn), out_idx)],
        out_shape=[jax.ShapeDtypeStruct((m, D_MODEL), F32), jax.ShapeDtypeStruct((m, D_MODEL), BF16)],
        scratch_shapes=[pltpu.VMEM((tm, D_MODEL), F32),
                        pltpu.VMEM((tm, D_MODEL), BF16),
                        pltpu.VMEM((tm, 2 * LANES), F32)],
        compiler_params=_cparams(("parallel", "arbitrary")),
        name="outproj_deepnorm_ple",
    )(merged, w_out, w_pg, x, pb, w_ple, ln_g, ln_b, ple_g)


def _row_tile(m, pref):
    return pref if m % pref == 0 else m


def _relayout_kernel(a_ref, dt_ref, if_ref, o_ref, os_ref):
    o_ref[...] = a_ref[0].astype(BF16)

    @pl.when(pl.program_id(1) == 0)
    def _():
        os_ref[LANE_DT:LANE_I, :] = dt_ref[0].astype(BF16)
        os_ref[LANE_I:LANE_F + MLSTM_HEADS, :] = if_ref[0].astype(BF16)
        os_ref[LANE_F + MLSTM_HEADS:, :] = jnp.zeros((LANES - LANE_F - MLSTM_HEADS, os_ref.shape[1]), BF16)


def _relayout_call(w_t, rows):
    depth, _, d = w_t.shape

    def src_row(j):
        r = j * rows
        shift = jnp.where(r < P_OFF_QB, 0, jnp.where(r < W_OFF_GATES, (OFF_QB - P_OFF_QB) // F32_SUBLANES,
                                                     (OFF_GATES - W_OFF_GATES) // F32_SUBLANES))
        return (j * (rows // F32_SUBLANES) + shift) * F32_SUBLANES

    return pl.pallas_call(
        _relayout_kernel,
        grid=(depth, W_OFF_SMALL // rows),
        in_specs=[pl.BlockSpec((pl.Element(1), pl.Element(rows), pl.Element(d)), lambda l, j: (l, src_row(j), 0)),
                  pl.BlockSpec((pl.Element(1), pl.Element(SSM_HEADS), pl.Element(d)), lambda l, j: (l, OFF_DT, 0)),
                  pl.BlockSpec((pl.Element(1), pl.Element(2 * MLSTM_HEADS), pl.Element(d)),
                               lambda l, j: (l, OFF_IC, 0))],
        out_specs=[pl.BlockSpec((None, rows, d), lambda l, j: (l, j, 0)),
                   pl.BlockSpec((None, LANES, d), lambda l, j: (l, 0, 0))],
        out_shape=[jax.ShapeDtypeStruct((depth, W_OFF_SMALL, d), BF16),
                   jax.ShapeDtypeStruct((depth, LANES, d), BF16)],
        compiler_params=_cparams(("parallel", "arbitrary")),
        name="weight_relayout",
    )(w_t, w_t, w_t)


def _prep_weights(w_in, w_branch, w_out, w_ple, w_ple_gate):
    w_main, w_small = _relayout_call(jnp.swapaxes(w_in, 1, 2), 512)
    return (w_main, w_small, w_branch.astype(BF16), w_out.astype(BF16), w_ple.astype(BF16),
            w_ple_gate.astype(BF16))


def _layer(x, xb, p_i, li, weights, conv_w, conv_b, dt_bias, a_log, d_skip, ssm_norm_g,
           diff_lambda, diff_norm_g, mlstm_gate_b, ln_g, ln_b, ple_norm_g):
    w_all, w_small, w_br, w_out, w_ple, w_pg = weights
    bsz, s, _ = x.shape
    t = bsz * s
    x2 = x.reshape(t, D_MODEL)
    xb2 = xb.reshape(t, D_MODEL)

    proj, ps = _proj_call(xb2, w_all, w_small, li, _row_tile(t, 1024), 1024)
    proj = proj.reshape(bsz, s, PROJ_WIDTH)
    ps = ps.reshape(bsz, s, LANES)

    pad = lambda v, lane0: jnp.zeros((1, LANES), F32).at[0, lane0:lane0 + v.shape[0]].set(v.astype(F32))
    out_a = _ssd_call(proj, ps, conv_w, conv_b.reshape(1, -1), pad(dt_bias, LANE_DT), pad(a_log, LANE_DT),
                      jnp.repeat(d_skip.astype(F32), SSM_HEAD_DIM).reshape(1, -1), ssm_norm_g.reshape(1, -1))

    lam_init = 0.8 - 0.6 * math.exp(-0.3 * li)
    slopes = 2.0 ** (-8.0 * jnp.arange(1, DIFF_HEADS + 1, dtype=F32) / DIFF_HEADS)
    slopes = jnp.broadcast_to(slopes[:, None, None], (DIFF_HEADS, 1, LANES))
    out_b = _attn_call(proj, slopes, diff_lambda, diff_norm_g.reshape(1, -1), lam_init)

    gate_bias = pad(mlstm_gate_b[0], LANE_I) + pad(mlstm_gate_b[1], LANE_F)
    out_c = _mlstm_call(proj, ps, gate_bias)

    tm = _row_tile(t, 1024)
    merged = _merge_call(xb2, out_a.reshape(t, -1), out_b.reshape(t, -1), out_c.reshape(t, -1),
                         w_all, w_br, li, tm, 256)
    out, out_b16 = _final_call(merged, w_out, w_pg, x2, p_i.reshape(t, PLE_DIM).astype(BF16), w_ple,
                               ln_g.reshape(1, -1), ln_b.reshape(1, -1), ple_norm_g.reshape(1, -1), li, tm, 256)
    return out.reshape(bsz, s, D_MODEL), out_b16.reshape(bsz, s, D_MODEL)


def kernel(x, p, w_in, conv_w, conv_b, dt_bias, a_log, d_skip, ssm_norm_g, diff_lambda, diff_norm_g,
           mlstm_gate_b, w_branch, w_out, ln_g, ln_b, w_ple, ple_norm_g, w_ple_gate):
    weights = _prep_weights(w_in, w_branch, w_out, w_ple, w_ple_gate)
    h = x
    hb = x.astype(BF16)
    for li in range(w_in.shape[0]):
        h, hb = _layer(h, hb, p[li], li, weights, conv_w[li], conv_b[li], dt_bias[li], a_log[li],
                       d_skip[li], ssm_norm_g[li], diff_lambda[li], diff_norm_g[li],
                       mlstm_gate_b[li], ln_g[li], ln_b[li], ple_norm_g[li])
    return h
```

```python
import functools
import math

import jax
import jax.numpy as jnp
from jax import lax
from jax.experimental import pallas as pl
from jax.experimental.pallas import tpu as pltpu

F32 = jnp.float32
BF16 = jnp.bfloat16

V7X_VMEM_LIMIT_BYTES = 56 * 1024 * 1024
LANES = 128
F32_SUBLANES = 8

D_MODEL = 4096
CHUNK = 64
SSM_WIDTH = 2048
SSM_HEAD_DIM = 64
SSM_HEADS = 32
SSM_GROUPS = 8
SSM_STATE = 128
SSM_XBC = SSM_WIDTH + 2 * SSM_GROUPS * SSM_STATE
GROUP_WIDTH = SSM_WIDTH // SSM_GROUPS
HEADS_PER_GROUP = SSM_HEADS // SSM_GROUPS
DIFF_WIDTH = 1024
DIFF_HEAD_DIM = 64
DIFF_HEADS = 8
MLSTM_WIDTH = 1024
MLSTM_HEAD_DIM = 128
MLSTM_HEADS = 8
N_BRANCH = 3
PLE_DIM = 256
DEPTH = 2
DEEPNORM_ALPHA = (2.0 * DEPTH) ** 0.25

OFF_XBC = 0
OFF_ZA = OFF_XBC + SSM_XBC
OFF_DT = OFF_ZA + SSM_WIDTH
OFF_QB = OFF_DT + SSM_HEADS
OFF_ZB = OFF_QB + 3 * DIFF_WIDTH
OFF_QC = OFF_ZB + DIFF_WIDTH
OFF_IC = OFF_QC + 5 * MLSTM_WIDTH
OFF_FC = OFF_IC + MLSTM_HEADS
OFF_GATES = OFF_FC + MLSTM_HEADS

P_OFF_XBC = 0
P_OFF_ZA = P_OFF_XBC + SSM_XBC
P_OFF_QB = P_OFF_ZA + SSM_WIDTH
P_OFF_KB = P_OFF_QB + DIFF_WIDTH
P_OFF_VB = P_OFF_KB + DIFF_WIDTH
P_OFF_ZB = P_OFF_VB + DIFF_WIDTH
P_OFF_C = P_OFF_ZB + DIFF_WIDTH
PROJ_WIDTH = P_OFF_C + 5 * MLSTM_WIDTH
W_OFF_GATES = PROJ_WIDTH
W_OFF_SMALL = W_OFF_GATES + N_BRANCH * D_MODEL

LANE_DT = 0
LANE_I = SSM_HEADS
LANE_F = SSM_HEADS + MLSTM_HEADS

SCAN_CHUNK = 256
ATT_BLOCK = 512
ATT_GROUP = 4
ATT_STAGE = 1
ATT_AUG = 2 * LANES
ATT_ACC_ROWS = LANES + 16
POS_SPLIT = 256
NORM_ROWS = 128
NEG_BIG = -1e30
LOG2E = 1.4426950408889634


def _cparams(sem):
    return pltpu.CompilerParams(dimension_semantics=sem, vmem_limit_bytes=V7X_VMEM_LIMIT_BYTES)


def _silu(x):
    return x * jax.nn.sigmoid(x)


def _softplus(x):
    return jnp.maximum(x, 0.0) + jnp.log1p(jnp.exp(-jnp.abs(x)))


def _dot(a, b):
    return jnp.dot(a, b, preferred_element_type=F32)


def _dot_nt(a, b):
    return lax.dot_general(a, b, (((1,), (1,)), ((), ())), preferred_element_type=F32)


def _dot_tn(a, b):
    return lax.dot_general(a, b, (((0,), (0,)), ((), ())), preferred_element_type=F32)


def _cumsum_rows(x, tril_bf16):
    hi = x.astype(BF16)
    r1 = x - hi.astype(F32)
    mid = r1.astype(BF16)
    lo = (r1 - mid.astype(F32)).astype(BF16)
    return _dot(tril_bf16, hi) + _dot(tril_bf16, mid) + _dot(tril_bf16, lo)


def _proj_kernel(x_ref, w_ref, ws_ref, o_ref, os_ref):
    o_ref[...] = _dot_nt(x_ref[...], w_ref[...])

    @pl.when(pl.program_id(1) == 0)
    def _():
        os_ref[...] = _dot_nt(x_ref[...], ws_ref[...])


def _proj_call(xb, w_main, w_small, li, tm, tn):
    m, k = xb.shape
    return pl.pallas_call(
        _proj_kernel,
        grid=(m // tm, PROJ_WIDTH // tn),
        in_specs=[pl.BlockSpec((tm, k), lambda i, j: (i, 0)),
                  pl.BlockSpec((None, tn, k), lambda i, j: (li, j, 0)),
                  pl.BlockSpec((None, LANES, k), lambda i, j: (li, 0, 0))],
        out_specs=[pl.BlockSpec((tm, tn), lambda i, j: (i, j)),
                   pl.BlockSpec((tm, LANES), lambda i, j: (i, 0))],
        out_shape=[jax.ShapeDtypeStruct((m, PROJ_WIDTH), F32), jax.ShapeDtypeStruct((m, LANES), F32)],
        compiler_params=_cparams(("parallel", "arbitrary")),
        name="in_proj",
    )(xb, w_main, w_small)


def _ssd_kernel(xbc_ref, z_ref, ps_ref, cw_ref, cb_ref, dtb_ref, alog_ref, dskip_ref, ng_ref,
                o_ref, xpad, state, ybuf):
    L = SCAN_CHUNK
    c = pl.program_id(1)

    @pl.when(c == 0)
    def _():
        xpad[0:8, :] = jnp.zeros((8, SSM_XBC), F32)
        state[...] = jnp.zeros(state.shape, F32)

    @pl.when(c > 0)
    def _():
        xpad[0:8, :] = xpad[L:L + 8, :]

    xpad[8:8 + L, :] = xbc_ref[0]

    def conv_silu(c0, width):
        acc = cb_ref[:, c0:c0 + width] + cw_ref[3:4, c0:c0 + width] * xpad[8:8 + L, c0:c0 + width]
        acc = acc + cw_ref[2:3, c0:c0 + width] * xpad[7:7 + L, c0:c0 + width]
        acc = acc + cw_ref[1:2, c0:c0 + width] * xpad[6:6 + L, c0:c0 + width]
        acc = acc + cw_ref[0:1, c0:c0 + width] * xpad[5:5 + L, c0:c0 + width]
        return _silu(acc)

    rows = lax.broadcasted_iota(jnp.int32, (L, L), 0)
    cols = lax.broadcasted_iota(jnp.int32, (L, L), 1)
    causal = rows >= cols
    tril = jnp.where(causal, 1.0, 0.0).astype(BF16)

    dt = _softplus(ps_ref[0] + dtb_ref[...])
    la = dt * (-jnp.exp(alog_ref[...]))
    lc = _cumsum_rows(la, tril)
    lc_t = lc.T

    head_of_lane = lax.broadcasted_iota(jnp.int32, (L, GROUP_WIDTH), 1) // SSM_HEAD_DIM

    def expand(vals, hol):
        out = vals[HEADS_PER_GROUP - 1]
        for r in range(HEADS_PER_GROUP - 2, -1, -1):
            out = jnp.where(hol == r, vals[r], out)
        return out

    ssq = jnp.zeros((L, 1), F32)
    for g in range(SSM_GROUPS):
        h0 = g * HEADS_PER_GROUP
        xg = conv_silu(g * GROUP_WIDTH, GROUP_WIDTH)
        bg = conv_silu(SSM_WIDTH + g * SSM_STATE, SSM_STATE).astype(BF16)
        cg = conv_silu(SSM_WIDTH + SSM_GROUPS * SSM_STATE + g * SSM_STATE, SSM_STATE).astype(BF16)

        dt4 = expand([dt[:, h0 + r:h0 + r + 1] for r in range(HEADS_PER_GROUP)], head_of_lane)
        lc4 = expand([lc[:, h0 + r:h0 + r + 1] for r in range(HEADS_PER_GROUP)], head_of_lane)
        lc_end4 = expand([lc[L - 1:L, h0 + r:h0 + r + 1] for r in range(HEADS_PER_GROUP)],
                         head_of_lane[0:1, :])
        xdt = xg * dt4

        cb = _dot_nt(cg, bg)
        y = jnp.zeros((L, GROUP_WIDTH), F32)
        for r in range(HEADS_PER_GROUP):
            h = h0 + r
            seg = lc[:, h:h + 1] - lc_t[h:h + 1, :]
            decay = jnp.where(causal, jnp.exp(seg), 0.0)
            m_r = (cb * decay).astype(BF16)
            x_r = jnp.where(head_of_lane == r, xdt, 0.0).astype(BF16)
            y = y + _dot(m_r, x_r)

        prev = state[g]
        y = y + _dot(cg, prev.astype(BF16)) * jnp.exp(lc4)
        xs = (xdt * jnp.exp(lc_end4 - lc4)).astype(BF16)
        state[g] = prev * jnp.exp(lc_end4) + _dot_tn(bg, xs)

        y = y + dskip_ref[:, g * GROUP_WIDTH:(g + 1) * GROUP_WIDTH] * xg
        yz = y * _silu(z_ref[0, :, g * GROUP_WIDTH:(g + 1) * GROUP_WIDTH])
        ybuf[:, g * GROUP_WIDTH:(g + 1) * GROUP_WIDTH] = yz
        ssq = ssq + jnp.sum(yz * yz, axis=-1, keepdims=True)

    rs = lax.rsqrt(ssq * (1.0 / SSM_WIDTH) + 1e-5)
    o_ref[0] = (ybuf[...] * rs * ng_ref[...]).astype(o_ref.dtype)


def _ssd_call(pa, ps, conv_w, conv_b, dtb, alog, dskip, norm_g):
    bsz, s, _ = pa.shape
    L = SCAN_CHUNK
    full = lambda b, c: (0, 0)
    return pl.pallas_call(
        _ssd_kernel,
        grid=(bsz, s // L),
        in_specs=[pl.BlockSpec((1, L, SSM_XBC), lambda b, c: (b, c, 0)),
                  pl.BlockSpec((1, L, SSM_WIDTH), lambda b, c: (b, c, SSM_XBC // SSM_WIDTH)),
                  pl.BlockSpec((1, L, LANES), lambda b, c: (b, c, 0)),
                  pl.BlockSpec(conv_w.shape, full),
                  pl.BlockSpec(conv_b.shape, full),
                  pl.BlockSpec(dtb.shape, full),
                  pl.BlockSpec(alog.shape, full),
                  pl.BlockSpec(dskip.shape, full),
                  pl.BlockSpec(norm_g.shape, full)],
        out_specs=pl.BlockSpec((1, L, SSM_WIDTH), lambda b, c: (b, c, 0)),
        out_shape=jax.ShapeDtypeStruct((bsz, s, SSM_WIDTH), BF16),
        scratch_shapes=[pltpu.VMEM((L + 8, SSM_XBC), F32),
                        pltpu.VMEM((SSM_GROUPS, SSM_STATE, GROUP_WIDTH), F32),
                        pltpu.VMEM((L, SSM_WIDTH), F32)],
        compiler_params=_cparams(("parallel", "arbitrary")),
        name="ssd_scan",
    )(pa, pa, ps, conv_w, conv_b, dtb, alog, dskip, norm_g)


def _attn_kernel(lam_init, q_ref, k_ref, v_ref, z_ref, slope_ref, corr_ref, dl_ref, ng_ref, o_ref,
                 k_aug, vt_aug, acc1, acc2):
    T = ATT_BLOCK
    D2 = 2 * DIFF_HEAD_DIM
    qi = pl.program_id(2)
    n_kv = k_ref.shape[1] // T

    @pl.when(qi == 0)
    def _():
        lane = lax.broadcasted_iota(jnp.int32, (T, LANES), 1)
        row = lax.broadcasted_iota(jnp.int32, (T, LANES), 0)
        for c in range(n_kv):
            rows = slice(c * T, (c + 1) * T)
            pos = row + c * T
            pos_lo = jnp.bitwise_and(pos, POS_SPLIT - 1)
            pos_hi = pos - pos_lo
            pos_cols = jnp.where(lane < 3, pos_lo, jnp.where(lane < 6, pos_hi, 0))
            k_aug[rows, 0:D2] = k_ref[0, rows, :].astype(BF16)
            k_aug[rows, D2:ATT_AUG] = pos_cols.astype(F32).astype(BF16)
            vt_aug[0:D2, rows] = v_ref[0, rows, :].T.astype(BF16)
        ones_row = lax.broadcasted_iota(jnp.int32, (ATT_ACC_ROWS - D2, k_ref.shape[1]), 0) == 0
        vt_aug[D2:ATT_ACC_ROWS, :] = jnp.where(ones_row, 1.0, 0.0).astype(BF16)

    lane = lax.broadcasted_iota(jnp.int32, (T, D2), 1)
    qs = q_ref[0] * (DIFF_HEAD_DIM ** -0.5 * LOG2E)
    sl2 = slope_ref[0][:, 0:1] * LOG2E
    sl_hi = sl2.astype(BF16).astype(F32)
    sl_mid = (sl2 - sl_hi).astype(BF16).astype(F32)
    sl_lo = sl2 - sl_hi - sl_mid
    lane1 = lax.broadcasted_iota(jnp.int32, (1, LANES), 1)
    ext = jnp.where((lane1 == 0) | (lane1 == 3), sl_hi,
                    jnp.where((lane1 == 1) | (lane1 == 4), sl_mid,
                              jnp.where((lane1 == 2) | (lane1 == 5), sl_lo, 0.0)))
    ext = jnp.broadcast_to(ext, (T, LANES)).astype(BF16)
    qa1 = jnp.concatenate([jnp.where(lane < DIFF_HEAD_DIM, qs, 0.0).astype(BF16), ext], axis=1)
    qa2 = jnp.concatenate([jnp.where(lane >= DIFF_HEAD_DIM, qs, 0.0).astype(BF16), ext], axis=1)

    acc1[...] = jnp.zeros(acc1.shape, F32)
    acc2[...] = jnp.zeros(acc2.shape, F32)

    def block(first, n, ms, corr):
        qas, accs = (qa1, qa2), (acc1, acc2)
        starts = [pl.multiple_of((first + t) * T, T) for t in range(n)]
        stages = [list(range(i, min(i + ATT_STAGE, n))) for i in range(0, n, ATT_STAGE)]

        def score_step(t, m_run, out):
            kb = k_aug[pl.ds(starts[t], T), :]
            for a in range(2):
                s = _dot_nt(kb, qas[a])
                if corr is not None and t == n - 1:
                    s = s + corr
                out[a].append(s)
                m_run[a] = jnp.maximum(m_run[a], jnp.max(s, axis=0, keepdims=True))

        def value_step(t, idx, scores, m_stage, pv):
            vb = vt_aug[:, pl.ds(starts[t], T)]
            for a in range(2):
                d = _dot(vb, jnp.exp2(scores[a][idx] - m_stage[a]).astype(BF16))
                pv[a] = d if pv[a] is None else pv[a] + d

        m_prev = list(ms)
        pending = None
        for stage in stages + [None]:
            scores, m_run = [[], []], (list(pending[2]) if pending else list(ms))
            pv = [None, None]
            for idx in range(ATT_STAGE):
                if stage is not None and idx < len(stage):
                    score_step(stage[idx], m_run, scores)
                if pending is not None and idx < len(pending[0]):
                    value_step(pending[0][idx], idx, pending[1], pending[2], pv)
            if pending is not None:
                for a in range(2):
                    accs[a][...] = jnp.exp2(m_prev[a] - pending[2][a]) * accs[a][...] + pv[a]
                m_prev = list(pending[2])
            pending = (stage, scores, m_run) if stage is not None else None
        return tuple(m_prev)

    m0 = jnp.full((1, T), NEG_BIG, F32)
    ms = lax.fori_loop(0, qi // ATT_GROUP, lambda i, ms: block(ATT_GROUP * i, ATT_GROUP, ms, None), (m0, m0))
    corr = corr_ref[...] * sl2
    for left in range(ATT_GROUP):
        @pl.when(qi % ATT_GROUP == left)
        def _():
            block(qi - left, left + 1, ms, corr)

    dl = dl_ref[...]
    lam = (jnp.exp(jnp.sum(dl[0:1] * dl[1:2], axis=-1, keepdims=True))
           - jnp.exp(jnp.sum(dl[2:3] * dl[3:4], axis=-1, keepdims=True)) + lam_init)
    o_t = acc1[0:D2, :] / acc1[D2:D2 + 1, :] - lam * (acc2[0:D2, :] / acc2[D2:D2 + 1, :])
    o = o_t.T
    y = o * lax.rsqrt(jnp.mean(o * o, axis=-1, keepdims=True) + 1e-5) * ng_ref[...]
    y = y * (1.0 - lam_init)
    o_ref[0] = (y * _silu(z_ref[0])).astype(o_ref.dtype)


def _attn_corr_table():
    kk = jnp.arange(ATT_BLOCK)[:, None]
    qq = jnp.arange(ATT_BLOCK)[None, :]
    visible = (kk // CHUNK) <= (qq // CHUNK)
    return jnp.where(visible, -2.0 * jnp.maximum(kk - qq, 0).astype(F32), NEG_BIG)


def _attn_call(proj, slopes, dl, norm_g, lam_init):
    bsz, s, _ = proj.shape
    T = ATT_BLOCK
    hw = 2 * DIFF_HEAD_DIM
    blk = lambda off: off // hw
    const2 = lambda b, h, i: (0, 0)
    return pl.pallas_call(
        functools.partial(_attn_kernel, lam_init),
        grid=(bsz, DIFF_HEADS, s // T),
        in_specs=[pl.BlockSpec((1, T, hw), lambda b, h, i: (b, i, blk(P_OFF_QB) + h)),
                  pl.BlockSpec((1, s, hw), lambda b, h, i: (b, 0, blk(P_OFF_KB) + h)),
                  pl.BlockSpec((1, s, hw), lambda b, h, i: (b, 0, blk(P_OFF_VB) + h)),
                  pl.BlockSpec((1, T, hw), lambda b, h, i: (b, i, blk(P_OFF_ZB) + h)),
                  pl.BlockSpec((1, 1, LANES), lambda b, h, i: (h, 0, 0)),
                  pl.BlockSpec((T, T), const2),
                  pl.BlockSpec(dl.shape, const2),
                  pl.BlockSpec(norm_g.shape, const2)],
        out_specs=pl.BlockSpec((1, T, hw), lambda b, h, i: (b, i, h)),
        out_shape=jax.ShapeDtypeStruct((bsz, s, DIFF_WIDTH), BF16),
        scratch_shapes=[pltpu.VMEM((s, ATT_AUG), BF16),
                        pltpu.VMEM((ATT_ACC_ROWS, s), BF16),
                        pltpu.VMEM((ATT_ACC_ROWS, T), F32),
                        pltpu.VMEM((ATT_ACC_ROWS, T), F32)],
        compiler_params=_cparams(("parallel", "parallel", "arbitrary")),
        name="diff_attention",
    )(proj, proj, proj, proj, slopes, _attn_corr_table(), dl, norm_g)


def _mlstm_kernel(q_ref, k_ref, v_ref, o_ref_in, z_ref, ps_ref, gb_ref, out_ref, c_state, m_state):
    L = SCAN_CHUNK
    D = MLSTM_HEAD_DIM
    c = pl.program_id(1)

    @pl.when(c == 0)
    def _():
        c_state[...] = jnp.zeros(c_state.shape, F32)
        m_state[...] = jnp.zeros(m_state.shape, F32)

    ones_col = jnp.where(lax.broadcasted_iota(jnp.int32, (L, D), 1) == 0, 1.0, 0.0)
    rep = lambda col: jnp.broadcast_to(col, (col.shape[0], LANES))
    twice = lambda x: jnp.concatenate([x, x], axis=1)

    rows = lax.broadcasted_iota(jnp.int32, (L, L), 0)
    cols = lax.broadcasted_iota(jnp.int32, (L, L), 1)
    causal = rows >= cols
    tril = jnp.where(causal, 1.0, 0.0).astype(BF16)

    raw = ps_ref[0] + gb_ref[...]
    logf = -_softplus(-raw)
    bcum = _cumsum_rows(logf, tril)
    bcum_t = bcum.T
    raw_t = raw.T

    for h in range(MLSTM_HEADS):
        li, lf = LANE_I + h, LANE_F + h
        sl = slice(h * D, (h + 1) * D)
        q = q_ref[0, :, sl]
        ks = k_ref[0, :, sl] * (D ** -0.5)
        v_ext = jnp.concatenate([v_ref[0, :, sl], ones_col], axis=1)
        qb, kb = q.astype(BF16), ks.astype(BF16)

        b_rep = rep(bcum[:, lf:lf + 1])
        ig_rep = rep(raw[:, li:li + 1])
        b_row = bcum_t[lf:lf + 1, :]
        ig_row = raw_t[li:li + 1, :]
        btot = b_rep[L - 1:L, :]
        m_prev = m_state[h:h + 1, :]
        c_prev = c_state[h]

        dmat = jnp.where(causal, twice(b_rep) - b_row + ig_row, NEG_BIG)
        m_inter = b_rep + m_prev
        m_row = jnp.maximum(m_inter, rep(jnp.max(dmat, axis=-1, keepdims=True)))
        wts = jnp.exp(dmat - twice(m_row))
        sw = wts * _dot_nt(qb, kb)
        a_inter = jnp.exp(m_inter - m_row)
        nd = _dot(sw.astype(BF16), v_ext.astype(BF16)) + _dot(qb, c_prev.astype(BF16)) * twice(a_inter)
        hout = nd[:, :D] / jnp.maximum(jnp.abs(rep(nd[:, D:D + 1])), jnp.exp(-m_row))
        gate = jax.nn.sigmoid(o_ref_in[0, :, sl]) * _silu(z_ref[0, :, sl])
        out_ref[0, :, sl] = (gate * hout).astype(out_ref.dtype)

        w_end = btot - b_rep + ig_rep
        m_loc = jnp.max(w_end, axis=0, keepdims=True)
        m_new = jnp.maximum(btot + m_prev, m_loc)
        a_old = jnp.exp(btot + m_prev - m_new)
        e_end = jnp.exp(w_end - m_new)
        c_state[h] = twice(a_old) * c_prev + _dot_tn(kb, (v_ext * twice(e_end)).astype(BF16))
        m_state[h:h + 1, :] = m_new


def _mlstm_call(proj, ps, gate_bias):
    bsz, s, _ = proj.shape
    L = SCAN_CHUNK
    W = MLSTM_WIDTH
    blk0 = P_OFF_C // W
    return pl.pallas_call(
        _mlstm_kernel,
        grid=(bsz, s // L),
        in_specs=[pl.BlockSpec((1, L, W), lambda b, c: (b, c, blk0)),
                  pl.BlockSpec((1, L, W), lambda b, c: (b, c, blk0 + 1)),
                  pl.BlockSpec((1, L, W), lambda b, c: (b, c, blk0 + 2)),
                  pl.BlockSpec((1, L, W), lambda b, c: (b, c, blk0 + 3)),
                  pl.BlockSpec((1, L, W), lambda b, c: (b, c, blk0 + 4)),
                  pl.BlockSpec((1, L, LANES), lambda b, c: (b, c, 0)),
                  pl.BlockSpec(gate_bias.shape, lambda b, c: (0, 0))],
        out_specs=pl.BlockSpec((1, L, W), lambda b, c: (b, c, 0)),
        out_shape=jax.ShapeDtypeStruct((bsz, s, W), BF16),
        scratch_shapes=[pltpu.VMEM((MLSTM_HEADS, MLSTM_HEAD_DIM, 2 * MLSTM_HEAD_DIM), F32),
                        pltpu.VMEM((MLSTM_HEADS, LANES), F32)],
        compiler_params=_cparams(("parallel", "arbitrary")),
        name="mlstm_scan",
    )(proj, proj, proj, proj, proj, ps, gate_bias)


def _merge_kernel(x_ref, oa_ref, ob_ref, oc_ref, wga_ref, wgb_ref, wgc_ref, wa_ref, wb_ref, wc_ref, o_ref):
    x = x_ref[...]
    acc = jax.nn.sigmoid(_dot_nt(x, wga_ref[...])) * _dot(oa_ref[...], wa_ref[...])
    acc = acc + jax.nn.sigmoid(_dot_nt(x, wgb_ref[...])) * _dot(ob_ref[...], wb_ref[...])
    acc = acc + jax.nn.sigmoid(_dot_nt(x, wgc_ref[...])) * _dot(oc_ref[...], wc_ref[...])
    o_ref[...] = acc.astype(o_ref.dtype)


def _merge_call(xb, oa, ob, oc, w_all, w_br, li, tm, tn):
    m = xb.shape[0]
    row = lambda i, j: (i, 0)
    gate_spec = lambda br: pl.BlockSpec(
        (None, tn, D_MODEL), lambda i, j: (li, (W_OFF_GATES + br * D_MODEL) // tn + j, 0))
    return pl.pallas_call(
        _merge_kernel,
        grid=(m // tm, D_MODEL // tn),
        in_specs=[pl.BlockSpec((tm, D_MODEL), row),
                  pl.BlockSpec((tm, SSM_WIDTH), row, pipeline_mode=pl.Buffered(1)),
                  pl.BlockSpec((tm, DIFF_WIDTH), row, pipeline_mode=pl.Buffered(1)),
                  pl.BlockSpec((tm, MLSTM_WIDTH), row, pipeline_mode=pl.Buffered(1)),
                  gate_spec(0), gate_spec(1), gate_spec(2),
                  pl.BlockSpec((None, SSM_WIDTH, tn), lambda i, j: (li, 0, j)),
                  pl.BlockSpec((None, DIFF_WIDTH, tn), lambda i, j: (li, SSM_WIDTH // DIFF_WIDTH, j)),
                  pl.BlockSpec((None, MLSTM_WIDTH, tn),
                               lambda i, j: (li, (SSM_WIDTH + DIFF_WIDTH) // MLSTM_WIDTH, j))],
        out_specs=pl.BlockSpec((tm, tn), lambda i, j: (i, j)),
        out_shape=jax.ShapeDtypeStruct((m, D_MODEL), BF16),
        compiler_params=_cparams(("parallel", "arbitrary")),
        name="gated_merge",
    )(xb, oa, ob, oc, w_all, w_all, w_all, w_br, w_br, w_br)


def _final_kernel(nj, tn, mg_ref, wo_ref, wg_ref, x_ref, p_ref, wple_ref, lng_ref, lnb_ref, pg_ref,
                  o_ref, ob_ref, h_scr, hb_scr, st_scr):
    j = pl.program_id(1)
    tm = h_scr.shape[0]

    @pl.when(j == 0)
    def _():
        st_scr[...] = jnp.zeros(st_scr.shape, F32)

    @pl.when(j < nj)
    def _():
        off = pl.multiple_of(j * tn, tn)
        pre = DEEPNORM_ALPHA * x_ref[...] + _dot(mg_ref[...], wo_ref[...])
        h_scr[:, pl.ds(off, tn)] = pre
        st_scr[:, 0:LANES] += jnp.sum(pre, axis=-1, keepdims=True)

    @pl.when(j == nj)
    def _():
        def norm_rows(r, carry):
            rows = pl.ds(pl.multiple_of(r * NORM_ROWS, NORM_ROWS), NORM_ROWS)
            cen = h_scr[rows, :] - st_scr[rows, 0:1] * (1.0 / D_MODEL)
            var = jnp.mean(cen * cen, axis=-1, keepdims=True)
            h = cen * lax.rsqrt(var + 1e-5) * lng_ref[...] + lnb_ref[...]
            h_scr[rows, :] = h
            hb_scr[rows, :] = h.astype(BF16)
            e = _dot(p_ref[rows, :], wple_ref[...])
            st_scr[rows, LANES:2 * LANES] = jnp.broadcast_to(
                jnp.sum(e * e, axis=-1, keepdims=True), (NORM_ROWS, LANES))
            return carry

        lax.fori_loop(0, tm // NORM_ROWS, norm_rows, 0)

    @pl.when(j >= nj)
    def _():
        off = pl.multiple_of((j - nj) * tn, tn)
        gate = jax.nn.sigmoid(_dot(hb_scr[...], wg_ref[...]))
        rs = lax.rsqrt(st_scr[:, LANES:LANES + 1] * (1.0 / D_MODEL) + 1e-5)
        e = _dot(p_ref[...], wple_ref[:, pl.ds(off, tn)]) * rs * pg_ref[:, pl.ds(off, tn)]
        out = h_scr[:, pl.ds(off, tn)] + gate * e
        o_ref[...] = out
        ob_ref[...] = out.astype(BF16)


def _final_call(merged, w_out, w_pg, x, pb, w_ple, ln_g, ln_b, ple_g, li, tm, tn):
    m = merged.shape[0]
    nj = D_MODEL // tn
    full = lambda i, j: (0, 0)
    out_idx = lambda i, j: (i, jnp.maximum(j - nj, 0))
    return pl.pallas_call(
        functools.partial(_final_kernel, nj, tn),
        grid=(m // tm, 2 * nj),
        in_specs=[pl.BlockSpec((tm, D_MODEL), lambda i, j: (i, 0), pipeline_mode=pl.Buffered(1)),
                  pl.BlockSpec((None, D_MODEL, tn), lambda i, j: (li, 0, jnp.minimum(j, nj - 1))),
                  pl.BlockSpec((None, D_MODEL, tn), lambda i, j: (li, 0, jnp.maximum(j - nj, 0))),
                  pl.BlockSpec((tm, tn), lambda i, j: (i, jnp.minimum(j, nj - 1))),
                  pl.BlockSpec((tm, PLE_DIM), lambda i, j: (i, 0)),
                  pl.BlockSpec((None,) + w_ple.shape[1:], lambda i, j: (li, 0, 0), pipeline_mode=pl.Buffered(1)),
                  pl.BlockSpec(ln_g.shape, full),
                  pl.BlockSpec(ln_b.shape, full),
                  pl.BlockSpec(ple_g.shape, full)],
        out_specs=[pl.BlockSpec((tm, tn), out_idx), pl.BlockSpec((tm, tn), out_idx)],
        out_shape=[jax.ShapeDtypeStruct((m, D_MODEL), F32), jax.ShapeDtypeStruct((m, D_MODEL), BF16)],
        scratch_shapes=[pltpu.VMEM((tm, D_MODEL), F32),
                        pltpu.VMEM((tm, D_MODEL), BF16),
                        pltpu.VMEM((tm, 2 * LANES), F32)],
        compiler_params=_cparams(("parallel", "arbitrary")),
        name="outproj_deepnorm_ple",
    )(merged, w_out, w_pg, x, pb, w_ple, ln_g, ln_b, ple_g)


def _row_tile(m, pref):
    return pref if m % pref == 0 else m


def _relayout_kernel(a_ref, dt_ref, if_ref, o_ref, os_ref):
    o_ref[...] = a_ref[0].astype(BF16)

    @pl.when(pl.program_id(1) == 0)
    def _():
        os_ref[LANE_DT:LANE_I, :] = dt_ref[0].astype(BF16)
        os_ref[LANE_I:LANE_F + MLSTM_HEADS, :] = if_ref[0].astype(BF16)
        os_ref[LANE_F + MLSTM_HEADS:, :] = jnp.zeros((LANES - LANE_F - MLSTM_HEADS, os_ref.shape[1]), BF16)


def _relayout_call(w_t, rows):
    depth, _, d = w_t.shape

    def src_row(j):
        r = j * rows
        shift = jnp.where(r < P_OFF_QB, 0, jnp.where(r < W_OFF_GATES, (OFF_QB - P_OFF_QB) // F32_SUBLANES,
                                                     (OFF_GATES - W_OFF_GATES) // F32_SUBLANES))
        return (j * (rows // F32_SUBLANES) + shift) * F32_SUBLANES

    return pl.pallas_call(
        _relayout_kernel,
        grid=(depth, W_OFF_SMALL // rows),
        in_specs=[pl.BlockSpec((pl.Element(1), pl.Element(rows), pl.Element(d)), lambda l, j: (l, src_row(j), 0)),
                  pl.BlockSpec((pl.Element(1), pl.Element(SSM_HEADS), pl.Element(d)), lambda l, j: (l, OFF_DT, 0)),
                  pl.BlockSpec((pl.Element(1), pl.Element(2 * MLSTM_HEADS), pl.Element(d)),
                               lambda l, j: (l, OFF_IC, 0))],
        out_specs=[pl.BlockSpec((None, rows, d), lambda l, j: (l, j, 0)),
                   pl.BlockSpec((None, LANES, d), lambda l, j: (l, 0, 0))],
        out_shape=[jax.ShapeDtypeStruct((depth, W_OFF_SMALL, d), BF16),
                   jax.ShapeDtypeStruct((depth, LANES, d), BF16)],
        compiler_params=_cparams(("parallel", "arbitrary")),
        name="weight_relayout",
    )(w_t, w_t, w_t)


def _prep_weights(w_in, w_branch, w_out, w_ple, w_ple_gate):
    w_main, w_small = _relayout_call(jnp.swapaxes(w_in, 1, 2), 512)
    return (w_main, w_small, w_branch.astype(BF16), w_out.astype(BF16), w_ple.astype(BF16),
            w_ple_gate.astype(BF16))


def _layer(x, xb, p_i, li, weights, conv_w, conv_b, dt_bias, a_log, d_skip, ssm_norm_g,
           diff_lambda, diff_norm_g, mlstm_gate_b, ln_g, ln_b, ple_norm_g):
    w_all, w_small, w_br, w_out, w_ple, w_pg = weights
    bsz, s, _ = x.shape
    t = bsz * s
    x2 = x.reshape(t, D_MODEL)
    xb2 = xb.reshape(t, D_MODEL)

    proj, ps = _proj_call(xb2, w_all, w_small, li, _row_tile(t, 1024), 1024)
    proj = proj.reshape(bsz, s, PROJ_WIDTH)
    ps = ps.reshape(bsz, s, LANES)

    pad = lambda v, lane0: jnp.zeros((1, LANES), F32).at[0, lane0:lane0 + v.shape[0]].set(v.astype(F32))
    out_a = _ssd_call(proj, ps, conv_w, conv_b.reshape(1, -1), pad(dt_bias, LANE_DT), pad(a_log, LANE_DT),
                      jnp.repeat(d_skip.astype(F32), SSM_HEAD_DIM).reshape(1, -1), ssm_norm_g.reshape(1, -1))

    lam_init = 0.8 - 0.6 * math.exp(-0.3 * li)
    slopes = 2.0 ** (-8.0 * jnp.arange(1, DIFF_HEADS + 1, dtype=F32) / DIFF_HEADS)
    slopes = jnp.broadcast_to(slopes[:, None, None], (DIFF_HEADS, 1, LANES))
    out_b = _attn_call(proj, slopes, diff_lambda, diff_norm_g.reshape(1, -1), lam_init)

    gate_bias = pad(mlstm_gate_b[0], LANE_I) + pad(mlstm_gate_b[1], LANE_F)
    out_c = _mlstm_call(proj, ps, gate_bias)

    tm = _row_tile(t, 1024)
    merged = _merge_call(xb2, out_a.reshape(t, -1), out_b.reshape(t, -1), out_c.reshape(t, -1),
                         w_all, w_br, li, tm, 256)
    out, out_b16 = _final_call(merged, w_out, w_pg, x2, p_i.reshape(t, PLE_DIM).astype(BF16), w_ple,
                               ln_g.reshape(1, -1), ln_b.reshape(1, -1), ple_norm_g.reshape(1, -1), li, tm, 256)
    return out.reshape(bsz, s, D_MODEL), out_b16.reshape(bsz, s, D_MODEL)


def kernel(x, p, w_in, conv_w, conv_b, dt_bias, a_log, d_skip, ssm_norm_g, diff_lambda, diff_norm_g,
           mlstm_gate_b, w_branch, w_out, ln_g, ln_b, w_ple, ple_norm_g, w_ple_gate):
    weights = _prep_weights(w_in, w_branch, w_out, w_ple, w_ple_gate)
    h = x
    hb = x.astype(BF16)
    for li in range(w_in.shape[0]):
        h, hb = _layer(h, hb, p[li], li, weights, conv_w[li], conv_b[li], dt_bias[li], a_log[li],
                       d_skip[li], ssm_norm_g[li], diff_lambda[li], diff_norm_g[li],
                       mlstm_gate_b[li], ln_g[li], ln_b[li], ple_norm_g[li])
    return h
```

```python
import functools
import math

import jax
import jax.numpy as jnp
from jax import lax
from jax.experimental import pallas as pl
from jax.experimental.pallas import tpu as pltpu

F32 = jnp.float32
BF16 = jnp.bfloat16

V7X_VMEM_LIMIT_BYTES = 56 * 1024 * 1024
LANES = 128
F32_SUBLANES = 8

D_MODEL = 4096
CHUNK = 64
SSM_WIDTH = 2048
SSM_HEAD_DIM = 64
SSM_HEADS = 32
SSM_GROUPS = 8
SSM_STATE = 128
SSM_XBC = SSM_WIDTH + 2 * SSM_GROUPS * SSM_STATE
GROUP_WIDTH = SSM_WIDTH // SSM_GROUPS
HEADS_PER_GROUP = SSM_HEADS // SSM_GROUPS
DIFF_WIDTH = 1024
DIFF_HEAD_DIM = 64
DIFF_HEADS = 8
MLSTM_WIDTH = 1024
MLSTM_HEAD_DIM = 128
MLSTM_HEADS = 8
N_BRANCH = 3
PLE_DIM = 256
DEPTH = 2
DEEPNORM_ALPHA = (2.0 * DEPTH) ** 0.25

OFF_XBC = 0
OFF_ZA = OFF_XBC + SSM_XBC
OFF_DT = OFF_ZA + SSM_WIDTH
OFF_QB = OFF_DT + SSM_HEADS
OFF_ZB = OFF_QB + 3 * DIFF_WIDTH
OFF_QC = OFF_ZB + DIFF_WIDTH
OFF_IC = OFF_QC + 5 * MLSTM_WIDTH
OFF_FC = OFF_IC + MLSTM_HEADS
OFF_GATES = OFF_FC + MLSTM_HEADS

P_OFF_XBC = 0
P_OFF_ZA = P_OFF_XBC + SSM_XBC
P_OFF_QB = P_OFF_ZA + SSM_WIDTH
P_OFF_KB = P_OFF_QB + DIFF_WIDTH
P_OFF_VB = P_OFF_KB + DIFF_WIDTH
P_OFF_ZB = P_OFF_VB + DIFF_WIDTH
P_OFF_C = P_OFF_ZB + DIFF_WIDTH
PROJ_WIDTH = P_OFF_C + 5 * MLSTM_WIDTH
W_OFF_GATES = PROJ_WIDTH
W_OFF_SMALL = W_OFF_GATES + N_BRANCH * D_MODEL

LANE_DT = 0
LANE_I = SSM_HEADS
LANE_F = SSM_HEADS + MLSTM_HEADS

SCAN_CHUNK = 256
ATT_BLOCK = 512
ATT_GROUP = 4
ATT_STAGE = 1
ATT_AUG = 2 * LANES
ATT_ACC_ROWS = LANES + 16
POS_SPLIT = 256
NORM_ROWS = 128
NEG_BIG = -1e30
LOG2E = 1.4426950408889634


def _cparams(sem):
    return pltpu.CompilerParams(dimension_semantics=sem, vmem_limit_bytes=V7X_VMEM_LIMIT_BYTES)


def _silu(x):
    return x * jax.nn.sigmoid(x)


def _softplus(x):
    return jnp.maximum(x, 0.0) + jnp.log1p(jnp.exp(-jnp.abs(x)))


def _dot(a, b):
    return jnp.dot(a, b, preferred_element_type=F32)


def _dot_nt(a, b):
    return lax.dot_general(a, b, (((1,), (1,)), ((), ())), preferred_element_type=F32)


def _dot_tn(a, b):
    return lax.dot_general(a, b, (((0,), (0,)), ((), ())), preferred_element_type=F32)


def _cumsum_rows(x, tril_bf16):
    hi = x.astype(BF16)
    r1 = x - hi.astype(F32)
    mid = r1.astype(BF16)
    lo = (r1 - mid.astype(F32)).astype(BF16)
    return _dot(tril_bf16, hi) + _dot(tril_bf16, mid) + _dot(tril_bf16, lo)


def _proj_kernel(x_ref, w_ref, ws_ref, o_ref, os_ref):
    o_ref[...] = _dot_nt(x_ref[...], w_ref[...])

    @pl.when(pl.program_id(1) == 0)
    def _():
        os_ref[...] = _dot_nt(x_ref[...], ws_ref[...])


def _proj_call(xb, w_main, w_small, li, tm, tn):
    m, k = xb.shape
    return pl.pallas_call(
        _proj_kernel,
        grid=(m // tm, PROJ_WIDTH // tn),
        in_specs=[pl.BlockSpec((tm, k), lambda i, j: (i, 0)),
                  pl.BlockSpec((None, tn, k), lambda i, j: (li, j, 0)),
                  pl.BlockSpec((None, LANES, k), lambda i, j: (li, 0, 0))],
        out_specs=[pl.BlockSpec((tm, tn), lambda i, j: (i, j)),
                   pl.BlockSpec((tm, LANES), lambda i, j: (i, 0))],
        out_shape=[jax.ShapeDtypeStruct((m, PROJ_WIDTH), F32), jax.ShapeDtypeStruct((m, LANES), F32)],
        compiler_params=_cparams(("parallel", "arbitrary")),
        name="in_proj",
    )(xb, w_main, w_small)


def _ssd_kernel(xbc_ref, z_ref, ps_ref, cw_ref, cb_ref, dtb_ref, alog_ref, dskip_ref, ng_ref,
                o_ref, xpad, state, ybuf):
    L = SCAN_CHUNK
    c = pl.program_id(1)

    @pl.when(c == 0)
    def _():
        xpad[0:8, :] = jnp.zeros((8, SSM_XBC), F32)
        state[...] = jnp.zeros(state.shape, F32)

    @pl.when(c > 0)
    def _():
        xpad[0:8, :] = xpad[L:L + 8, :]

    xpad[8:8 + L, :] = xbc_ref[0]

    def conv_silu(c0, width):
        acc = cb_ref[:, c0:c0 + width] + cw_ref[3:4, c0:c0 + width] * xpad[8:8 + L, c0:c0 + width]
        acc = acc + cw_ref[2:3, c0:c0 + width] * xpad[7:7 + L, c0:c0 + width]
        acc = acc + cw_ref[1:2, c0:c0 + width] * xpad[6:6 + L, c0:c0 + width]
        acc = acc + cw_ref[0:1, c0:c0 + width] * xpad[5:5 + L, c0:c0 + width]
        return _silu(acc)

    rows = lax.broadcasted_iota(jnp.int32, (L, L), 0)
    cols = lax.broadcasted_iota(jnp.int32, (L, L), 1)
    causal = rows >= cols
    tril = jnp.where(causal, 1.0, 0.0).astype(BF16)

    dt = _softplus(ps_ref[0] + dtb_ref[...])
    la = dt * (-jnp.exp(alog_ref[...]))
    lc = _cumsum_rows(la, tril)
    lc_t = lc.T

    head_of_lane = lax.broadcasted_iota(jnp.int32, (L, GROUP_WIDTH), 1) // SSM_HEAD_DIM

    def expand(vals, hol):
        out = vals[HEADS_PER_GROUP - 1]
        for r in range(HEADS_PER_GROUP - 2, -1, -1):
            out = jnp.where(hol == r, vals[r], out)
        return out

    ssq = jnp.zeros((L, 1), F32)
    for g in range(SSM_GROUPS):
        h0 = g * HEADS_PER_GROUP
        xg = conv_silu(g * GROUP_WIDTH, GROUP_WIDTH)
        bg = conv_silu(SSM_WIDTH + g * SSM_STATE, SSM_STATE).astype(BF16)
        cg = conv_silu(SSM_WIDTH + SSM_GROUPS * SSM_STATE + g * SSM_STATE, SSM_STATE).astype(BF16)

        dt4 = expand([dt[:, h0 + r:h0 + r + 1] for r in range(HEADS_PER_GROUP)], head_of_lane)
        lc4 = expand([lc[:, h0 + r:h0 + r + 1] for r in range(HEADS_PER_GROUP)], head_of_lane)
        lc_end4 = expand([lc[L - 1:L, h0 + r:h0 + r + 1] for r in range(HEADS_PER_GROUP)],
                         head_of_lane[0:1, :])
        xdt = xg * dt4

        cb = _dot_nt(cg, bg)
        y = jnp.zeros((L, GROUP_WIDTH), F32)
        for r in range(HEADS_PER_GROUP):
            h = h0 + r
            seg = lc[:, h:h + 1] - lc_t[h:h + 1, :]
            decay = jnp.where(causal, jnp.exp(seg), 0.0)
            m_r = (cb * decay).astype(BF16)
            x_r = jnp.where(head_of_lane == r, xdt, 0.0).astype(BF16)
            y = y + _dot(m_r, x_r)

        prev = state[g]
        y = y + _dot(cg, prev.astype(BF16)) * jnp.exp(lc4)
        xs = (xdt * jnp.exp(lc_end4 - lc4)).astype(BF16)
        state[g] = prev * jnp.exp(lc_end4) + _dot_tn(bg, xs)

        y = y + dskip_ref[:, g * GROUP_WIDTH:(g + 1) * GROUP_WIDTH] * xg
        yz = y * _silu(z_ref[0, :, g * GROUP_WIDTH:(g + 1) * GROUP_WIDTH])
        ybuf[:, g * GROUP_WIDTH:(g + 1) * GROUP_WIDTH] = yz
        ssq = ssq + jnp.sum(yz * yz, axis=-1, keepdims=True)

    rs = lax.rsqrt(ssq * (1.0 / SSM_WIDTH) + 1e-5)
    o_ref[0] = (ybuf[...] * rs * ng_ref[...]).astype(o_ref.dtype)


def _ssd_call(pa, ps, conv_w, conv_b, dtb, alog, dskip, norm_g):
    bsz, s, _ = pa.shape
    L = SCAN_CHUNK
    full = lambda b, c: (0, 0)
    return pl.pallas_call(
        _ssd_kernel,
        grid=(bsz, s // L),
        in_specs=[pl.BlockSpec((1, L, SSM_XBC), lambda b, c: (b, c, 0)),
                  pl.BlockSpec((1, L, SSM_WIDTH), lambda b, c: (b, c, SSM_XBC // SSM_WIDTH)),
                  pl.BlockSpec((1, L, LANES), lambda b, c: (b, c, 0)),
                  pl.BlockSpec(conv_w.shape, full),
                  pl.BlockSpec(conv_b.shape, full),
                  pl.BlockSpec(dtb.shape, full),
                  pl.BlockSpec(alog.shape, full),
                  pl.BlockSpec(dskip.shape, full),
                  pl.BlockSpec(norm_g.shape, full)],
        out_specs=pl.BlockSpec((1, L, SSM_WIDTH), lambda b, c: (b, c, 0)),
        out_shape=jax.ShapeDtypeStruct((bsz, s, SSM_WIDTH), BF16),
        scratch_shapes=[pltpu.VMEM((L + 8, SSM_XBC), F32),
                        pltpu.VMEM((SSM_GROUPS, SSM_STATE, GROUP_WIDTH), F32),
                        pltpu.VMEM((L, SSM_WIDTH), F32)],
        compiler_params=_cparams(("parallel", "arbitrary")),
        name="ssd_scan",
    )(pa, pa, ps, conv_w, conv_b, dtb, alog, dskip, norm_g)


def _attn_kernel(lam_init, q_ref, k_ref, v_ref, z_ref, slope_ref, corr_ref, dl_ref, ng_ref, o_ref,
                 k_aug, vt_aug, acc1, acc2):
    T = ATT_BLOCK
    D2 = 2 * DIFF_HEAD_DIM
    qi = pl.program_id(2)
    n_kv = k_ref.shape[1] // T

    @pl.when(qi == 0)
    def _():
        lane = lax.broadcasted_iota(jnp.int32, (T, LANES), 1)
        row = lax.broadcasted_iota(jnp.int32, (T, LANES), 0)
        for c in range(n_kv):
            rows = slice(c * T, (c + 1) * T)
            pos = row + c * T
            pos_lo = jnp.bitwise_and(pos, POS_SPLIT - 1)
            pos_hi = pos - pos_lo
            pos_cols = jnp.where(lane < 3, pos_lo, jnp.where(lane < 6, pos_hi, 0))
            k_aug[rows, 0:D2] = k_ref[0, rows, :].astype(BF16)
            k_aug[rows, D2:ATT_AUG] = pos_cols.astype(F32).astype(BF16)
            vt_aug[0:D2, rows] = v_ref[0, rows, :].T.astype(BF16)
        ones_row = lax.broadcasted_iota(jnp.int32, (ATT_ACC_ROWS - D2, k_ref.shape[1]), 0) == 0
        vt_aug[D2:ATT_ACC_ROWS, :] = jnp.where(ones_row, 1.0, 0.0).astype(BF16)

    lane = lax.broadcasted_iota(jnp.int32, (T, D2), 1)
    qs = q_ref[0] * (DIFF_HEAD_DIM ** -0.5 * LOG2E)
    sl2 = slope_ref[0][:, 0:1] * LOG2E
    sl_hi = sl2.astype(BF16).astype(F32)
    sl_mid = (sl2 - sl_hi).astype(BF16).astype(F32)
    sl_lo = sl2 - sl_hi - sl_mid
    lane1 = lax.broadcasted_iota(jnp.int32, (1, LANES), 1)
    ext = jnp.where((lane1 == 0) | (lane1 == 3), sl_hi,
                    jnp.where((lane1 == 1) | (lane1 == 4), sl_mid,
                              jnp.where((lane1 == 2) | (lane1 == 5), sl_lo, 0.0)))
    ext = jnp.broadcast_to(ext, (T, LANES)).astype(BF16)
    qa1 = jnp.concatenate([jnp.where(lane < DIFF_HEAD_DIM, qs, 0.0).astype(BF16), ext], axis=1)
    qa2 = jnp.concatenate([jnp.where(lane >= DIFF_HEAD_DIM, qs, 0.0).astype(BF16), ext], axis=1)

    acc1[...] = jnp.zeros(acc1.shape, F32)
    acc2[...] = jnp.zeros(acc2.shape, F32)

    def block(first, n, ms, corr):
        qas, accs = (qa1, qa2), (acc1, acc2)
        starts = [pl.multiple_of((first + t) * T, T) for t in range(n)]
        stages = [list(range(i, min(i + ATT_STAGE, n))) for i in range(0, n, ATT_STAGE)]

        def score_step(t, m_run, out):
            kb = k_aug[pl.ds(starts[t], T), :]
            for a in range(2):
                s = _dot_nt(kb, qas[a])
                if corr is not None and t == n - 1:
                    s = s + corr
                out[a].append(s)
                m_run[a] = jnp.maximum(m_run[a], jnp.max(s, axis=0, keepdims=True))

        def value_step(t, idx, scores, m_stage, pv):
            vb = vt_aug[:, pl.ds(starts[t], T)]
            for a in range(2):
                d = _dot(vb, jnp.exp2(scores[a][idx] - m_stage[a]).astype(BF16))
                pv[a] = d if pv[a] is None else pv[a] + d

        m_prev = list(ms)
        pending = None
        for stage in stages + [None]:
            scores, m_run = [[], []], (list(pending[2]) if pending else list(ms))
            pv = [None, None]
            for idx in range(ATT_STAGE):
                if stage is not None and idx < len(stage):
                    score_step(stage[idx], m_run, scores)
                if pending is not None and idx < len(pending[0]):
                    value_step(pending[0][idx], idx, pending[1], pending[2], pv)
            if pending is not None:
                for a in range(2):
                    accs[a][...] = jnp.exp2(m_prev[a] - pending[2][a]) * accs[a][...] + pv[a]
                m_prev = list(pending[2])
            pending = (stage, scores, m_run) if stage is not None else None
        return tuple(m_prev)

    m0 = jnp.full((1, T), NEG_BIG, F32)
    ms = lax.fori_loop(0, qi // ATT_GROUP, lambda i, ms: block(ATT_GROUP * i, ATT_GROUP, ms, None), (m0, m0))
    corr = corr_ref[...] * sl2
    for left in range(ATT_GROUP):
        @pl.when(qi % ATT_GROUP == left)
        def _():
            block(qi - left, left + 1, ms, corr)

    dl = dl_ref[...]
    lam = (jnp.exp(jnp.sum(dl[0:1] * dl[1:2], axis=-1, keepdims=True))
           - jnp.exp(jnp.sum(dl[2:3] * dl[3:4], axis=-1, keepdims=True)) + lam_init)
    o_t = acc1[0:D2, :] / acc1[D2:D2 + 1, :] - lam * (acc2[0:D2, :] / acc2[D2:D2 + 1, :])
    o = o_t.T
    y = o * lax.rsqrt(jnp.mean(o * o, axis=-1, keepdims=True) + 1e-5) * ng_ref[...]
    y = y * (1.0 - lam_init)
    o_ref[0] = (y * _silu(z_ref[0])).astype(o_ref.dtype)


def _attn_corr_table():
    kk = jnp.arange(ATT_BLOCK)[:, None]
    qq = jnp.arange(ATT_BLOCK)[None, :]
    visible = (kk // CHUNK) <= (qq // CHUNK)
    return jnp.where(visible, -2.0 * jnp.maximum(kk - qq, 0).astype(F32), NEG_BIG)


def _attn_call(proj, slopes, dl, norm_g, lam_init):
    bsz, s, _ = proj.shape
    T = ATT_BLOCK
    hw = 2 * DIFF_HEAD_DIM
    blk = lambda off: off // hw
    const2 = lambda b, h, i: (0, 0)
    return pl.pallas_call(
        functools.partial(_attn_kernel, lam_init),
        grid=(bsz, DIFF_HEADS, s // T),
        in_specs=[pl.BlockSpec((1, T, hw), lambda b, h, i: (b, i, blk(P_OFF_QB) + h)),
                  pl.BlockSpec((1, s, hw), lambda b, h, i: (b, 0, blk(P_OFF_KB) + h)),
                  pl.BlockSpec((1, s, hw), lambda b, h, i: (b, 0, blk(P_OFF_VB) + h)),
                  pl.BlockSpec((1, T, hw), lambda b, h, i: (b, i, blk(P_OFF_ZB) + h)),
                  pl.BlockSpec((1, 1, LANES), lambda b, h, i: (h, 0, 0)),
                  pl.BlockSpec((T, T), const2),
                  pl.BlockSpec(dl.shape, const2),
                  pl.BlockSpec(norm_g.shape, const2)],
        out_specs=pl.BlockSpec((1, T, hw), lambda b, h, i: (b, i, h)),
        out_shape=jax.ShapeDtypeStruct((bsz, s, DIFF_WIDTH), BF16),
        scratch_shapes=[pltpu.VMEM((s, ATT_AUG), BF16),
                        pltpu.VMEM((ATT_ACC_ROWS, s), BF16),
                        pltpu.VMEM((ATT_ACC_ROWS, T), F32),
                        pltpu.VMEM((ATT_ACC_ROWS, T), F32)],
        compiler_params=_cparams(("parallel", "parallel", "arbitrary")),
        name="diff_attention",
    )(proj, proj, proj, proj, slopes, _attn_corr_table(), dl, norm_g)


def _mlstm_kernel(q_ref, k_ref, v_ref, o_ref_in, z_ref, ps_ref, gb_ref, out_ref, c_state, m_state):
    L = SCAN_CHUNK
    D = MLSTM_HEAD_DIM
    c = pl.program_id(1)

    @pl.when(c == 0)
    def _():
        c_state[...] = jnp.zeros(c_state.shape, F32)
        m_state[...] = jnp.zeros(m_state.shape, F32)

    ones_col = jnp.where(lax.broadcasted_iota(jnp.int32, (L, D), 1) == 0, 1.0, 0.0)
    rep = lambda col: jnp.broadcast_to(col, (col.shape[0], LANES))
    twice = lambda x: jnp.concatenate([x, x], axis=1)

    rows = lax.broadcasted_iota(jnp.int32, (L, L), 0)
    cols = lax.broadcasted_iota(jnp.int32, (L, L), 1)
    causal = rows >= cols
    tril = jnp.where(causal, 1.0, 0.0).astype(BF16)

    raw = ps_ref[0] + gb_ref[...]
    logf = -_softplus(-raw)
    bcum = _cumsum_rows(logf, tril)
    bcum_t = bcum.T
    raw_t = raw.T

    for h in range(MLSTM_HEADS):
        li, lf = LANE_I + h, LANE_F + h
        sl = slice(h * D, (h + 1) * D)
        q = q_ref[0, :, sl]
        ks = k_ref[0, :, sl] * (D ** -0.5)
        v_ext = jnp.concatenate([v_ref[0, :, sl], ones_col], axis=1)
        qb, kb = q.astype(BF16), ks.astype(BF16)

        b_rep = rep(bcum[:, lf:lf + 1])
        ig_rep = rep(raw[:, li:li + 1])
        b_row = bcum_t[lf:lf + 1, :]
        ig_row = raw_t[li:li + 1, :]
        btot = b_rep[L - 1:L, :]
        m_prev = m_state[h:h + 1, :]
        c_prev = c_state[h]

        dmat = jnp.where(causal, twice(b_rep) - b_row + ig_row, NEG_BIG)
        m_inter = b_rep + m_prev
        m_row = jnp.maximum(m_inter, rep(jnp.max(dmat, axis=-1, keepdims=True)))
        wts = jnp.exp(dmat - twice(m_row))
        sw = wts * _dot_nt(qb, kb)
        a_inter = jnp.exp(m_inter - m_row)
        nd = _dot(sw.astype(BF16), v_ext.astype(BF16)) + _dot(qb, c_prev.astype(BF16)) * twice(a_inter)
        hout = nd[:, :D] / jnp.maximum(jnp.abs(rep(nd[:, D:D + 1])), jnp.exp(-m_row))
        gate = jax.nn.sigmoid(o_ref_in[0, :, sl]) * _silu(z_ref[0, :, sl])
        out_ref[0, :, sl] = (gate * hout).astype(out_ref.dtype)

        w_end = btot - b_rep + ig_rep
        m_loc = jnp.max(w_end, axis=0, keepdims=True)
        m_new = jnp.maximum(btot + m_prev, m_loc)
        a_old = jnp.exp(btot + m_prev - m_new)
        e_end = jnp.exp(w_end - m_new)
        c_state[h] = twice(a_old) * c_prev + _dot_tn(kb, (v_ext * twice(e_end)).astype(BF16))
        m_state[h:h + 1, :] = m_new


def _mlstm_call(proj, ps, gate_bias):
    bsz, s, _ = proj.shape
    L = SCAN_CHUNK
    W = MLSTM_WIDTH
    blk0 = P_OFF_C // W
    return pl.pallas_call(
        _mlstm_kernel,
        grid=(bsz, s // L),
        in_specs=[pl.BlockSpec((1, L, W), lambda b, c: (b, c, blk0)),
                  pl.BlockSpec((1, L, W), lambda b, c: (b, c, blk0 + 1)),
                  pl.BlockSpec((1, L, W), lambda b, c: (b, c, blk0 + 2)),
                  pl.BlockSpec((1, L, W), lambda b, c: (b, c, blk0 + 3)),
                  pl.BlockSpec((1, L, W), lambda b, c: (b, c, blk0 + 4)),
                  pl.BlockSpec((1, L, LANES), lambda b, c: (b, c, 0)),
                  pl.BlockSpec(gate_bias.shape, lambda b, c: (0, 0))],
        out_specs=pl.BlockSpec((1, L, W), lambda b, c: (b, c, 0)),
        out_shape=jax.ShapeDtypeStruct((bsz, s, W), BF16),
        scratch_shapes=[pltpu.VMEM((MLSTM_HEADS, MLSTM_HEAD_DIM, 2 * MLSTM_HEAD_DIM), F32),
                        pltpu.VMEM((MLSTM_HEADS, LANES), F32)],
        compiler_params=_cparams(("parallel", "arbitrary")),
        name="mlstm_scan",
    )(proj, proj, proj, proj, proj, ps, gate_bias)


def _merge_kernel(x_ref, oa_ref, ob_ref, oc_ref, wga_ref, wgb_ref, wgc_ref, wa_ref, wb_ref, wc_ref, o_ref):
    x = x_ref[...]
    acc = jax.nn.sigmoid(_dot_nt(x, wga_ref[...])) * _dot(oa_ref[...], wa_ref[...])
    acc = acc + jax.nn.sigmoid(_dot_nt(x, wgb_ref[...])) * _dot(ob_ref[...], wb_ref[...])
    acc = acc + jax.nn.sigmoid(_dot_nt(x, wgc_ref[...])) * _dot(oc_ref[...], wc_ref[...])
    o_ref[...] = acc.astype(o_ref.dtype)


def _merge_call(xb, oa, ob, oc, w_all, w_br, li, tm, tn):
    m = xb.shape[0]
    row = lambda i, j: (i, 0)
    gate_spec = lambda br: pl.BlockSpec(
        (None, tn, D_MODEL), lambda i, j: (li, (W_OFF_GATES + br * D_MODEL) // tn + j, 0))
    return pl.pallas_call(
        _merge_kernel,
        grid=(m // tm, D_MODEL // tn),
        in_specs=[pl.BlockSpec((tm, D_MODEL), row),
                  pl.BlockSpec((tm, SSM_WIDTH), row, pipeline_mode=pl.Buffered(1)),
                  pl.BlockSpec((tm, DIFF_WIDTH), row, pipeline_mode=pl.Buffered(1)),
                  pl.BlockSpec((tm, MLSTM_WIDTH), row, pipeline_mode=pl.Buffered(1)),
                  gate_spec(0), gate_spec(1), gate_spec(2),
                  pl.BlockSpec((None, SSM_WIDTH, tn), lambda i, j: (li, 0, j)),
                  pl.BlockSpec((None, DIFF_WIDTH, tn), lambda i, j: (li, SSM_WIDTH // DIFF_WIDTH, j)),
                  pl.BlockSpec((None, MLSTM_WIDTH, tn),
                               lambda i, j: (li, (SSM_WIDTH + DIFF_WIDTH) // MLSTM_WIDTH, j))],
        out_specs=pl.BlockSpec((tm, tn), lambda i, j: (i, j)),
        out_shape=jax.ShapeDtypeStruct((m, D_MODEL), BF16),
        compiler_params=_cparams(("parallel", "arbitrary")),
        name="gated_merge",
    )(xb, oa, ob, oc, w_all, w_all, w_all, w_br, w_br, w_br)


def _final_kernel(nj, tn, mg_ref, wo_ref, wg_ref, x_ref, p_ref, wple_ref, lng_ref, lnb_ref, pg_ref,
                  o_ref, ob_ref, h_scr, hb_scr, st_scr):
    j = pl.program_id(1)
    tm = h_scr.shape[0]

    @pl.when(j == 0)
    def _():
        st_scr[...] = jnp.zeros(st_scr.shape, F32)

    @pl.when(j < nj)
    def _():
        off = pl.multiple_of(j * tn, tn)
        pre = DEEPNORM_ALPHA * x_ref[...] + _dot(mg_ref[...], wo_ref[...])
        h_scr[:, pl.ds(off, tn)] = pre
        st_scr[:, 0:LANES] += jnp.sum(pre, axis=-1, keepdims=True)

    @pl.when(j == nj)
    def _():
        def norm_rows(r, carry):
            rows = pl.ds(pl.multiple_of(r * NORM_ROWS, NORM_ROWS), NORM_ROWS)
            cen = h_scr[rows, :] - st_scr[rows, 0:1] * (1.0 / D_MODEL)
            var = jnp.mean(cen * cen, axis=-1, keepdims=True)
            h = cen * lax.rsqrt(var + 1e-5) * lng_ref[...] + lnb_ref[...]
            h_scr[rows, :] = h
            hb_scr[rows, :] = h.astype(BF16)
            e = _dot(p_ref[rows, :], wple_ref[...])
            st_scr[rows, LANES:2 * LANES] = jnp.broadcast_to(
                jnp.sum(e * e, axis=-1, keepdims=True), (NORM_ROWS, LANES))
            return carry

        lax.fori_loop(0, tm // NORM_ROWS, norm_rows, 0)

    @pl.when(j >= nj)
    def _():
        off = pl.multiple_of((j - nj) * tn, tn)
        gate = jax.nn.sigmoid(_dot(hb_scr[...], wg_ref[...]))
        rs = lax.rsqrt(st_scr[:, LANES:LANES + 1] * (1.0 / D_MODEL) + 1e-5)
        e = _dot(p_ref[...], wple_ref[:, pl.ds(off, tn)]) * rs * pg_ref[:, pl.ds(off, tn)]
        out = h_scr[:, pl.ds(off, tn)] + gate * e
        o_ref[...] = out
        ob_ref[...] = out.astype(BF16)


def _final_call(merged, w_out, w_pg, x, pb, w_ple, ln_g, ln_b, ple_g, li, tm, tn):
    m = merged.shape[0]
    nj = D_MODEL // tn
    full = lambda i, j: (0, 0)
    out_idx = lambda i, j: (i, jnp.maximum(j - nj, 0))
    return pl.pallas_call(
        functools.partial(_final_kernel, nj, tn),
        grid=(m // tm, 2 * nj),
        in_specs=[pl.BlockSpec((tm, D_MODEL), lambda i, j: (i, 0), pipeline_mode=pl.Buffered(1)),
                  pl.BlockSpec((None, D_MODEL, tn), lambda i, j: (li, 0, jnp.minimum(j, nj - 1))),
                  pl.BlockSpec((None, D_MODEL, tn), lambda i, j: (li, 0, jnp.maximum(j - nj, 0))),
                  pl.BlockSpec((tm, tn), lambda i, j: (i, jnp.minimum(j, nj - 1))),
                  pl.BlockSpec((tm, PLE_DIM), lambda i, j: (i, 0)),
                  pl.BlockSpec((None,) + w_ple.shape[1:], lambda i, j: (li, 0, 0), pipeline_mode=pl.Buffered(1)),
                  pl.BlockSpec(ln_g.shape, full),
                  pl.BlockSpec(ln_b.shape, full),
                  pl.BlockSpec(ple_g.shape, full)],
        out_specs=[pl.BlockSpec((tm, tn), out_idx), pl.BlockSpec((tm, tn), out_idx)],
        out_shape=[jax.ShapeDtypeStruct((m, D_MODEL), F32), jax.ShapeDtypeStruct((m, D_MODEL), BF16)],
        scratch_shapes=[pltpu.VMEM((tm, D_MODEL), F32),
                        pltpu.VMEM((tm, D_MODEL), BF16),
                        pltpu.VMEM((tm, 2 * LANES), F32)],
        compiler_params=_cparams(("parallel", "arbitrary")),
        name="outproj_deepnorm_ple",
    )(merged, w_out, w_pg, x, pb, w_ple, ln_g, ln_b, ple_g)


def _row_tile(m, pref):
    return pref if m % pref == 0 else m


def _relayout_kernel(a_ref, dt_ref, if_ref, o_ref, os_ref):
    o_ref[...] = a_ref[0].astype(BF16)

    @pl.when(pl.program_id(1) == 0)
    def _():
        os_ref[LANE_DT:LANE_I, :] = dt_ref[0].astype(BF16)
        os_ref[LANE_I:LANE_F + MLSTM_HEADS, :] = if_ref[0].astype(BF16)
        os_ref[LANE_F + MLSTM_HEADS:, :] = jnp.zeros((LANES - LANE_F - MLSTM_HEADS, os_ref.shape[1]), BF16)


def _relayout_call(w_t, rows):
    depth, _, d = w_t.shape

    def src_row(j):
        r = j * rows
        shift = jnp.where(r < P_OFF_QB, 0, jnp.where(r < W_OFF_GATES, (OFF_QB - P_OFF_QB) // F32_SUBLANES,
                                                     (OFF_GATES - W_OFF_GATES) // F32_SUBLANES))
        return (j * (rows // F32_SUBLANES) + shift) * F32_SUBLANES

    return pl.pallas_call(
        _relayout_kernel,
        grid=(depth, W_OFF_SMALL // rows),
        in_specs=[pl.BlockSpec((pl.Element(1), pl.Element(rows), pl.Element(d)), lambda l, j: (l, src_row(j), 0)),
                  pl.BlockSpec((pl.Element(1), pl.Element(SSM_HEADS), pl.Element(d)), lambda l, j: (l, OFF_DT, 0)),
                  pl.BlockSpec((pl.Element(1), pl.Element(2 * MLSTM_HEADS), pl.Element(d)),
                               lambda l, j: (l, OFF_IC, 0))],
        out_specs=[pl.BlockSpec((None, rows, d), lambda l, j: (l, j, 0)),
                   pl.BlockSpec((None, LANES, d), lambda l, j: (l, 0, 0))],
        out_shape=[jax.ShapeDtypeStruct((depth, W_OFF_SMALL, d), BF16),
                   jax.ShapeDtypeStruct((depth, LANES, d), BF16)],
        compiler_params=_cparams(("parallel", "arbitrary")),
        name="weight_relayout",
    )(w_t, w_t, w_t)


def _cast_kernel(x_ref, o_ref):
    o_ref[...] = x_ref[...].astype(o_ref.dtype)


def _cast_call(w, rows):
    depth, r, c = w.shape
    spec = pl.BlockSpec((None, rows, c), lambda l, i: (l, i, 0))
    return pl.pallas_call(
        _cast_kernel,
        grid=(depth, r // rows),
        in_specs=[spec],
        out_specs=spec,
        out_shape=jax.ShapeDtypeStruct(w.shape, BF16),
        compiler_params=_cparams(("parallel", "parallel")),
        name="weight_cast",
    )(w)


def _prep_weights(w_in, w_branch, w_out, w_ple, w_ple_gate):
    w_main, w_small = _relayout_call(jnp.swapaxes(w_in, 1, 2), 512)
    return (w_main, w_small, _cast_call(w_branch, 512), _cast_call(w_out, 512), w_ple.astype(BF16),
            _cast_call(w_ple_gate, 512))


def _layer(x, xb, p_i, li, weights, conv_w, conv_b, dt_bias, a_log, d_skip, ssm_norm_g,
           diff_lambda, diff_norm_g, mlstm_gate_b, ln_g, ln_b, ple_norm_g):
    w_all, w_small, w_br, w_out, w_ple, w_pg = weights
    bsz, s, _ = x.shape
    t = bsz * s
    x2 = x.reshape(t, D_MODEL)
    xb2 = xb.reshape(t, D_MODEL)

    proj, ps = _proj_call(xb2, w_all, w_small, li, _row_tile(t, 1024), 1024)
    proj = proj.reshape(bsz, s, PROJ_WIDTH)
    ps = ps.reshape(bsz, s, LANES)

    pad = lambda v, lane0: jnp.zeros((1, LANES), F32).at[0, lane0:lane0 + v.shape[0]].set(v.astype(F32))
    out_a = _ssd_call(proj, ps, conv_w, conv_b.reshape(1, -1), pad(dt_bias, LANE_DT), pad(a_log, LANE_DT),
                      jnp.repeat(d_skip.astype(F32), SSM_HEAD_DIM).reshape(1, -1), ssm_norm_g.reshape(1, -1))

    lam_init = 0.8 - 0.6 * math.exp(-0.3 * li)
    slopes = 2.0 ** (-8.0 * jnp.arange(1, DIFF_HEADS + 1, dtype=F32) / DIFF_HEADS)
    slopes = jnp.broadcast_to(slopes[:, None, None], (DIFF_HEADS, 1, LANES))
    out_b = _attn_call(proj, slopes, diff_lambda, diff_norm_g.reshape(1, -1), lam_init)

    gate_bias = pad(mlstm_gate_b[0], LANE_I) + pad(mlstm_gate_b[1], LANE_F)
    out_c = _mlstm_call(proj, ps, gate_bias)

    tm = _row_tile(t, 1024)
    merged = _merge_call(xb2, out_a.reshape(t, -1), out_b.reshape(t, -1), out_c.reshape(t, -1),
                         w_all, w_br, li, tm, 256)
    out, out_b16 = _final_call(merged, w_out, w_pg, x2, p_i.reshape(t, PLE_DIM).astype(BF16), w_ple,
                               ln_g.reshape(1, -1), ln_b.reshape(1, -1), ple_norm_g.reshape(1, -1), li, tm, 256)
    return out.reshape(bsz, s, D_MODEL), out_b16.reshape(bsz, s, D_MODEL)


def kernel(x, p, w_in, conv_w, conv_b, dt_bias, a_log, d_skip, ssm_norm_g, diff_lambda, diff_norm_g,
           mlstm_gate_b, w_branch, w_out, ln_g, ln_b, w_ple, ple_norm_g, w_ple_gate):
    weights = _prep_weights(w_in, w_branch, w_out, w_ple, w_ple_gate)
    h = x
    hb = x.astype(BF16)
    for li in range(w_in.shape[0]):
        h, hb = _layer(h, hb, p[li], li, weights, conv_w[li], conv_b[li], dt_bias[li], a_log[li],
                       d_skip[li], ssm_norm_g[li], diff_lambda[li], diff_norm_g[li],
                       mlstm_gate_b[li], ln_g[li], ln_b[li], ple_norm_g[li])
    return h
```

```python
import functools
import math

import jax
import jax.numpy as jnp
from jax import lax
from jax.experimental import pallas as pl
from jax.experimental.pallas import tpu as pltpu

F32 = jnp.float32
BF16 = jnp.bfloat16

V7X_VMEM_LIMIT_BYTES = 56 * 1024 * 1024
LANES = 128
F32_SUBLANES = 8

D_MODEL = 4096
CHUNK = 64
SSM_WIDTH = 2048
SSM_HEAD_DIM = 64
SSM_HEADS = 32
SSM_GROUPS = 8
SSM_STATE = 128
SSM_XBC = SSM_WIDTH + 2 * SSM_GROUPS * SSM_STATE
GROUP_WIDTH = SSM_WIDTH // SSM_GROUPS
HEADS_PER_GROUP = SSM_HEADS // SSM_GROUPS
DIFF_WIDTH = 1024
DIFF_HEAD_DIM = 64
DIFF_HEADS = 8
MLSTM_WIDTH = 1024
MLSTM_HEAD_DIM = 128
MLSTM_HEADS = 8
N_BRANCH = 3
PLE_DIM = 256
DEPTH = 2
DEEPNORM_ALPHA = (2.0 * DEPTH) ** 0.25

OFF_XBC = 0
OFF_ZA = OFF_XBC + SSM_XBC
OFF_DT = OFF_ZA + SSM_WIDTH
OFF_QB = OFF_DT + SSM_HEADS
OFF_ZB = OFF_QB + 3 * DIFF_WIDTH
OFF_QC = OFF_ZB + DIFF_WIDTH
OFF_IC = OFF_QC + 5 * MLSTM_WIDTH
OFF_FC = OFF_IC + MLSTM_HEADS
OFF_GATES = OFF_FC + MLSTM_HEADS

P_OFF_XBC = 0
P_OFF_ZA = P_OFF_XBC + SSM_XBC
P_OFF_QB = P_OFF_ZA + SSM_WIDTH
P_OFF_KB = P_OFF_QB + DIFF_WIDTH
P_OFF_VB = P_OFF_KB + DIFF_WIDTH
P_OFF_ZB = P_OFF_VB + DIFF_WIDTH
P_OFF_C = P_OFF_ZB + DIFF_WIDTH
PROJ_WIDTH = P_OFF_C + 5 * MLSTM_WIDTH
W_OFF_GATES = PROJ_WIDTH
W_OFF_SMALL = W_OFF_GATES + N_BRANCH * D_MODEL

LANE_DT = 0
LANE_I = SSM_HEADS
LANE_F = SSM_HEADS + MLSTM_HEADS

ROW_TILE = 1024
PROJ_COL_TILE = 1024
EPILOGUE_COL_TILE = 256
RELAYOUT_ROWS = 512
SCAN_CHUNK = 2 * LANES
ATT_BLOCK = 512
ATT_GROUP = 4
ATT_STAGE = 1
ATT_AUG = 2 * LANES
ATT_ACC_ROWS = LANES + 16
POS_SPLIT = 256
NORM_ROWS = 128
NEG_BIG = -1e30
LOG2E = 1.4426950408889634


def _cparams(sem):
    return pltpu.CompilerParams(dimension_semantics=sem, vmem_limit_bytes=V7X_VMEM_LIMIT_BYTES)


def _silu(x):
    return x * jax.nn.sigmoid(x)


def _softplus(x):
    return jnp.maximum(x, 0.0) + jnp.log1p(jnp.exp(-jnp.abs(x)))


def _dot(a, b):
    return jnp.dot(a, b, preferred_element_type=F32)


def _dot_nt(a, b):
    return lax.dot_general(a, b, (((1,), (1,)), ((), ())), preferred_element_type=F32)


def _dot_tn(a, b):
    return lax.dot_general(a, b, (((0,), (0,)), ((), ())), preferred_element_type=F32)


def _cumsum_rows(x, tril_bf16):
    hi = x.astype(BF16)
    r1 = x - hi.astype(F32)
    mid = r1.astype(BF16)
    lo = (r1 - mid.astype(F32)).astype(BF16)
    return _dot(tril_bf16, hi) + _dot(tril_bf16, mid) + _dot(tril_bf16, lo)


def _proj_kernel(x_ref, w_ref, ws_ref, o_ref, os_ref):
    o_ref[...] = _dot_nt(x_ref[...], w_ref[...])

    @pl.when(pl.program_id(1) == 0)
    def _():
        os_ref[...] = _dot_nt(x_ref[...], ws_ref[...])


def _proj_call(xb, w_main, w_small, li, tm, tn):
    m, k = xb.shape
    return pl.pallas_call(
        _proj_kernel,
        grid=(m // tm, PROJ_WIDTH // tn),
        in_specs=[pl.BlockSpec((tm, k), lambda i, j: (i, 0)),
                  pl.BlockSpec((None, tn, k), lambda i, j: (li, j, 0)),
                  pl.BlockSpec((None, LANES, k), lambda i, j: (li, 0, 0))],
        out_specs=[pl.BlockSpec((tm, tn), lambda i, j: (i, j)),
                   pl.BlockSpec((tm, LANES), lambda i, j: (i, 0))],
        out_shape=[jax.ShapeDtypeStruct((m, PROJ_WIDTH), F32), jax.ShapeDtypeStruct((m, LANES), F32)],
        compiler_params=_cparams(("parallel", "arbitrary")),
        name="in_proj",
    )(xb, w_main, w_small)


def _ssd_kernel(xbc_ref, z_ref, ps_ref, cw_ref, cb_ref, dtb_ref, alog_ref, dskip_ref, ng_ref,
                o_ref, xpad, state, ybuf):
    L = SCAN_CHUNK
    c = pl.program_id(1)

    @pl.when(c == 0)
    def _():
        xpad[0:8, :] = jnp.zeros((8, SSM_XBC), F32)
        state[...] = jnp.zeros(state.shape, F32)

    @pl.when(c > 0)
    def _():
        xpad[0:8, :] = xpad[L:L + 8, :]

    xpad[8:8 + L, :] = xbc_ref[0]

    def conv_silu(c0, width):
        acc = cb_ref[:, c0:c0 + width] + cw_ref[3:4, c0:c0 + width] * xpad[8:8 + L, c0:c0 + width]
        acc = acc + cw_ref[2:3, c0:c0 + width] * xpad[7:7 + L, c0:c0 + width]
        acc = acc + cw_ref[1:2, c0:c0 + width] * xpad[6:6 + L, c0:c0 + width]
        acc = acc + cw_ref[0:1, c0:c0 + width] * xpad[5:5 + L, c0:c0 + width]
        return _silu(acc)

    rows = lax.broadcasted_iota(jnp.int32, (L, L), 0)
    cols = lax.broadcasted_iota(jnp.int32, (L, L), 1)
    causal = rows >= cols
    tril = jnp.where(causal, 1.0, 0.0).astype(BF16)

    dt = _softplus(ps_ref[0] + dtb_ref[...])
    la = dt * (-jnp.exp(alog_ref[...]))
    lc = _cumsum_rows(la, tril)
    lc_t = lc.T

    head_of_lane = lax.broadcasted_iota(jnp.int32, (L, GROUP_WIDTH), 1) // SSM_HEAD_DIM

    def expand(vals, hol):
        out = vals[HEADS_PER_GROUP - 1]
        for r in range(HEADS_PER_GROUP - 2, -1, -1):
            out = jnp.where(hol == r, vals[r], out)
        return out

    ssq = jnp.zeros((L, 1), F32)
    for g in range(SSM_GROUPS):
        h0 = g * HEADS_PER_GROUP
        xg = conv_silu(g * GROUP_WIDTH, GROUP_WIDTH)
        bg = conv_silu(SSM_WIDTH + g * SSM_STATE, SSM_STATE).astype(BF16)
        cg = conv_silu(SSM_WIDTH + SSM_GROUPS * SSM_STATE + g * SSM_STATE, SSM_STATE).astype(BF16)

        dt4 = expand([dt[:, h0 + r:h0 + r + 1] for r in range(HEADS_PER_GROUP)], head_of_lane)
        lc4 = expand([lc[:, h0 + r:h0 + r + 1] for r in range(HEADS_PER_GROUP)], head_of_lane)
        lc_end4 = expand([lc[L - 1:L, h0 + r:h0 + r + 1] for r in range(HEADS_PER_GROUP)],
                         head_of_lane[0:1, :])
        xdt = xg * dt4

        cb = _dot_nt(cg, bg)
        y = jnp.zeros((L, GROUP_WIDTH), F32)
        for r in range(HEADS_PER_GROUP):
            h = h0 + r
            seg = lc[:, h:h + 1] - lc_t[h:h + 1, :]
            decay = jnp.where(causal, jnp.exp(seg), 0.0)
            m_r = (cb * decay).astype(BF16)
            x_r = jnp.where(head_of_lane == r, xdt, 0.0).astype(BF16)
            y = y + _dot(m_r, x_r)

        prev = state[g]
        y = y + _dot(cg, prev.astype(BF16)) * jnp.exp(lc4)
        xs = (xdt * jnp.exp(lc_end4 - lc4)).astype(BF16)
        state[g] = prev * jnp.exp(lc_end4) + _dot_tn(bg, xs)

        y = y + dskip_ref[:, g * GROUP_WIDTH:(g + 1) * GROUP_WIDTH] * xg
        yz = y * _silu(z_ref[0, :, g * GROUP_WIDTH:(g + 1) * GROUP_WIDTH])
        ybuf[:, g * GROUP_WIDTH:(g + 1) * GROUP_WIDTH] = yz
        ssq = ssq + jnp.sum(yz * yz, axis=-1, keepdims=True)

    rs = lax.rsqrt(ssq * (1.0 / SSM_WIDTH) + 1e-5)
    o_ref[0] = (ybuf[...] * rs * ng_ref[...]).astype(o_ref.dtype)


def _ssd_call(pa, ps, conv_w, conv_b, dtb, alog, dskip, norm_g):
    bsz, s, _ = pa.shape
    L = SCAN_CHUNK
    full = lambda b, c: (0, 0)
    return pl.pallas_call(
        _ssd_kernel,
        grid=(bsz, s // L),
        in_specs=[pl.BlockSpec((1, L, SSM_XBC), lambda b, c: (b, c, 0)),
                  pl.BlockSpec((1, L, SSM_WIDTH), lambda b, c: (b, c, SSM_XBC // SSM_WIDTH)),
                  pl.BlockSpec((1, L, LANES), lambda b, c: (b, c, 0)),
                  pl.BlockSpec(conv_w.shape, full),
                  pl.BlockSpec(conv_b.shape, full),
                  pl.BlockSpec(dtb.shape, full),
                  pl.BlockSpec(alog.shape, full),
                  pl.BlockSpec(dskip.shape, full),
                  pl.BlockSpec(norm_g.shape, full)],
        out_specs=pl.BlockSpec((1, L, SSM_WIDTH), lambda b, c: (b, c, 0)),
        out_shape=jax.ShapeDtypeStruct((bsz, s, SSM_WIDTH), BF16),
        scratch_shapes=[pltpu.VMEM((L + 8, SSM_XBC), F32),
                        pltpu.VMEM((SSM_GROUPS, SSM_STATE, GROUP_WIDTH), F32),
                        pltpu.VMEM((L, SSM_WIDTH), F32)],
        compiler_params=_cparams(("parallel", "arbitrary")),
        name="ssd_scan",
    )(pa, pa, ps, conv_w, conv_b, dtb, alog, dskip, norm_g)


def _attn_kernel(lam_init, q_ref, k_ref, v_ref, z_ref, slope_ref, corr_ref, dl_ref, ng_ref, o_ref,
                 k_aug, vt_aug, acc1, acc2):
    T = ATT_BLOCK
    D2 = 2 * DIFF_HEAD_DIM
    qi = pl.program_id(2)
    n_kv = k_ref.shape[1] // T

    @pl.when(qi == 0)
    def _():
        lane = lax.broadcasted_iota(jnp.int32, (T, LANES), 1)
        row = lax.broadcasted_iota(jnp.int32, (T, LANES), 0)
        for c in range(n_kv):
            rows = slice(c * T, (c + 1) * T)
            pos = row + c * T
            pos_lo = jnp.bitwise_and(pos, POS_SPLIT - 1)
            pos_hi = pos - pos_lo
            pos_cols = jnp.where(lane < 3, pos_lo, jnp.where(lane < 6, pos_hi, 0))
            k_aug[rows, 0:D2] = k_ref[0, rows, :].astype(BF16)
            k_aug[rows, D2:ATT_AUG] = pos_cols.astype(F32).astype(BF16)
            vt_aug[0:D2, rows] = v_ref[0, rows, :].T.astype(BF16)
        ones_row = lax.broadcasted_iota(jnp.int32, (ATT_ACC_ROWS - D2, k_ref.shape[1]), 0) == 0
        vt_aug[D2:ATT_ACC_ROWS, :] = jnp.where(ones_row, 1.0, 0.0).astype(BF16)

    lane = lax.broadcasted_iota(jnp.int32, (T, D2), 1)
    qs = q_ref[0] * (DIFF_HEAD_DIM ** -0.5 * LOG2E)
    sl2 = slope_ref[0][:, 0:1] * LOG2E
    sl_hi = sl2.astype(BF16).astype(F32)
    sl_mid = (sl2 - sl_hi).astype(BF16).astype(F32)
    sl_lo = sl2 - sl_hi - sl_mid
    lane1 = lax.broadcasted_iota(jnp.int32, (1, LANES), 1)
    ext = jnp.where((lane1 == 0) | (lane1 == 3), sl_hi,
                    jnp.where((lane1 == 1) | (lane1 == 4), sl_mid,
                              jnp.where((lane1 == 2) | (lane1 == 5), sl_lo, 0.0)))
    ext = jnp.broadcast_to(ext, (T, LANES)).astype(BF16)
    qa1 = jnp.concatenate([jnp.where(lane < DIFF_HEAD_DIM, qs, 0.0).astype(BF16), ext], axis=1)
    qa2 = jnp.concatenate([jnp.where(lane >= DIFF_HEAD_DIM, qs, 0.0).astype(BF16), ext], axis=1)

    acc1[...] = jnp.zeros(acc1.shape, F32)
    acc2[...] = jnp.zeros(acc2.shape, F32)

    def block(first, n, ms, corr):
        qas, accs = (qa1, qa2), (acc1, acc2)
        starts = [pl.multiple_of((first + t) * T, T) for t in range(n)]
        stages = [list(range(i, min(i + ATT_STAGE, n))) for i in range(0, n, ATT_STAGE)]

        def score_step(t, m_run, out):
            kb = k_aug[pl.ds(starts[t], T), :]
            for a in range(2):
                s = _dot_nt(kb, qas[a])
                if corr is not None and t == n - 1:
                    s = s + corr
                out[a].append(s)
                m_run[a] = jnp.maximum(m_run[a], jnp.max(s, axis=0, keepdims=True))

        def value_step(t, idx, scores, m_stage, pv):
            vb = vt_aug[:, pl.ds(starts[t], T)]
            for a in range(2):
                d = _dot(vb, jnp.exp2(scores[a][idx] - m_stage[a]).astype(BF16))
                pv[a] = d if pv[a] is None else pv[a] + d

        m_prev = list(ms)
        pending = None
        for stage in stages + [None]:
            scores, m_run = [[], []], (list(pending[2]) if pending else list(ms))
            pv = [None, None]
            for idx in range(ATT_STAGE):
                if stage is not None and idx < len(stage):
                    score_step(stage[idx], m_run, scores)
                if pending is not None and idx < len(pending[0]):
                    value_step(pending[0][idx], idx, pending[1], pending[2], pv)
            if pending is not None:
                for a in range(2):
                    accs[a][...] = jnp.exp2(m_prev[a] - pending[2][a]) * accs[a][...] + pv[a]
                m_prev = list(pending[2])
            pending = (stage, scores, m_run) if stage is not None else None
        return tuple(m_prev)

    m0 = jnp.full((1, T), NEG_BIG, F32)
    ms = lax.fori_loop(0, qi // ATT_GROUP, lambda i, ms: block(ATT_GROUP * i, ATT_GROUP, ms, None), (m0, m0))
    corr = corr_ref[...] * sl2
    for left in range(ATT_GROUP):
        @pl.when(qi % ATT_GROUP == left)
        def _():
            block(qi - left, left + 1, ms, corr)

    dl = dl_ref[...]
    lam = (jnp.exp(jnp.sum(dl[0:1] * dl[1:2], axis=-1, keepdims=True))
           - jnp.exp(jnp.sum(dl[2:3] * dl[3:4], axis=-1, keepdims=True)) + lam_init)
    o_t = acc1[0:D2, :] / acc1[D2:D2 + 1, :] - lam * (acc2[0:D2, :] / acc2[D2:D2 + 1, :])
    o = o_t.T
    y = o * lax.rsqrt(jnp.mean(o * o, axis=-1, keepdims=True) + 1e-5) * ng_ref[...]
    y = y * (1.0 - lam_init)
    o_ref[0] = (y * _silu(z_ref[0])).astype(o_ref.dtype)


def _attn_corr_table():
    kk = jnp.arange(ATT_BLOCK)[:, None]
    qq = jnp.arange(ATT_BLOCK)[None, :]
    visible = (kk // CHUNK) <= (qq // CHUNK)
    return jnp.where(visible, -2.0 * jnp.maximum(kk - qq, 0).astype(F32), NEG_BIG)


def _attn_call(proj, slopes, dl, norm_g, lam_init):
    bsz, s, _ = proj.shape
    T = ATT_BLOCK
    hw = 2 * DIFF_HEAD_DIM
    blk = lambda off: off // hw
    const2 = lambda b, h, i: (0, 0)
    return pl.pallas_call(
        functools.partial(_attn_kernel, lam_init),
        grid=(bsz, DIFF_HEADS, s // T),
        in_specs=[pl.BlockSpec((1, T, hw), lambda b, h, i: (b, i, blk(P_OFF_QB) + h)),
                  pl.BlockSpec((1, s, hw), lambda b, h, i: (b, 0, blk(P_OFF_KB) + h)),
                  pl.BlockSpec((1, s, hw), lambda b, h, i: (b, 0, blk(P_OFF_VB) + h)),
                  pl.BlockSpec((1, T, hw), lambda b, h, i: (b, i, blk(P_OFF_ZB) + h)),
                  pl.BlockSpec((1, 1, LANES), lambda b, h, i: (h, 0, 0)),
                  pl.BlockSpec((T, T), const2),
                  pl.BlockSpec(dl.shape, const2),
                  pl.BlockSpec(norm_g.shape, const2)],
        out_specs=pl.BlockSpec((1, T, hw), lambda b, h, i: (b, i, h)),
        out_shape=jax.ShapeDtypeStruct((bsz, s, DIFF_WIDTH), BF16),
        scratch_shapes=[pltpu.VMEM((s, ATT_AUG), BF16),
                        pltpu.VMEM((ATT_ACC_ROWS, s), BF16),
                        pltpu.VMEM((ATT_ACC_ROWS, T), F32),
                        pltpu.VMEM((ATT_ACC_ROWS, T), F32)],
        compiler_params=_cparams(("parallel", "parallel", "arbitrary")),
        name="diff_attention",
    )(proj, proj, proj, proj, slopes, _attn_corr_table(), dl, norm_g)


def _mlstm_kernel(q_ref, k_ref, v_ref, o_ref_in, z_ref, ps_ref, gb_ref, out_ref, c_state, m_state):
    L = SCAN_CHUNK
    D = MLSTM_HEAD_DIM
    c = pl.program_id(1)

    @pl.when(c == 0)
    def _():
        c_state[...] = jnp.zeros(c_state.shape, F32)
        m_state[...] = jnp.zeros(m_state.shape, F32)

    ones_col = jnp.where(lax.broadcasted_iota(jnp.int32, (L, D), 1) == 0, 1.0, 0.0)
    rep = lambda col: jnp.broadcast_to(col, (col.shape[0], LANES))
    twice = lambda x: jnp.concatenate([x, x], axis=1)

    rows = lax.broadcasted_iota(jnp.int32, (L, L), 0)
    cols = lax.broadcasted_iota(jnp.int32, (L, L), 1)
    causal = rows >= cols
    tril = jnp.where(causal, 1.0, 0.0).astype(BF16)

    raw = ps_ref[0] + gb_ref[...]
    logf = -_softplus(-raw)
    bcum = _cumsum_rows(logf, tril)
    bcum_t = bcum.T
    raw_t = raw.T

    for h in range(MLSTM_HEADS):
        li, lf = LANE_I + h, LANE_F + h
        sl = slice(h * D, (h + 1) * D)
        q = q_ref[0, :, sl]
        ks = k_ref[0, :, sl] * (D ** -0.5)
        v_ext = jnp.concatenate([v_ref[0, :, sl], ones_col], axis=1)
        qb, kb = q.astype(BF16), ks.astype(BF16)

        b_rep = rep(bcum[:, lf:lf + 1])
        ig_rep = rep(raw[:, li:li + 1])
        b_row = bcum_t[lf:lf + 1, :]
        ig_row = raw_t[li:li + 1, :]
        btot = b_rep[L - 1:L, :]
        m_prev = m_state[h:h + 1, :]
        c_prev = c_state[h]

        dmat = jnp.where(causal, twice(b_rep) - b_row + ig_row, NEG_BIG)
        m_inter = b_rep + m_prev
        m_row = jnp.maximum(m_inter, rep(jnp.max(dmat, axis=-1, keepdims=True)))
        wts = jnp.exp(dmat - twice(m_row))
        sw = wts * _dot_nt(qb, kb)
        a_inter = jnp.exp(m_inter - m_row)
        nd = _dot(sw.astype(BF16), v_ext.astype(BF16)) + _dot(qb, c_prev.astype(BF16)) * twice(a_inter)
        hout = nd[:, :D] / jnp.maximum(jnp.abs(rep(nd[:, D:D + 1])), jnp.exp(-m_row))
        gate = jax.nn.sigmoid(o_ref_in[0, :, sl]) * _silu(z_ref[0, :, sl])
        out_ref[0, :, sl] = (gate * hout).astype(out_ref.dtype)

        w_end = btot - b_rep + ig_rep
        m_loc = jnp.max(w_end, axis=0, keepdims=True)
        m_new = jnp.maximum(btot + m_prev, m_loc)
        a_old = jnp.exp(btot + m_prev - m_new)
        e_end = jnp.exp(w_end - m_new)
        c_state[h] = twice(a_old) * c_prev + _dot_tn(kb, (v_ext * twice(e_end)).astype(BF16))
        m_state[h:h + 1, :] = m_new


def _mlstm_call(proj, ps, gate_bias):
    bsz, s, _ = proj.shape
    L = SCAN_CHUNK
    W = MLSTM_WIDTH
    blk0 = P_OFF_C // W
    return pl.pallas_call(
        _mlstm_kernel,
        grid=(bsz, s // L),
        in_specs=[pl.BlockSpec((1, L, W), lambda b, c: (b, c, blk0)),
                  pl.BlockSpec((1, L, W), lambda b, c: (b, c, blk0 + 1)),
                  pl.BlockSpec((1, L, W), lambda b, c: (b, c, blk0 + 2)),
                  pl.BlockSpec((1, L, W), lambda b, c: (b, c, blk0 + 3)),
                  pl.BlockSpec((1, L, W), lambda b, c: (b, c, blk0 + 4)),
                  pl.BlockSpec((1, L, LANES), lambda b, c: (b, c, 0)),
                  pl.BlockSpec(gate_bias.shape, lambda b, c: (0, 0))],
        out_specs=pl.BlockSpec((1, L, W), lambda b, c: (b, c, 0)),
        out_shape=jax.ShapeDtypeStruct((bsz, s, W), BF16),
        scratch_shapes=[pltpu.VMEM((MLSTM_HEADS, MLSTM_HEAD_DIM, 2 * MLSTM_HEAD_DIM), F32),
                        pltpu.VMEM((MLSTM_HEADS, LANES), F32)],
        compiler_params=_cparams(("parallel", "arbitrary")),
        name="mlstm_scan",
    )(proj, proj, proj, proj, proj, ps, gate_bias)


def _merge_kernel(x_ref, oa_ref, ob_ref, oc_ref, wga_ref, wgb_ref, wgc_ref, wa_ref, wb_ref, wc_ref, o_ref):
    x = x_ref[...]
    acc = jax.nn.sigmoid(_dot_nt(x, wga_ref[...])) * _dot(oa_ref[...], wa_ref[...])
    acc = acc + jax.nn.sigmoid(_dot_nt(x, wgb_ref[...])) * _dot(ob_ref[...], wb_ref[...])
    acc = acc + jax.nn.sigmoid(_dot_nt(x, wgc_ref[...])) * _dot(oc_ref[...], wc_ref[...])
    o_ref[...] = acc.astype(o_ref.dtype)


def _merge_call(xb, oa, ob, oc, w_all, w_br, li, tm, tn):
    m = xb.shape[0]
    row = lambda i, j: (i, 0)
    gate_spec = lambda br: pl.BlockSpec(
        (None, tn, D_MODEL), lambda i, j: (li, (W_OFF_GATES + br * D_MODEL) // tn + j, 0))
    return pl.pallas_call(
        _merge_kernel,
        grid=(m // tm, D_MODEL // tn),
        in_specs=[pl.BlockSpec((tm, D_MODEL), row),
                  pl.BlockSpec((tm, SSM_WIDTH), row, pipeline_mode=pl.Buffered(1)),
                  pl.BlockSpec((tm, DIFF_WIDTH), row, pipeline_mode=pl.Buffered(1)),
                  pl.BlockSpec((tm, MLSTM_WIDTH), row, pipeline_mode=pl.Buffered(1)),
                  gate_spec(0), gate_spec(1), gate_spec(2),
                  pl.BlockSpec((None, SSM_WIDTH, tn), lambda i, j: (li, 0, j)),
                  pl.BlockSpec((None, DIFF_WIDTH, tn), lambda i, j: (li, SSM_WIDTH // DIFF_WIDTH, j)),
                  pl.BlockSpec((None, MLSTM_WIDTH, tn),
                               lambda i, j: (li, (SSM_WIDTH + DIFF_WIDTH) // MLSTM_WIDTH, j))],
        out_specs=pl.BlockSpec((tm, tn), lambda i, j: (i, j)),
        out_shape=jax.ShapeDtypeStruct((m, D_MODEL), BF16),
        compiler_params=_cparams(("parallel", "arbitrary")),
        name="gated_merge",
    )(xb, oa, ob, oc, w_all, w_all, w_all, w_br, w_br, w_br)


def _final_kernel(nj, tn, mg_ref, wo_ref, wg_ref, x_ref, p_ref, wple_ref, lng_ref, lnb_ref, pg_ref,
                  o_ref, ob_ref, h_scr, hb_scr, st_scr):
    j = pl.program_id(1)
    tm = h_scr.shape[0]

    @pl.when(j == 0)
    def _():
        st_scr[...] = jnp.zeros(st_scr.shape, F32)

    @pl.when(j < nj)
    def _():
        off = pl.multiple_of(j * tn, tn)
        pre = DEEPNORM_ALPHA * x_ref[...] + _dot(mg_ref[...], wo_ref[...])
        h_scr[:, pl.ds(off, tn)] = pre
        st_scr[:, 0:LANES] += jnp.sum(pre, axis=-1, keepdims=True)

    @pl.when(j == nj)
    def _():
        def norm_rows(r, carry):
            rows = pl.ds(pl.multiple_of(r * NORM_ROWS, NORM_ROWS), NORM_ROWS)
            cen = h_scr[rows, :] - st_scr[rows, 0:1] * (1.0 / D_MODEL)
            var = jnp.mean(cen * cen, axis=-1, keepdims=True)
            h = cen * lax.rsqrt(var + 1e-5) * lng_ref[...] + lnb_ref[...]
            h_scr[rows, :] = h
            hb_scr[rows, :] = h.astype(BF16)
            e = _dot(p_ref[rows, :], wple_ref[...])
            st_scr[rows, LANES:2 * LANES] = jnp.broadcast_to(
                jnp.sum(e * e, axis=-1, keepdims=True), (NORM_ROWS, LANES))
            return carry

        lax.fori_loop(0, tm // NORM_ROWS, norm_rows, 0)

    @pl.when(j >= nj)
    def _():
        off = pl.multiple_of((j - nj) * tn, tn)
        gate = jax.nn.sigmoid(_dot(hb_scr[...], wg_ref[...]))
        rs = lax.rsqrt(st_scr[:, LANES:LANES + 1] * (1.0 / D_MODEL) + 1e-5)
        e = _dot(p_ref[...], wple_ref[:, pl.ds(off, tn)]) * rs * pg_ref[:, pl.ds(off, tn)]
        out = h_scr[:, pl.ds(off, tn)] + gate * e
        o_ref[...] = out
        ob_ref[...] = out.astype(BF16)


def _final_call(merged, w_out, w_pg, x, pb, w_ple, ln_g, ln_b, ple_g, li, tm, tn):
    m = merged.shape[0]
    nj = D_MODEL // tn
    full = lambda i, j: (0, 0)
    out_idx = lambda i, j: (i, jnp.maximum(j - nj, 0))
    return pl.pallas_call(
        functools.partial(_final_kernel, nj, tn),
        grid=(m // tm, 2 * nj),
        in_specs=[pl.BlockSpec((tm, D_MODEL), lambda i, j: (i, 0), pipeline_mode=pl.Buffered(1)),
                  pl.BlockSpec((None, D_MODEL, tn), lambda i, j: (li, 0, jnp.minimum(j, nj - 1))),
                  pl.BlockSpec((None, D_MODEL, tn), lambda i, j: (li, 0, jnp.maximum(j - nj, 0))),
                  pl.BlockSpec((tm, tn), lambda i, j: (i, jnp.minimum(j, nj - 1))),
                  pl.BlockSpec((tm, PLE_DIM), lambda i, j: (i, 0)),
                  pl.BlockSpec((None,) + w_ple.shape[1:], lambda i, j: (li, 0, 0), pipeline_mode=pl.Buffered(1)),
                  pl.BlockSpec(ln_g.shape, full),
                  pl.BlockSpec(ln_b.shape, full),
                  pl.BlockSpec(ple_g.shape, full)],
        out_specs=[pl.BlockSpec((tm, tn), out_idx), pl.BlockSpec((tm, tn), out_idx)],
        out_shape=[jax.ShapeDtypeStruct((m, D_MODEL), F32), jax.ShapeDtypeStruct((m, D_MODEL), BF16)],
        scratch_shapes=[pltpu.VMEM((tm, D_MODEL), F32),
                        pltpu.VMEM((tm, D_MODEL), BF16),
                        pltpu.VMEM((tm, 2 * LANES), F32)],
        compiler_params=_cparams(("parallel", "arbitrary")),
        name="outproj_deepnorm_ple",
    )(merged, w_out, w_pg, x, pb, w_ple, ln_g, ln_b, ple_g)


def _row_tile(m, pref):
    return pref if m % pref == 0 else m


def _relayout_kernel(a_ref, dt_ref, if_ref, o_ref, os_ref):
    o_ref[...] = a_ref[0].astype(BF16)

    @pl.when(pl.program_id(1) == 0)
    def _():
        os_ref[LANE_DT:LANE_I, :] = dt_ref[0].astype(BF16)
        os_ref[LANE_I:LANE_F + MLSTM_HEADS, :] = if_ref[0].astype(BF16)
        os_ref[LANE_F + MLSTM_HEADS:, :] = jnp.zeros((LANES - LANE_F - MLSTM_HEADS, os_ref.shape[1]), BF16)


def _relayout_call(w_t, rows):
    depth, _, d = w_t.shape

    def src_row(j):
        r = j * rows
        shift = jnp.where(r < P_OFF_QB, 0, jnp.where(r < W_OFF_GATES, (OFF_QB - P_OFF_QB) // F32_SUBLANES,
                                                     (OFF_GATES - W_OFF_GATES) // F32_SUBLANES))
        return (j * (rows // F32_SUBLANES) + shift) * F32_SUBLANES

    return pl.pallas_call(
        _relayout_kernel,
        grid=(depth, W_OFF_SMALL // rows),
        in_specs=[pl.BlockSpec((pl.Element(1), pl.Element(rows), pl.Element(d)), lambda l, j: (l, src_row(j), 0)),
                  pl.BlockSpec((pl.Element(1), pl.Element(SSM_HEADS), pl.Element(d)), lambda l, j: (l, OFF_DT, 0)),
                  pl.BlockSpec((pl.Element(1), pl.Element(2 * MLSTM_HEADS), pl.Element(d)),
                               lambda l, j: (l, OFF_IC, 0))],
        out_specs=[pl.BlockSpec((None, rows, d), lambda l, j: (l, j, 0)),
                   pl.BlockSpec((None, LANES, d), lambda l, j: (l, 0, 0))],
        out_shape=[jax.ShapeDtypeStruct((depth, W_OFF_SMALL, d), BF16),
                   jax.ShapeDtypeStruct((depth, LANES, d), BF16)],
        compiler_params=_cparams(("parallel", "arbitrary")),
        name="weight_relayout",
    )(w_t, w_t, w_t)


def _prep_weights(w_in, w_branch, w_out, w_ple, w_ple_gate):
    w_main, w_small = _relayout_call(jnp.swapaxes(w_in, 1, 2), RELAYOUT_ROWS)
    return (w_main, w_small, w_branch.astype(BF16), w_out.astype(BF16), w_ple.astype(BF16),
            w_ple_gate.astype(BF16))


def _layer(x, xb, p_i, li, weights, conv_w, conv_b, dt_bias, a_log, d_skip, ssm_norm_g,
           diff_lambda, diff_norm_g, mlstm_gate_b, ln_g, ln_b, ple_norm_g):
    w_all, w_small, w_br, w_out, w_ple, w_pg = weights
    bsz, s, _ = x.shape
    t = bsz * s
    x2 = x.reshape(t, D_MODEL)
    xb2 = xb.reshape(t, D_MODEL)

    tm = _row_tile(t, ROW_TILE)
    proj, ps = _proj_call(xb2, w_all, w_small, li, tm, PROJ_COL_TILE)
    proj = proj.reshape(bsz, s, PROJ_WIDTH)
    ps = ps.reshape(bsz, s, LANES)

    pad = lambda v, lane0: jnp.zeros((1, LANES), F32).at[0, lane0:lane0 + v.shape[0]].set(v.astype(F32))
    out_a = _ssd_call(proj, ps, conv_w, conv_b.reshape(1, -1), pad(dt_bias, LANE_DT), pad(a_log, LANE_DT),
                      jnp.repeat(d_skip.astype(F32), SSM_HEAD_DIM).reshape(1, -1), ssm_norm_g.reshape(1, -1))

    lam_init = 0.8 - 0.6 * math.exp(-0.3 * li)
    slopes = 2.0 ** (-8.0 * jnp.arange(1, DIFF_HEADS + 1, dtype=F32) / DIFF_HEADS)
    slopes = jnp.broadcast_to(slopes[:, None, None], (DIFF_HEADS, 1, LANES))
    out_b = _attn_call(proj, slopes, diff_lambda, diff_norm_g.reshape(1, -1), lam_init)

    gate_bias = pad(mlstm_gate_b[0], LANE_I) + pad(mlstm_gate_b[1], LANE_F)
    out_c = _mlstm_call(proj, ps, gate_bias)

    merged = _merge_call(xb2, out_a.reshape(t, -1), out_b.reshape(t, -1), out_c.reshape(t, -1),
                         w_all, w_br, li, tm, EPILOGUE_COL_TILE)
    out, out_b16 = _final_call(merged, w_out, w_pg, x2, p_i.reshape(t, PLE_DIM).astype(BF16), w_ple,
                               ln_g.reshape(1, -1), ln_b.reshape(1, -1), ple_norm_g.reshape(1, -1),
                               li, tm, EPILOGUE_COL_TILE)
    return out.reshape(bsz, s, D_MODEL), out_b16.reshape(bsz, s, D_MODEL)


def kernel(x, p, w_in, conv_w, conv_b, dt_bias, a_log, d_skip, ssm_norm_g, diff_lambda, diff_norm_g,
           mlstm_gate_b, w_branch, w_out, ln_g, ln_b, w_ple, ple_norm_g, w_ple_gate):
    assert x.shape[-1] == D_MODEL and w_in.shape == (DEPTH, D_MODEL, OFF_GATES + N_BRANCH * D_MODEL)
    weights = _prep_weights(w_in, w_branch, w_out, w_ple, w_ple_gate)
    h = x
    hb = x.astype(BF16)
    for li in range(w_in.shape[0]):
        h, hb = _layer(h, hb, p[li], li, weights, conv_w[li], conv_b[li], dt_bias[li], a_log[li],
                       d_skip[li], ssm_norm_g[li], diff_lambda[li], diff_norm_g[li],
                       mlstm_gate_b[li], ln_g[li], ln_b[li], ple_norm_g[li])
    return h
```

```python
import functools
import math

import jax
import jax.numpy as jnp
from jax import lax
from jax.experimental import pallas as pl
from jax.experimental.pallas import tpu as pltpu

F32 = jnp.float32
BF16 = jnp.bfloat16

V7X_VMEM_LIMIT_BYTES = 56 * 1024 * 1024
LANES = 128
F32_SUBLANES = 8

D_MODEL = 4096
CHUNK = 64
SSM_WIDTH = 2048
SSM_HEAD_DIM = 64
SSM_HEADS = 32
SSM_GROUPS = 8
SSM_STATE = 128
SSM_XBC = SSM_WIDTH + 2 * SSM_GROUPS * SSM_STATE
GROUP_WIDTH = SSM_WIDTH // SSM_GROUPS
HEADS_PER_GROUP = SSM_HEADS // SSM_GROUPS
DIFF_WIDTH = 1024
DIFF_HEAD_DIM = 64
DIFF_HEADS = 8
MLSTM_WIDTH = 1024
MLSTM_HEAD_DIM = 128
MLSTM_HEADS = 8
N_BRANCH = 3
PLE_DIM = 256
DEPTH = 2
DEEPNORM_ALPHA = (2.0 * DEPTH) ** 0.25

OFF_XBC = 0
OFF_ZA = OFF_XBC + SSM_XBC
OFF_DT = OFF_ZA + SSM_WIDTH
OFF_QB = OFF_DT + SSM_HEADS
OFF_ZB = OFF_QB + 3 * DIFF_WIDTH
OFF_QC = OFF_ZB + DIFF_WIDTH
OFF_IC = OFF_QC + 5 * MLSTM_WIDTH
OFF_FC = OFF_IC + MLSTM_HEADS
OFF_GATES = OFF_FC + MLSTM_HEADS

P_OFF_XBC = 0
P_OFF_ZA = P_OFF_XBC + SSM_XBC
P_OFF_QB = P_OFF_ZA + SSM_WIDTH
P_OFF_KB = P_OFF_QB + DIFF_WIDTH
P_OFF_VB = P_OFF_KB + DIFF_WIDTH
P_OFF_ZB = P_OFF_VB + DIFF_WIDTH
P_OFF_C = P_OFF_ZB + DIFF_WIDTH
PROJ_WIDTH = P_OFF_C + 5 * MLSTM_WIDTH
W_OFF_GATES = PROJ_WIDTH
W_OFF_SMALL = W_OFF_GATES + N_BRANCH * D_MODEL

LANE_DT = 0
LANE_I = SSM_HEADS
LANE_F = SSM_HEADS + MLSTM_HEADS

ROW_TILE = 1024
PROJ_COL_TILE = 1024
EPILOGUE_COL_TILE = 256
RELAYOUT_ROWS = 512
SCAN_CHUNK = 2 * LANES
ATT_BLOCK = 512
ATT_GROUP = 4
ATT_STAGE = 1
ATT_AUG = 2 * LANES
ATT_ACC_ROWS = LANES + 16
POS_SPLIT = 256
NORM_ROWS = 128
NORM_COLS = 1024
NEG_BIG = -1e30
LOG2E = 1.4426950408889634


def _cparams(sem):
    return pltpu.CompilerParams(dimension_semantics=sem, vmem_limit_bytes=V7X_VMEM_LIMIT_BYTES)


def _silu(x):
    return x * jax.nn.sigmoid(x)


def _softplus(x):
    return jnp.maximum(x, 0.0) + jnp.log1p(jnp.exp(-jnp.abs(x)))


def _dot(a, b):
    return jnp.dot(a, b, preferred_element_type=F32)


def _dot_nt(a, b):
    return lax.dot_general(a, b, (((1,), (1,)), ((), ())), preferred_element_type=F32)


def _dot_tn(a, b):
    return lax.dot_general(a, b, (((0,), (0,)), ((), ())), preferred_element_type=F32)


def _cumsum_rows(x, tril_bf16):
    hi = x.astype(BF16)
    r1 = x - hi.astype(F32)
    mid = r1.astype(BF16)
    lo = (r1 - mid.astype(F32)).astype(BF16)
    return _dot(tril_bf16, hi) + _dot(tril_bf16, mid) + _dot(tril_bf16, lo)


def _proj_kernel(x_ref, w_ref, ws_ref, o_ref, os_ref):
    o_ref[...] = _dot_nt(x_ref[...], w_ref[...])

    @pl.when(pl.program_id(1) == 0)
    def _():
        os_ref[...] = _dot_nt(x_ref[...], ws_ref[...])


def _proj_call(xb, w_main, w_small, li, tm, tn):
    m, k = xb.shape
    return pl.pallas_call(
        _proj_kernel,
        grid=(m // tm, PROJ_WIDTH // tn),
        in_specs=[pl.BlockSpec((tm, k), lambda i, j: (i, 0)),
                  pl.BlockSpec((None, tn, k), lambda i, j: (li, j, 0)),
                  pl.BlockSpec((None, LANES, k), lambda i, j: (li, 0, 0))],
        out_specs=[pl.BlockSpec((tm, tn), lambda i, j: (i, j)),
                   pl.BlockSpec((tm, LANES), lambda i, j: (i, 0))],
        out_shape=[jax.ShapeDtypeStruct((m, PROJ_WIDTH), F32), jax.ShapeDtypeStruct((m, LANES), F32)],
        compiler_params=_cparams(("parallel", "arbitrary")),
        name="in_proj",
    )(xb, w_main, w_small)


def _ssd_kernel(xbc_ref, z_ref, ps_ref, cw_ref, cb_ref, dtb_ref, alog_ref, dskip_ref, ng_ref,
                o_ref, xpad, state, ybuf):
    L = SCAN_CHUNK
    c = pl.program_id(1)

    @pl.when(c == 0)
    def _():
        xpad[0:8, :] = jnp.zeros((8, SSM_XBC), F32)
        state[...] = jnp.zeros(state.shape, F32)

    @pl.when(c > 0)
    def _():
        xpad[0:8, :] = xpad[L:L + 8, :]

    xpad[8:8 + L, :] = xbc_ref[0]

    def conv_silu(c0, width):
        acc = cb_ref[:, c0:c0 + width] + cw_ref[3:4, c0:c0 + width] * xpad[8:8 + L, c0:c0 + width]
        acc = acc + cw_ref[2:3, c0:c0 + width] * xpad[7:7 + L, c0:c0 + width]
        acc = acc + cw_ref[1:2, c0:c0 + width] * xpad[6:6 + L, c0:c0 + width]
        acc = acc + cw_ref[0:1, c0:c0 + width] * xpad[5:5 + L, c0:c0 + width]
        return _silu(acc)

    rows = lax.broadcasted_iota(jnp.int32, (L, L), 0)
    cols = lax.broadcasted_iota(jnp.int32, (L, L), 1)
    causal = rows >= cols
    tril = jnp.where(causal, 1.0, 0.0).astype(BF16)

    dt = _softplus(ps_ref[0] + dtb_ref[...])
    la = dt * (-jnp.exp(alog_ref[...]))
    lc = _cumsum_rows(la, tril)
    lc_t = lc.T

    head_of_lane = lax.broadcasted_iota(jnp.int32, (L, GROUP_WIDTH), 1) // SSM_HEAD_DIM

    def expand(vals, hol):
        out = vals[HEADS_PER_GROUP - 1]
        for r in range(HEADS_PER_GROUP - 2, -1, -1):
            out = jnp.where(hol == r, vals[r], out)
        return out

    ssq = jnp.zeros((L, 1), F32)
    for g in range(SSM_GROUPS):
        h0 = g * HEADS_PER_GROUP
        xg = conv_silu(g * GROUP_WIDTH, GROUP_WIDTH)
        bg = conv_silu(SSM_WIDTH + g * SSM_STATE, SSM_STATE).astype(BF16)
        cg = conv_silu(SSM_WIDTH + SSM_GROUPS * SSM_STATE + g * SSM_STATE, SSM_STATE).astype(BF16)

        dt4 = expand([dt[:, h0 + r:h0 + r + 1] for r in range(HEADS_PER_GROUP)], head_of_lane)
        lc4 = expand([lc[:, h0 + r:h0 + r + 1] for r in range(HEADS_PER_GROUP)], head_of_lane)
        lc_end4 = expand([lc[L - 1:L, h0 + r:h0 + r + 1] for r in range(HEADS_PER_GROUP)],
                         head_of_lane[0:1, :])
        xdt = xg * dt4

        cb = _dot_nt(cg, bg)
        y = jnp.zeros((L, GROUP_WIDTH), F32)
        for r in range(HEADS_PER_GROUP):
            h = h0 + r
            seg = lc[:, h:h + 1] - lc_t[h:h + 1, :]
            decay = jnp.where(causal, jnp.exp(seg), 0.0)
            m_r = (cb * decay).astype(BF16)
            x_r = jnp.where(head_of_lane == r, xdt, 0.0).astype(BF16)
            y = y + _dot(m_r, x_r)

        prev = state[g]
        y = y + _dot(cg, prev.astype(BF16)) * jnp.exp(lc4)
        xs = (xdt * jnp.exp(lc_end4 - lc4)).astype(BF16)
        state[g] = prev * jnp.exp(lc_end4) + _dot_tn(bg, xs)

        y = y + dskip_ref[:, g * GROUP_WIDTH:(g + 1) * GROUP_WIDTH] * xg
        yz = y * _silu(z_ref[0, :, g * GROUP_WIDTH:(g + 1) * GROUP_WIDTH])
        ybuf[:, g * GROUP_WIDTH:(g + 1) * GROUP_WIDTH] = yz
        ssq = ssq + jnp.sum(yz * yz, axis=-1, keepdims=True)

    rs = lax.rsqrt(ssq * (1.0 / SSM_WIDTH) + 1e-5)
    o_ref[0] = (ybuf[...] * rs * ng_ref[...]).astype(o_ref.dtype)


def _ssd_call(pa, ps, conv_w, conv_b, dtb, alog, dskip, norm_g):
    bsz, s, _ = pa.shape
    L = SCAN_CHUNK
    full = lambda b, c: (0, 0)
    return pl.pallas_call(
        _ssd_kernel,
        grid=(bsz, s // L),
        in_specs=[pl.BlockSpec((1, L, SSM_XBC), lambda b, c: (b, c, 0)),
                  pl.BlockSpec((1, L, SSM_WIDTH), lambda b, c: (b, c, SSM_XBC // SSM_WIDTH)),
                  pl.BlockSpec((1, L, LANES), lambda b, c: (b, c, 0)),
                  pl.BlockSpec(conv_w.shape, full),
                  pl.BlockSpec(conv_b.shape, full),
                  pl.BlockSpec(dtb.shape, full),
                  pl.BlockSpec(alog.shape, full),
                  pl.BlockSpec(dskip.shape, full),
                  pl.BlockSpec(norm_g.shape, full)],
        out_specs=pl.BlockSpec((1, L, SSM_WIDTH), lambda b, c: (b, c, 0)),
        out_shape=jax.ShapeDtypeStruct((bsz, s, SSM_WIDTH), BF16),
        scratch_shapes=[pltpu.VMEM((L + 8, SSM_XBC), F32),
                        pltpu.VMEM((SSM_GROUPS, SSM_STATE, GROUP_WIDTH), F32),
                        pltpu.VMEM((L, SSM_WIDTH), F32)],
        compiler_params=_cparams(("parallel", "arbitrary")),
        name="ssd_scan",
    )(pa, pa, ps, conv_w, conv_b, dtb, alog, dskip, norm_g)


def _attn_kernel(lam_init, q_ref, k_ref, v_ref, z_ref, slope_ref, corr_ref, dl_ref, ng_ref, o_ref,
                 k_aug, vt_aug, acc1, acc2):
    T = ATT_BLOCK
    D2 = 2 * DIFF_HEAD_DIM
    qi = pl.program_id(2)
    n_kv = k_ref.shape[1] // T

    @pl.when(qi == 0)
    def _():
        lane = lax.broadcasted_iota(jnp.int32, (T, LANES), 1)
        row = lax.broadcasted_iota(jnp.int32, (T, LANES), 0)
        for c in range(n_kv):
            rows = slice(c * T, (c + 1) * T)
            pos = row + c * T
            pos_lo = jnp.bitwise_and(pos, POS_SPLIT - 1)
            pos_hi = pos - pos_lo
            pos_cols = jnp.where(lane < 3, pos_lo, jnp.where(lane < 6, pos_hi, 0))
            k_aug[rows, 0:D2] = k_ref[0, rows, :].astype(BF16)
            k_aug[rows, D2:ATT_AUG] = pos_cols.astype(F32).astype(BF16)
            vt_aug[0:D2, rows] = v_ref[0, rows, :].T.astype(BF16)
        ones_row = lax.broadcasted_iota(jnp.int32, (ATT_ACC_ROWS - D2, k_ref.shape[1]), 0) == 0
        vt_aug[D2:ATT_ACC_ROWS, :] = jnp.where(ones_row, 1.0, 0.0).astype(BF16)

    lane = lax.broadcasted_iota(jnp.int32, (T, D2), 1)
    qs = q_ref[0] * (DIFF_HEAD_DIM ** -0.5 * LOG2E)
    sl2 = slope_ref[0][:, 0:1] * LOG2E
    sl_hi = sl2.astype(BF16).astype(F32)
    sl_mid = (sl2 - sl_hi).astype(BF16).astype(F32)
    sl_lo = sl2 - sl_hi - sl_mid
    lane1 = lax.broadcasted_iota(jnp.int32, (1, LANES), 1)
    ext = jnp.where((lane1 == 0) | (lane1 == 3), sl_hi,
                    jnp.where((lane1 == 1) | (lane1 == 4), sl_mid,
                              jnp.where((lane1 == 2) | (lane1 == 5), sl_lo, 0.0)))
    ext = jnp.broadcast_to(ext, (T, LANES)).astype(BF16)
    qa1 = jnp.concatenate([jnp.where(lane < DIFF_HEAD_DIM, qs, 0.0).astype(BF16), ext], axis=1)
    qa2 = jnp.concatenate([jnp.where(lane >= DIFF_HEAD_DIM, qs, 0.0).astype(BF16), ext], axis=1)

    acc1[...] = jnp.zeros(acc1.shape, F32)
    acc2[...] = jnp.zeros(acc2.shape, F32)

    def block(first, n, ms, corr):
        qas, accs = (qa1, qa2), (acc1, acc2)
        starts = [pl.multiple_of((first + t) * T, T) for t in range(n)]
        stages = [list(range(i, min(i + ATT_STAGE, n))) for i in range(0, n, ATT_STAGE)]

        def score_step(t, m_run, out):
            kb = k_aug[pl.ds(starts[t], T), :]
            for a in range(2):
                s = _dot_nt(kb, qas[a])
                if corr is not None and t == n - 1:
                    s = s + corr
                out[a].append(s)
                m_run[a] = jnp.maximum(m_run[a], jnp.max(s, axis=0, keepdims=True))

        def value_step(t, idx, scores, m_stage, pv):
            vb = vt_aug[:, pl.ds(starts[t], T)]
            for a in range(2):
                d = _dot(vb, jnp.exp2(scores[a][idx] - m_stage[a]).astype(BF16))
                pv[a] = d if pv[a] is None else pv[a] + d

        m_prev = list(ms)
        pending = None
        for stage in stages + [None]:
            scores, m_run = [[], []], (list(pending[2]) if pending else list(ms))
            pv = [None, None]
            for idx in range(ATT_STAGE):
                if stage is not None and idx < len(stage):
                    score_step(stage[idx], m_run, scores)
                if pending is not None and idx < len(pending[0]):
                    value_step(pending[0][idx], idx, pending[1], pending[2], pv)
            if pending is not None:
                for a in range(2):
                    accs[a][...] = jnp.exp2(m_prev[a] - pending[2][a]) * accs[a][...] + pv[a]
                m_prev = list(pending[2])
            pending = (stage, scores, m_run) if stage is not None else None
        return tuple(m_prev)

    m0 = jnp.full((1, T), NEG_BIG, F32)
    ms = lax.fori_loop(0, qi // ATT_GROUP, lambda i, ms: block(ATT_GROUP * i, ATT_GROUP, ms, None), (m0, m0))
    corr = corr_ref[...] * sl2
    for left in range(ATT_GROUP):
        @pl.when(qi % ATT_GROUP == left)
        def _():
            block(qi - left, left + 1, ms, corr)

    dl = dl_ref[...]
    lam = (jnp.exp(jnp.sum(dl[0:1] * dl[1:2], axis=-1, keepdims=True))
           - jnp.exp(jnp.sum(dl[2:3] * dl[3:4], axis=-1, keepdims=True)) + lam_init)
    o_t = acc1[0:D2, :] / acc1[D2:D2 + 1, :] - lam * (acc2[0:D2, :] / acc2[D2:D2 + 1, :])
    o = o_t.T
    y = o * lax.rsqrt(jnp.mean(o * o, axis=-1, keepdims=True) + 1e-5) * ng_ref[...]
    y = y * (1.0 - lam_init)
    o_ref[0] = (y * _silu(z_ref[0])).astype(o_ref.dtype)


def _attn_corr_table():
    kk = jnp.arange(ATT_BLOCK)[:, None]
    qq = jnp.arange(ATT_BLOCK)[None, :]
    visible = (kk // CHUNK) <= (qq // CHUNK)
    return jnp.where(visible, -2.0 * jnp.maximum(kk - qq, 0).astype(F32), NEG_BIG)


def _attn_call(proj, slopes, dl, norm_g, lam_init):
    bsz, s, _ = proj.shape
    T = ATT_BLOCK
    hw = 2 * DIFF_HEAD_DIM
    blk = lambda off: off // hw
    const2 = lambda b, h, i: (0, 0)
    return pl.pallas_call(
        functools.partial(_attn_kernel, lam_init),
        grid=(bsz, DIFF_HEADS, s // T),
        in_specs=[pl.BlockSpec((1, T, hw), lambda b, h, i: (b, i, blk(P_OFF_QB) + h)),
                  pl.BlockSpec((1, s, hw), lambda b, h, i: (b, 0, blk(P_OFF_KB) + h)),
                  pl.BlockSpec((1, s, hw), lambda b, h, i: (b, 0, blk(P_OFF_VB) + h)),
                  pl.BlockSpec((1, T, hw), lambda b, h, i: (b, i, blk(P_OFF_ZB) + h)),
                  pl.BlockSpec((1, 1, LANES), lambda b, h, i: (h, 0, 0)),
                  pl.BlockSpec((T, T), const2),
                  pl.BlockSpec(dl.shape, const2),
                  pl.BlockSpec(norm_g.shape, const2)],
        out_specs=pl.BlockSpec((1, T, hw), lambda b, h, i: (b, i, h)),
        out_shape=jax.ShapeDtypeStruct((bsz, s, DIFF_WIDTH), BF16),
        scratch_shapes=[pltpu.VMEM((s, ATT_AUG), BF16),
                        pltpu.VMEM((ATT_ACC_ROWS, s), BF16),
                        pltpu.VMEM((ATT_ACC_ROWS, T), F32),
                        pltpu.VMEM((ATT_ACC_ROWS, T), F32)],
        compiler_params=_cparams(("parallel", "parallel", "arbitrary")),
        name="diff_attention",
    )(proj, proj, proj, proj, slopes, _attn_corr_table(), dl, norm_g)


def _mlstm_kernel(q_ref, k_ref, v_ref, o_ref_in, z_ref, ps_ref, gb_ref, out_ref, c_state, m_state):
    L = SCAN_CHUNK
    D = MLSTM_HEAD_DIM
    c = pl.program_id(1)

    @pl.when(c == 0)
    def _():
        c_state[...] = jnp.zeros(c_state.shape, F32)
        m_state[...] = jnp.zeros(m_state.shape, F32)

    ones_col = jnp.where(lax.broadcasted_iota(jnp.int32, (L, D), 1) == 0, 1.0, 0.0)
    rep = lambda col: jnp.broadcast_to(col, (col.shape[0], LANES))
    twice = lambda x: jnp.concatenate([x, x], axis=1)

    rows = lax.broadcasted_iota(jnp.int32, (L, L), 0)
    cols = lax.broadcasted_iota(jnp.int32, (L, L), 1)
    causal = rows >= cols
    tril = jnp.where(causal, 1.0, 0.0).astype(BF16)

    raw = ps_ref[0] + gb_ref[...]
    logf = -_softplus(-raw)
    bcum = _cumsum_rows(logf, tril)
    bcum_t = bcum.T
    raw_t = raw.T

    for h in range(MLSTM_HEADS):
        li, lf = LANE_I + h, LANE_F + h
        sl = slice(h * D, (h + 1) * D)
        q = q_ref[0, :, sl]
        ks = k_ref[0, :, sl] * (D ** -0.5)
        v_ext = jnp.concatenate([v_ref[0, :, sl], ones_col], axis=1)
        qb, kb = q.astype(BF16), ks.astype(BF16)

        b_rep = rep(bcum[:, lf:lf + 1])
        ig_rep = rep(raw[:, li:li + 1])
        b_row = bcum_t[lf:lf + 1, :]
        ig_row = raw_t[li:li + 1, :]
        btot = b_rep[L - 1:L, :]
        m_prev = m_state[h:h + 1, :]
        c_prev = c_state[h]

        dmat = jnp.where(causal, twice(b_rep) - b_row + ig_row, NEG_BIG)
        m_inter = b_rep + m_prev
        m_row = jnp.maximum(m_inter, rep(jnp.max(dmat, axis=-1, keepdims=True)))
        wts = jnp.exp(dmat - twice(m_row))
        sw = wts * _dot_nt(qb, kb)
        a_inter = jnp.exp(m_inter - m_row)
        nd = _dot(sw.astype(BF16), v_ext.astype(BF16)) + _dot(qb, c_prev.astype(BF16)) * twice(a_inter)
        hout = nd[:, :D] / jnp.maximum(jnp.abs(rep(nd[:, D:D + 1])), jnp.exp(-m_row))
        gate = jax.nn.sigmoid(o_ref_in[0, :, sl]) * _silu(z_ref[0, :, sl])
        out_ref[0, :, sl] = (gate * hout).astype(out_ref.dtype)

        w_end = btot - b_rep + ig_rep
        m_loc = jnp.max(w_end, axis=0, keepdims=True)
        m_new = jnp.maximum(btot + m_prev, m_loc)
        a_old = jnp.exp(btot + m_prev - m_new)
        e_end = jnp.exp(w_end - m_new)
        c_state[h] = twice(a_old) * c_prev + _dot_tn(kb, (v_ext * twice(e_end)).astype(BF16))
        m_state[h:h + 1, :] = m_new


def _mlstm_call(proj, ps, gate_bias):
    bsz, s, _ = proj.shape
    L = SCAN_CHUNK
    W = MLSTM_WIDTH
    blk0 = P_OFF_C // W
    return pl.pallas_call(
        _mlstm_kernel,
        grid=(bsz, s // L),
        in_specs=[pl.BlockSpec((1, L, W), lambda b, c: (b, c, blk0)),
                  pl.BlockSpec((1, L, W), lambda b, c: (b, c, blk0 + 1)),
                  pl.BlockSpec((1, L, W), lambda b, c: (b, c, blk0 + 2)),
                  pl.BlockSpec((1, L, W), lambda b, c: (b, c, blk0 + 3)),
                  pl.BlockSpec((1, L, W), lambda b, c: (b, c, blk0 + 4)),
                  pl.BlockSpec((1, L, LANES), lambda b, c: (b, c, 0)),
                  pl.BlockSpec(gate_bias.shape, lambda b, c: (0, 0))],
        out_specs=pl.BlockSpec((1, L, W), lambda b, c: (b, c, 0)),
        out_shape=jax.ShapeDtypeStruct((bsz, s, W), BF16),
        scratch_shapes=[pltpu.VMEM((MLSTM_HEADS, MLSTM_HEAD_DIM, 2 * MLSTM_HEAD_DIM), F32),
                        pltpu.VMEM((MLSTM_HEADS, LANES), F32)],
        compiler_params=_cparams(("parallel", "arbitrary")),
        name="mlstm_scan",
    )(proj, proj, proj, proj, proj, ps, gate_bias)


def _merge_kernel(x_ref, oa_ref, ob_ref, oc_ref, wga_ref, wgb_ref, wgc_ref, wa_ref, wb_ref, wc_ref, o_ref):
    x = x_ref[...]
    acc = jax.nn.sigmoid(_dot_nt(x, wga_ref[...])) * _dot(oa_ref[...], wa_ref[...])
    acc = acc + jax.nn.sigmoid(_dot_nt(x, wgb_ref[...])) * _dot(ob_ref[...], wb_ref[...])
    acc = acc + jax.nn.sigmoid(_dot_nt(x, wgc_ref[...])) * _dot(oc_ref[...], wc_ref[...])
    o_ref[...] = acc.astype(o_ref.dtype)


def _merge_call(xb, oa, ob, oc, w_all, w_br, li, tm, tn):
    m = xb.shape[0]
    row = lambda i, j: (i, 0)
    gate_spec = lambda br: pl.BlockSpec(
        (None, tn, D_MODEL), lambda i, j: (li, (W_OFF_GATES + br * D_MODEL) // tn + j, 0))
    return pl.pallas_call(
        _merge_kernel,
        grid=(m // tm, D_MODEL // tn),
        in_specs=[pl.BlockSpec((tm, D_MODEL), row),
                  pl.BlockSpec((tm, SSM_WIDTH), row, pipeline_mode=pl.Buffered(1)),
                  pl.BlockSpec((tm, DIFF_WIDTH), row, pipeline_mode=pl.Buffered(1)),
                  pl.BlockSpec((tm, MLSTM_WIDTH), row, pipeline_mode=pl.Buffered(1)),
                  gate_spec(0), gate_spec(1), gate_spec(2),
                  pl.BlockSpec((None, SSM_WIDTH, tn), lambda i, j: (li, 0, j)),
                  pl.BlockSpec((None, DIFF_WIDTH, tn), lambda i, j: (li, SSM_WIDTH // DIFF_WIDTH, j)),
                  pl.BlockSpec((None, MLSTM_WIDTH, tn),
                               lambda i, j: (li, (SSM_WIDTH + DIFF_WIDTH) // MLSTM_WIDTH, j))],
        out_specs=pl.BlockSpec((tm, tn), lambda i, j: (i, j)),
        out_shape=jax.ShapeDtypeStruct((m, D_MODEL), BF16),
        compiler_params=_cparams(("parallel", "arbitrary")),
        name="gated_merge",
    )(xb, oa, ob, oc, w_all, w_all, w_all, w_br, w_br, w_br)


def _final_kernel(nj, tn, mg_ref, wo_ref, wg_ref, x_ref, p_ref, wple_ref, lng_ref, lnb_ref, pg_ref,
                  o_ref, ob_ref, h_scr, hb_scr, st_scr):
    j = pl.program_id(1)
    tm = h_scr.shape[0]

    @pl.when(j == 0)
    def _():
        st_scr[...] = jnp.zeros(st_scr.shape, F32)

    @pl.when(j < nj)
    def _():
        off = pl.multiple_of(j * tn, tn)
        pre = DEEPNORM_ALPHA * x_ref[...] + _dot(mg_ref[...], wo_ref[...])
        h_scr[:, pl.ds(off, tn)] = pre
        st_scr[:, 0:LANES] += jnp.sum(pre, axis=-1, keepdims=True)

    @pl.when(j == nj)
    def _():
        def norm_rows(r, carry):
            rows = pl.ds(pl.multiple_of(r * NORM_ROWS, NORM_ROWS), NORM_ROWS)
            mu = st_scr[rows, 0:1] * (1.0 / D_MODEL)
            ssq = jnp.zeros((NORM_ROWS, 1), F32)
            for c0 in range(0, D_MODEL, NORM_COLS):
                cen = h_scr[rows, c0:c0 + NORM_COLS] - mu
                ssq = ssq + jnp.sum(cen * cen, axis=-1, keepdims=True)
            rstd = lax.rsqrt(ssq * (1.0 / D_MODEL) + 1e-5)
            for c0 in range(0, D_MODEL, NORM_COLS):
                cols = slice(c0, c0 + NORM_COLS)
                h = (h_scr[rows, cols] - mu) * rstd * lng_ref[:, cols] + lnb_ref[:, cols]
                h_scr[rows, cols] = h
                hb_scr[rows, cols] = h.astype(BF16)
            e = _dot(p_ref[rows, :], wple_ref[...])
            st_scr[rows, LANES:2 * LANES] = jnp.broadcast_to(
                jnp.sum(e * e, axis=-1, keepdims=True), (NORM_ROWS, LANES))
            return carry

        lax.fori_loop(0, tm // NORM_ROWS, norm_rows, 0)

    @pl.when(j >= nj)
    def _():
        off = pl.multiple_of((j - nj) * tn, tn)
        gate = jax.nn.sigmoid(_dot(hb_scr[...], wg_ref[...]))
        rs = lax.rsqrt(st_scr[:, LANES:LANES + 1] * (1.0 / D_MODEL) + 1e-5)
        e = _dot(p_ref[...], wple_ref[:, pl.ds(off, tn)]) * rs * pg_ref[:, pl.ds(off, tn)]
        out = h_scr[:, pl.ds(off, tn)] + gate * e
        o_ref[...] = out
        ob_ref[...] = out.astype(BF16)


def _final_call(merged, w_out, w_pg, x, pb, w_ple, ln_g, ln_b, ple_g, li, tm, tn):
    m = merged.shape[0]
    nj = D_MODEL // tn
    full = lambda i, j: (0, 0)
    out_idx = lambda i, j: (i, jnp.maximum(j - nj, 0))
    return pl.pallas_call(
        functools.partial(_final_kernel, nj, tn),
        grid=(m // tm, 2 * nj),
        in_specs=[pl.BlockSpec((tm, D_MODEL), lambda i, j: (i, 0), pipeline_mode=pl.Buffered(1)),
                  pl.BlockSpec((None, D_MODEL, tn), lambda i, j: (li, 0, jnp.minimum(j, nj - 1))),
                  pl.BlockSpec((None, D_MODEL, tn), lambda i, j: (li, 0, jnp.maximum(j - nj, 0))),
                  pl.BlockSpec((tm, tn), lambda i, j: (i, jnp.minimum(j, nj - 1))),
                  pl.BlockSpec((tm, PLE_DIM), lambda i, j: (i, 0)),
                  pl.BlockSpec((None,) + w_ple.shape[1:], lambda i, j: (li, 0, 0), pipeline_mode=pl.Buffered(1)),
                  pl.BlockSpec(ln_g.shape, full),
                  pl.BlockSpec(ln_b.shape, full),
                  pl.BlockSpec(ple_g.shape, full)],
        out_specs=[pl.BlockSpec((tm, tn), out_idx), pl.BlockSpec((tm, tn), out_idx)],
        out_shape=[jax.ShapeDtypeStruct((m, D_MODEL), F32), jax.ShapeDtypeStruct((m, D_MODEL), BF16)],
        scratch_shapes=[pltpu.VMEM((tm, D_MODEL), F32),
                        pltpu.VMEM((tm, D_MODEL), BF16),
                        pltpu.VMEM((tm, 2 * LANES), F32)],
        compiler_params=_cparams(("parallel", "arbitrary")),
        name="outproj_deepnorm_ple",
    )(merged, w_out, w_pg, x, pb, w_ple, ln_g, ln_b, ple_g)


def _row_tile(m, pref):
    return pref if m % pref == 0 else m


def _relayout_kernel(a_ref, dt_ref, if_ref, o_ref, os_ref):
    o_ref[...] = a_ref[0].astype(BF16)

    @pl.when(pl.program_id(1) == 0)
    def _():
        os_ref[LANE_DT:LANE_I, :] = dt_ref[0].astype(BF16)
        os_ref[LANE_I:LANE_F + MLSTM_HEADS, :] = if_ref[0].astype(BF16)
        os_ref[LANE_F + MLSTM_HEADS:, :] = jnp.zeros((LANES - LANE_F - MLSTM_HEADS, os_ref.shape[1]), BF16)


def _relayout_call(w_t, rows):
    depth, _, d = w_t.shape

    def src_row(j):
        r = j * rows
        shift = jnp.where(r < P_OFF_QB, 0, jnp.where(r < W_OFF_GATES, (OFF_QB - P_OFF_QB) // F32_SUBLANES,
                                                     (OFF_GATES - W_OFF_GATES) // F32_SUBLANES))
        return (j * (rows // F32_SUBLANES) + shift) * F32_SUBLANES

    return pl.pallas_call(
        _relayout_kernel,
        grid=(depth, W_OFF_SMALL // rows),
        in_specs=[pl.BlockSpec((pl.Element(1), pl.Element(rows), pl.Element(d)), lambda l, j: (l, src_row(j), 0)),
                  pl.BlockSpec((pl.Element(1), pl.Element(SSM_HEADS), pl.Element(d)), lambda l, j: (l, OFF_DT, 0)),
                  pl.BlockSpec((pl.Element(1), pl.Element(2 * MLSTM_HEADS), pl.Element(d)),
                               lambda l, j: (l, OFF_IC, 0))],
        out_specs=[pl.BlockSpec((None, rows, d), lambda l, j: (l, j, 0)),
                   pl.BlockSpec((None, LANES, d), lambda l, j: (l, 0, 0))],
        out_shape=[jax.ShapeDtypeStruct((depth, W_OFF_SMALL, d), BF16),
                   jax.ShapeDtypeStruct((depth, LANES, d), BF16)],
        compiler_params=_cparams(("parallel", "arbitrary")),
        name="weight_relayout",
    )(w_t, w_t, w_t)


def _prep_weights(w_in, w_branch, w_out, w_ple, w_ple_gate):
    w_main, w_small = _relayout_call(jnp.swapaxes(w_in, 1, 2), RELAYOUT_ROWS)
    return (w_main, w_small, w_branch.astype(BF16), w_out.astype(BF16), w_ple.astype(BF16),
            w_ple_gate.astype(BF16))


def _layer(x, xb, p_i, li, weights, conv_w, conv_b, dt_bias, a_log, d_skip, ssm_norm_g,
           diff_lambda, diff_norm_g, mlstm_gate_b, ln_g, ln_b, ple_norm_g):
    w_all, w_small, w_br, w_out, w_ple, w_pg = weights
    bsz, s, _ = x.shape
    t = bsz * s
    x2 = x.reshape(t, D_MODEL)
    xb2 = xb.reshape(t, D_MODEL)

    tm = _row_tile(t, ROW_TILE)
    proj, ps = _proj_call(xb2, w_all, w_small, li, tm, PROJ_COL_TILE)
    proj = proj.reshape(bsz, s, PROJ_WIDTH)
    ps = ps.reshape(bsz, s, LANES)

    pad = lambda v, lane0: jnp.zeros((1, LANES), F32).at[0, lane0:lane0 + v.shape[0]].set(v.astype(F32))
    out_a = _ssd_call(proj, ps, conv_w, conv_b.reshape(1, -1), pad(dt_bias, LANE_DT), pad(a_log, LANE_DT),
                      jnp.repeat(d_skip.astype(F32), SSM_HEAD_DIM).reshape(1, -1), ssm_norm_g.reshape(1, -1))

    lam_init = 0.8 - 0.6 * math.exp(-0.3 * li)
    slopes = 2.0 ** (-8.0 * jnp.arange(1, DIFF_HEADS + 1, dtype=F32) / DIFF_HEADS)
    slopes = jnp.broadcast_to(slopes[:, None, None], (DIFF_HEADS, 1, LANES))
    out_b = _attn_call(proj, slopes, diff_lambda, diff_norm_g.reshape(1, -1), lam_init)

    gate_bias = pad(mlstm_gate_b[0], LANE_I) + pad(mlstm_gate_b[1], LANE_F)
    out_c = _mlstm_call(proj, ps, gate_bias)

    merged = _merge_call(xb2, out_a.reshape(t, -1), out_b.reshape(t, -1), out_c.reshape(t, -1),
                         w_all, w_br, li, tm, EPILOGUE_COL_TILE)
    out, out_b16 = _final_call(merged, w_out, w_pg, x2, p_i.reshape(t, PLE_DIM).astype(BF16), w_ple,
                               ln_g.reshape(1, -1), ln_b.reshape(1, -1), ple_norm_g.reshape(1, -1),
                               li, tm, EPILOGUE_COL_TILE)
    return out.reshape(bsz, s, D_MODEL), out_b16.reshape(bsz, s, D_MODEL)


def kernel(x, p, w_in, conv_w, conv_b, dt_bias, a_log, d_skip, ssm_norm_g, diff_lambda, diff_norm_g,
           mlstm_gate_b, w_branch, w_out, ln_g, ln_b, w_ple, ple_norm_g, w_ple_gate):
    assert x.shape[-1] == D_MODEL and w_in.shape == (DEPTH, D_MODEL, OFF_GATES + N_BRANCH * D_MODEL)
    weights = _prep_weights(w_in, w_branch, w_out, w_ple, w_ple_gate)
    h = x
    hb = x.astype(BF16)
    for li in range(w_in.shape[0]):
        h, hb = _layer(h, hb, p[li], li, weights, conv_w[li], conv_b[li], dt_bias[li], a_log[li],
                       d_skip[li], ssm_norm_g[li], diff_lambda[li], diff_norm_g[li],
                       mlstm_gate_b[li], ln_g[li], ln_b[li], ple_norm_g[li])
    return h
```
